```python
import jax, jax.numpy as jnp
from jax import lax
import numpy as np

D_MODEL = 2048
BATCH = 4
SEQ = 2048
DEPTH = 2
DEC_BATCH = 8
DEC_SEQ = 32
PAST_LEN = 4096

CHUNK = 64
Q_BLOCK = 128
MLA_HEADS = 16
QK_NOPE = 128
QK_ROPE = 64
QK_HEAD = QK_NOPE + QK_ROPE
V_HEAD = 128
Q_LORA = 512
KV_LORA = 512
ROPE_THETA = 10000.0
ATTN_SCALE = QK_HEAD ** -0.5
HG_HEADS = 16
HG_DK = 128
HG_DV = D_MODEL // HG_HEADS
HG_WIDTH_K = HG_HEADS * HG_DK
HG_WIDTH_V = HG_HEADS * HG_DV
D_FF = 4 * D_MODEL
EPS = 1e-6
SPLIT_SIZES = (Q_LORA, KV_LORA, QK_ROPE, HG_WIDTH_K, HG_WIDTH_K, HG_WIDTH_V, HG_WIDTH_V, D_MODEL, D_MODEL)
D_IN = Q_LORA + KV_LORA + QK_ROPE + 2 * HG_WIDTH_K + 2 * HG_WIDTH_V + 2 * D_MODEL

kernel_name = "mla_hgrn2_gated_streaming_encoder"


def rmsnorm(x, g):
    xf = x.astype(jnp.float32)
    y = xf * lax.rsqrt(jnp.mean(xf * xf, axis=-1, keepdims=True) + EPS)
    return (y * g.astype(jnp.float32)).astype(x.dtype)


def rope(x, pos):
    half = QK_ROPE // 2
    inv = ROPE_THETA ** (-jnp.arange(half, dtype=jnp.float32) / half)
    ang = pos.astype(jnp.float32)[:, None] * inv[None, :]
    cos = jnp.cos(ang)[None, :, None, :]
    sin = jnp.sin(ang)[None, :, None, :]
    xf = x.astype(jnp.float32)
    x1, x2 = xf[..., :half], xf[..., half:]
    return jnp.concatenate([x1 * cos - x2 * sin, x1 * sin + x2 * cos], axis=-1).astype(x.dtype)


def split_in(z):
    idx = np.cumsum(np.array(SPLIT_SIZES))[:-1].tolist()
    return jnp.split(z, idx, axis=-1)


def mla_core(q_abs, q_pe, ckv, kpe, mask):
    s = (jnp.einsum('bshc,btc->bhst', q_abs, ckv, preferred_element_type=jnp.float32)
         + jnp.einsum('bshr,btr->bhst', q_pe, kpe, preferred_element_type=jnp.float32)) * ATTN_SCALE
    if mask is not None:
        s = jnp.where(mask[None, None], s, -jnp.inf)
    p = jax.nn.softmax(s, axis=-1).astype(ckv.dtype)
    return jnp.einsum('bhst,btc->bshc', p, ckv)


def mla_prompt(q_abs, q_pe, ckv, kpe):
    B, S, H, C = q_abs.shape
    nb = S // Q_BLOCK
    key_chunk = jnp.arange(S, dtype=jnp.int32) // CHUNK

    def block(args):
        j, qa, qp = args
        q_chunk = (j * Q_BLOCK + jnp.arange(Q_BLOCK, dtype=jnp.int32)) // CHUNK
        mask = key_chunk[None, :] <= q_chunk[:, None]
        return mla_core(qa, qp, ckv, kpe, mask)

    qa = q_abs.reshape(B, nb, Q_BLOCK, H, C).transpose(1, 0, 2, 3, 4)
    qp = q_pe.reshape(B, nb, Q_BLOCK, H, QK_ROPE).transpose(1, 0, 2, 3, 4)
    o = lax.map(block, (jnp.arange(nb, dtype=jnp.int32), qa, qp))
    return o.transpose(1, 0, 2, 3, 4).reshape(B, S, H, C)


def hgrn_inputs(hq, hf, hi, lb):
    B, S, _ = hq.shape
    z = hf.astype(jnp.float32)
    logf = jnp.logaddexp(jnp.log(lb), jnp.log1p(-lb) + jax.nn.log_sigmoid(z))
    k = (1.0 - lb) * jax.nn.sigmoid(-z)
    q = jax.nn.silu(hq.astype(jnp.float32))
    v = hi.astype(jnp.float32)

    def heads(t, d):
        return t.reshape(B, S, HG_HEADS, d).transpose(0, 2, 1, 3)
    return heads(q, HG_DK), heads(k, HG_DK), heads(v, HG_DV), heads(logf, HG_DK)


def hgrn_chunk(S0, q, k, v, logf):
    L = q.shape[2]
    b = jnp.cumsum(logf, axis=2)
    causal = jnp.tril(jnp.ones((L, L), dtype=bool))
    diff = b[:, :, :, None, :] - b[:, :, None, :, :]
    decay = jnp.exp(jnp.where(causal[None, None, :, :, None], diff, -jnp.inf))
    attn = jnp.einsum('bhtk,bhsk,bhtsk->bhts', q, k, decay)
    o = (jnp.einsum('bhts,bhsv->bhtv', attn, v)
         + jnp.einsum('bhtk,bhkv->bhtv', q * jnp.exp(b), S0))
    b_last = b[:, :, -1:, :]
    S_new = (jnp.exp(b_last[:, :, 0, :])[..., None] * S0
             + jnp.einsum('bhsk,bhsv->bhkv', k * jnp.exp(b_last - b), v))
    return o, S_new


def hgrn_prompt(q, k, v, logf):
    B, H, S, _ = q.shape
    nc = S // CHUNK

    def to_chunks(t):
        return t.reshape(B, H, nc, CHUNK, t.shape[-1]).transpose(2, 0, 1, 3, 4)

    def step(S_c, xs):
        o, S_n = hgrn_chunk(S_c, *xs)
        return S_n, o

    S0 = jnp.zeros((B, H, HG_DK, HG_DV), jnp.float32)
    S_fin, o = lax.scan(step, S0, (to_chunks(q), to_chunks(k), to_chunks(v), to_chunks(logf)))
    return o.transpose(1, 2, 0, 3, 4).reshape(B, H, S, HG_DV), S_fin


def layer_forward(x, pos, p, lb, cache_ckv, cache_kpe, state):
    B, S, _ = x.shape
    h = rmsnorm(x, p['pre_mix_g'])
    q_lat, kv_lat, k_pe, hq, hf, hi, hg, ga, gb = split_in(h @ p['w_in'])
    q = (rmsnorm(q_lat, p['q_norm_g']) @ p['w_uq']).reshape(B, S, MLA_HEADS, QK_HEAD)
    q_pe = rope(q[..., QK_NOPE:], pos)
    q_abs = jnp.einsum('bshn,chn->bshc', q[..., :QK_NOPE], p['w_uk'])
    c_kv = rmsnorm(kv_lat, p['kv_norm_g'])
    k_rot = rope(k_pe[:, :, None, :], pos)[:, :, 0, :]
    if cache_ckv is None:
        o_lat = mla_prompt(q_abs, q_pe, c_kv, k_rot)
    else:
        o_lat = mla_core(q_abs, q_pe, jnp.concatenate([cache_ckv, c_kv], axis=1),
                         jnp.concatenate([cache_kpe, k_rot], axis=1), None)
    o_a = jnp.einsum('bshc,chv->bshv', o_lat, p['w_uv']).reshape(B, S, MLA_HEADS * V_HEAD)
    branch_a = o_a @ p['w_oa']
    qh, kh, vh, lfh = hgrn_inputs(hq, hf, hi, lb)
    if state is None:
        o_h, S_fin = hgrn_prompt(qh, kh, vh, lfh)
    else:
        o_h, S_fin = hgrn_chunk(state.astype(jnp.float32), qh, kh, vh, lfh)
    o_h = rmsnorm(o_h.transpose(0, 2, 1, 3).astype(x.dtype), p['hg_norm_g'])
    o_h = o_h * jax.nn.silu(hg.reshape(B, S, HG_HEADS, HG_DV))
    branch_b = o_h.reshape(B, S, HG_WIDTH_V) @ p['w_ob']
    mix = (jax.nn.sigmoid(ga) * branch_a + jax.nn.sigmoid(gb) * branch_b) @ p['w_out']
    x = x + rmsnorm(mix, p['post_mix_g'])
    h2 = rmsnorm(x, p['pre_mlp_g'])
    m = jnp.square(jax.nn.relu(h2 @ p['w_up'])) @ p['w_down']
    x = x + rmsnorm(m, p['post_mlp_g'])
    return x, c_kv, k_rot, S_fin.astype(x.dtype)


def setup_inputs(seed: int = 0) -> dict:
    key = jax.random.key(seed)
    ks = jax.random.split(key, 24)
    f32 = jnp.float32

    def nrm(k, shape, scale):
        return jax.random.normal(k, shape, f32) * scale

    def gain(k, n):
        return 1.0 + 0.01 * jax.random.normal(k, (DEPTH, n), f32)

    return {
        "x_prompt": nrm(ks[0], (BATCH, SEQ, D_MODEL), 1.0),
        "x_sample": nrm(ks[1], (DEC_BATCH, DEC_SEQ, D_MODEL), 1.0),
        "cache_ckv": nrm(ks[2], (DEPTH, DEC_BATCH, PAST_LEN, KV_LORA), 1.0),
        "cache_kpe": nrm(ks[3], (DEPTH, DEC_BATCH, PAST_LEN, QK_ROPE), 1.0),
        "state_hgrn": nrm(ks[4], (DEPTH, DEC_BATCH, HG_HEADS, HG_DK, HG_DV), 0.5),
        "pre_mix_g": gain(ks[5], D_MODEL),
        "w_in": nrm(ks[6], (DEPTH, D_MODEL, D_IN), D_MODEL ** -0.5),
        "q_norm_g": gain(ks[7], Q_LORA),
        "w_uq": nrm(ks[8], (DEPTH, Q_LORA, MLA_HEADS * QK_HEAD), Q_LORA ** -0.5),
        "kv_norm_g": gain(ks[9], KV_LORA),
        "w_uk": nrm(ks[10], (DEPTH, KV_LORA, MLA_HEADS, QK_NOPE), KV_LORA ** -0.5),
        "w_uv": nrm(ks[11], (DEPTH, KV_LORA, MLA_HEADS, V_HEAD), KV_LORA ** -0.5),
        "w_oa": nrm(ks[12], (DEPTH, MLA_HEADS * V_HEAD, D_MODEL), (MLA_HEADS * V_HEAD) ** -0.5),
        "hg_lb": nrm(ks[13], (DEPTH, HG_WIDTH_K), 1.0),
        "hg_norm_g": gain(ks[14], HG_DV),
        "w_ob": nrm(ks[15], (DEPTH, HG_WIDTH_V, D_MODEL), HG_WIDTH_V ** -0.5),
        "w_out": nrm(ks[16], (DEPTH, D_MODEL, D_MODEL), D_MODEL ** -0.5),
        "post_mix_g": gain(ks[17], D_MODEL),
        "pre_mlp_g": gain(ks[18], D_MODEL),
        "w_up": nrm(ks[19], (DEPTH, D_MODEL, D_FF), D_MODEL ** -0.5),
        "w_down": nrm(ks[20], (DEPTH, D_FF, D_MODEL), D_FF ** -0.5),
        "post_mlp_g": gain(ks[21], D_MODEL),
    }


def reference(x_prompt, x_sample, cache_ckv, cache_kpe, state_hgrn,
              pre_mix_g, w_in, q_norm_g, w_uq, kv_norm_g, w_uk, w_uv, w_oa,
              hg_lb, hg_norm_g, w_ob, w_out, post_mix_g, pre_mlp_g, w_up, w_down, post_mlp_g):
    cs = jnp.cumsum(jax.nn.softmax(hg_lb.astype(jnp.float32), axis=0), axis=0)
    lbs = cs - cs[0:1]
    past = cache_ckv.shape[2]
    pos_p = jnp.arange(x_prompt.shape[1], dtype=jnp.int32)
    pos_s = past + jnp.arange(x_sample.shape[1], dtype=jnp.int32)
    yp, ys = x_prompt, x_sample
    ckv_p, kpe_p, st_p, ckv_s, kpe_s, st_s = [], [], [], [], [], []
    for l in range(DEPTH):
        p = dict(pre_mix_g=pre_mix_g[l], w_in=w_in[l], q_norm_g=q_norm_g[l], w_uq=w_uq[l],
                 kv_norm_g=kv_norm_g[l], w_uk=w_uk[l], w_uv=w_uv[l], w_oa=w_oa[l],
                 hg_norm_g=hg_norm_g[l], w_ob=w_ob[l], w_out=w_out[l], post_mix_g=post_mix_g[l],
                 pre_mlp_g=pre_mlp_g[l], w_up=w_up[l], w_down=w_down[l], post_mlp_g=post_mlp_g[l])
        yp, c1, k1, s1 = layer_forward(yp, pos_p, p, lbs[l], None, None, None)
        ys, c2, k2, s2 = layer_forward(ys, pos_s, p, lbs[l], cache_ckv[l], cache_kpe[l], state_hgrn[l])
        ckv_p.append(c1); kpe_p.append(k1); st_p.append(s1)
        ckv_s.append(c2); kpe_s.append(k2); st_s.append(s2)
    return (yp, ys,
            jnp.stack(ckv_p), jnp.stack(kpe_p), jnp.stack(st_p),
            jnp.stack(ckv_s), jnp.stack(kpe_s), jnp.stack(st_s))
```

```python
import functools

import jax
import jax.numpy as jnp
import numpy as np
from jax import lax
from jax.experimental import pallas as pl
from jax.experimental.pallas import tpu as pltpu

F32 = jnp.float32
BF16 = jnp.bfloat16

CHUNK = 64
MLA_HEADS = 16
QK_NOPE = 128
QK_ROPE = 64
QK_HEAD = QK_NOPE + QK_ROPE
V_HEAD = 128
Q_LORA = 512
KV_LORA = 512
ROPE_THETA = 10000.0
ATTN_SCALE = QK_HEAD ** -0.5
HG_HEADS = 16
HG_DK = 128
HG_DV = 128
EPS = 1e-6

LANES = 128
QK_PAD = 2 * LANES
NEG_BIG = -1e30
VMEM_LIMIT = 56 * 1024 * 1024


def _params(*sem):
    return pltpu.CompilerParams(dimension_semantics=sem, vmem_limit_bytes=VMEM_LIMIT)


def _rms(x, g):
    return x * lax.rsqrt(jnp.mean(x * x, axis=-1, keepdims=True) + EPS) * g


def _dot(a, b):
    return jnp.dot(a, b, preferred_element_type=F32)


def _dot_nt(a, b):
    return lax.dot_general(a, b, (((1,), (1,)), ((), ())), preferred_element_type=F32)


def _dot_tn(a, b):
    return lax.dot_general(a, b, (((0,), (0,)), ((), ())), preferred_element_type=F32)


def _rmsnorm_body(x_ref, g_ref, o_ref):
    o_ref[...] = _rms(x_ref[...], g_ref[...]).astype(o_ref.dtype)


def _rmsnorm(x, g, tm):
    rows, d = x.shape
    return pl.pallas_call(
        _rmsnorm_body,
        grid=(rows // tm,),
        in_specs=[pl.BlockSpec((tm, d), lambda i: (i, 0)),
                  pl.BlockSpec((1, d), lambda i: (0, 0))],
        out_specs=pl.BlockSpec((tm, d), lambda i: (i, 0)),
        out_shape=jax.ShapeDtypeStruct((rows, d), BF16),
        compiler_params=_params("parallel"),
        name="rmsnorm",
    )(x, g.reshape(1, d))


def _mm_body(x_ref, w_ref, *rest, epi, n_extra):
    extra = rest[:n_extra]
    outs = rest[n_extra:]
    acc = _dot(x_ref[...], w_ref[...])
    res = epi(acc, *[e[...] for e in extra])
    for o_ref, r in zip(outs, res):
        o_ref[...] = r.astype(o_ref.dtype)


def _mm(x, w, epi, outs, *, tm, tn, rows=None, col_args=(), tile_args=(), name):
    k = x.shape[1]
    rows = x.shape[0] if rows is None else rows
    n = w.shape[1]
    grid = (n // tn, rows // tm)
    in_specs = [pl.BlockSpec((tm, k), lambda j, i: (i, 0)),
                pl.BlockSpec((k, tn), lambda j, i: (0, j))]
    args = [x, w]
    for arr, width in col_args:
        in_specs.append(pl.BlockSpec((arr.shape[0], width), lambda j, i: (0, j)))
        args.append(arr)
    for arr, width, follows_n in tile_args:
        if follows_n:
            in_specs.append(pl.BlockSpec((tm, width), lambda j, i: (i, j)))
        else:
            in_specs.append(pl.BlockSpec((tm, width), lambda j, i: (i, 0)))
        args.append(arr)
    out_specs = [pl.BlockSpec((tm, width), lambda j, i: (i, j)) for width, _ in outs]
    out_shape = [jax.ShapeDtypeStruct((rows, width * (n // tn)), dt) for width, dt in outs]
    res = pl.pallas_call(
        functools.partial(_mm_body, epi=epi, n_extra=len(col_args) + len(tile_args)),
        grid=grid,
        in_specs=in_specs,
        out_specs=out_specs,
        out_shape=out_shape,
        compiler_params=_params("parallel", "parallel"),
        name=name,
    )(*args)
    return res


def _silu(z):
    return z * jax.nn.sigmoid(z)


def _epi_silu(acc):
    return (_silu(acc),)


def _epi_sigmoid(acc):
    return (jax.nn.sigmoid(acc),)


def _epi_id(acc):
    return (acc,)


def _epi_relu2(acc):
    r = jnp.maximum(acc, 0.0)
    return (r * r,)


def _epi_forget(acc, lb_logits, *, layer):
    mx = jnp.max(lb_logits, axis=0, keepdims=True)
    e = jnp.exp(lb_logits - mx)
    sm = e / jnp.sum(e, axis=0, keepdims=True)
    lb = jnp.zeros_like(mx)
    for i in range(1, layer + 1):
        lb = lb + sm[i:i + 1]
    z = acc
    log_sig = jnp.minimum(z, 0.0) - jnp.log1p(jnp.exp(-jnp.abs(z)))
    a = jnp.log(lb)
    c = jnp.log1p(-lb) + log_sig
    hi = jnp.maximum(a, c)
    lo = jnp.minimum(a, c)
    logf = hi + jnp.log1p(jnp.exp(lo - hi))
    kk = (1.0 - lb) * jax.nn.sigmoid(-z)
    return kk, logf


def _epi_q(acc, cos_t, sin_t):
    n0 = acc[:, 0 * LANES:1 * LANES] * ATTN_SCALE
    p0 = acc[:, 1 * LANES:2 * LANES] * cos_t + acc[:, 4 * LANES:5 * LANES] * sin_t
    n1 = acc[:, 2 * LANES:3 * LANES] * ATTN_SCALE
    p1 = acc[:, 3 * LANES:4 * LANES] * cos_t + acc[:, 5 * LANES:6 * LANES] * sin_t
    return (jnp.concatenate([n0, p0, n1, p1], axis=1),)


def _epi_kv(acc, krot):
    krot = krot.astype(F32)
    keys = jnp.concatenate([acc[:, 0:LANES], krot, acc[:, LANES:2 * LANES], krot], axis=1)
    return keys, acc[:, 2 * LANES:4 * LANES]


def _latent_body(h_ref, w_ref, gq_ref, gkv_ref, cos_ref, sin_ref,
                 qn_ref, ckv_ref, ckvb_ref, krot_ref, krotb_ref):
    acc = _dot(h_ref[...], w_ref[...])
    qn_ref[...] = _rms(acc[:, :Q_LORA], gq_ref[...]).astype(qn_ref.dtype)
    ckv = _rms(acc[:, Q_LORA:Q_LORA + KV_LORA], gkv_ref[...])
    ckv_ref[...] = ckv
    ckvb_ref[...] = ckv.astype(ckvb_ref.dtype)
    base = Q_LORA + KV_LORA
    kr = acc[:, base:base + LANES] * cos_ref[...] + acc[:, base + LANES:base + 2 * LANES] * sin_ref[...]
    krot_ref[...] = kr[:, :QK_ROPE]
    krotb_ref[...] = kr.astype(krotb_ref.dtype)


def _latent(h, w_a, gq, gkv, cos_t, sin_t, tm):
    rows, d = h.shape
    n = w_a.shape[1]
    row = lambda width: pl.BlockSpec((tm, width), lambda i: (i, 0))
    const = lambda r, width: pl.BlockSpec((r, width), lambda i: (0, 0))
    return pl.pallas_call(
        _latent_body,
        grid=(rows // tm,),
        in_specs=[row(d), const(d, n), const(1, Q_LORA), const(1, KV_LORA), row(LANES), row(LANES)],
        out_specs=[row(Q_LORA), row(KV_LORA), row(KV_LORA), row(QK_ROPE), row(LANES)],
        out_shape=[jax.ShapeDtypeStruct((rows, Q_LORA), BF16),
                   jax.ShapeDtypeStruct((rows, KV_LORA), F32),
                   jax.ShapeDtypeStruct((rows, KV_LORA), BF16),
                   jax.ShapeDtypeStruct((rows, QK_ROPE), F32),
                   jax.ShapeDtypeStruct((rows, LANES), BF16)],
        compiler_params=_params("parallel"),
        name="latent_proj",
    )(h, w_a, gq.reshape(1, -1), gkv.reshape(1, -1), cos_t, sin_t)


def _attn_body(q_ref, k_ref, v_ref, o_ref, *, tq):
    qi = pl.program_id(2)
    q = q_ref[...]

    def update(carry, s, v):
        m, l, acc = carry
        m_new = jnp.maximum(m, jnp.max(s, axis=-1, keepdims=True))
        alpha = jnp.exp(m - m_new)
        p = jnp.exp(s - m_new)
        l = alpha * l + jnp.sum(p, axis=-1, keepdims=True)
        acc = alpha * acc + _dot(p.astype(BF16), v)
        return m_new, l, acc

    def step(j, carry):
        off = pl.multiple_of(j * tq, tq)
        s = _dot_nt(q, k_ref[pl.ds(off, tq), :])
        return update(carry, s, v_ref[pl.ds(off, tq), :])

    init = (jnp.full((tq, 1), NEG_BIG, F32), jnp.zeros((tq, 1), F32), jnp.zeros((tq, V_HEAD), F32))
    carry = lax.fori_loop(0, qi, step, init)
    off = pl.multiple_of(qi * tq, tq)
    s = _dot_nt(q, k_ref[pl.ds(off, tq), :])
    r_chunk = lax.broadcasted_iota(jnp.int32, (tq, tq), 0) // CHUNK
    c_chunk = lax.broadcasted_iota(jnp.int32, (tq, tq), 1) // CHUNK
    s = jnp.where(c_chunk <= r_chunk, s, NEG_BIG)
    _, l, acc = update(carry, s, v_ref[pl.ds(off, tq), :])
    o_ref[...] = (acc / l).astype(o_ref.dtype)


def _prompt_attention(q, k, v, batch, seq, tq):
    nq = seq // tq
    return pl.pallas_call(
        functools.partial(_attn_body, tq=tq),
        grid=(batch, MLA_HEADS, nq),
        in_specs=[pl.BlockSpec((tq, QK_PAD), lambda b, h, i: (b * nq + i, h)),
                  pl.BlockSpec((seq, QK_PAD), lambda b, h, i: (b, h)),
                  pl.BlockSpec((seq, V_HEAD), lambda b, h, i: (b, h))],
        out_specs=pl.BlockSpec((tq, V_HEAD), lambda b, h, i: (b * nq + i, h)),
        out_shape=jax.ShapeDtypeStruct((batch * seq, MLA_HEADS * V_HEAD), BF16),
        compiler_params=_params("parallel", "parallel", "arbitrary"),
        name="prompt_attention",
    )(q, k, v)


def _sattn_body(q_ref, cn_ref, kn_ref, cc_ref, ck_ref, wuk_ref, wuv_ref, o_ref, qa_scr, qp_scr,
                *, dec_seq, key_tile):
    for h in range(MLA_HEADS):
        qn = q_ref[:, h * QK_PAD:h * QK_PAD + QK_NOPE]
        qa_scr[h * dec_seq:(h + 1) * dec_seq, :] = _dot(qn, wuk_ref[h]).astype(BF16)
        qp_scr[h * dec_seq:(h + 1) * dec_seq, :] = q_ref[:, h * QK_PAD + QK_NOPE:(h + 1) * QK_PAD]
    qa = qa_scr[...]
    qp = qp_scr[...]
    rows = MLA_HEADS * dec_seq

    def update(carry, s, c):
        m, l, acc = carry
        m_new = jnp.maximum(m, jnp.max(s, axis=-1, keepdims=True))
        alpha = jnp.exp(m - m_new)
        p = jnp.exp(s - m_new)
        l = alpha * l + jnp.sum(p, axis=-1, keepdims=True)
        acc = alpha * acc + _dot(p.astype(BF16), c)
        return m_new, l, acc

    carry = (jnp.full((rows, 1), NEG_BIG, F32), jnp.zeros((rows, 1), F32), jnp.zeros((rows, KV_LORA), F32))
    past = cc_ref.shape[0]
    for t in range(past // key_tile):
        c = cc_ref[t * key_tile:(t + 1) * key_tile, :].astype(BF16)
        kp = ck_ref[t * key_tile:(t + 1) * key_tile, :].astype(BF16)
        s = _dot_nt(qa, c) + _dot_nt(qp[:, :QK_ROPE], kp)
        carry = update(carry, s, c)
    cn = cn_ref[...]
    s = _dot_nt(qa, cn) + _dot_nt(qp, kn_ref[...])
    _, l, acc = update(carry, s, cn)
    o_lat = (acc / l).astype(BF16)
    for h in range(MLA_HEADS):
        o_ref[:, h * V_HEAD:(h + 1) * V_HEAD] = _dot(
            o_lat[h * dec_seq:(h + 1) * dec_seq, :], wuv_ref[h]).astype(o_ref.dtype)


def _sample_attention(q, ckv_b, krot_b, cache_c, cache_k, wuk_t, wuv_h, layer, n_prompt_rows, dec_batch, dec_seq):
    rb = n_prompt_rows // dec_seq
    past = cache_c.shape[2]
    key_tile = min(past, 1024)
    const3 = lambda s: pl.BlockSpec(s, lambda b: (0, 0, 0))
    return pl.pallas_call(
        functools.partial(_sattn_body, dec_seq=dec_seq, key_tile=key_tile),
        grid=(dec_batch,),
        in_specs=[pl.BlockSpec((dec_seq, MLA_HEADS * QK_PAD), lambda b: (rb + b, 0)),
                  pl.BlockSpec((dec_seq, KV_LORA), lambda b: (rb + b, 0)),
                  pl.BlockSpec((dec_seq, LANES), lambda b: (rb + b, 0)),
                  pl.BlockSpec((None, None, past, KV_LORA), lambda b: (layer, b, 0, 0)),
                  pl.BlockSpec((None, None, past, QK_ROPE), lambda b: (layer, b, 0, 0)),
                  const3((MLA_HEADS, QK_NOPE, KV_LORA)),
                  const3((MLA_HEADS, KV_LORA, V_HEAD))],
        out_specs=pl.BlockSpec((dec_seq, MLA_HEADS * V_HEAD), lambda b: (b, 0)),
        out_shape=jax.ShapeDtypeStruct((dec_batch * dec_seq, MLA_HEADS * V_HEAD), BF16),
        scratch_shapes=[pltpu.VMEM((MLA_HEADS * dec_seq, KV_LORA), BF16),
                        pltpu.VMEM((MLA_HEADS * dec_seq, LANES), BF16)],
        compiler_params=_params("parallel"),
        name="sample_attention",
    )(q, ckv_b, krot_b, cache_c, cache_k, wuk_t, wuv_h)


def _hgrn_tables(length):
    t = np.arange(length)[:, None]
    r = np.arange(length)[None, :]
    groups = [(r <= t), (r > t)]
    masks = [(r == t)]
    m = length // 2
    while m >= 1:
        blk = t // (2 * m)
        start2 = blk * 2 * m + m
        second = (t % (2 * m)) >= m
        incl = second & (r >= start2) & (r <= t)
        excl = (~second) & (r > t) & (r < start2)
        groups.append(incl | excl)
        r_blk = r // (2 * m)
        r_first = (r % (2 * m)) < m
        masks.append(second & r_first & (r_blk == blk))
        m //= 2
    return (np.concatenate(groups, axis=0).astype(np.float32),
            np.stack(masks, axis=0).astype(np.float32))


def _hgrn_body(*refs, length, n_chunks, has_state):
    if has_state:
        q_ref, k_ref, lf_ref, v_ref, g_ref, gn_ref, sum_ref, mask_ref, s0_ref, o_ref, sout_ref, st_scr = refs
        st_scr[...] = s0_ref[...].T
    else:
        q_ref, k_ref, lf_ref, v_ref, g_ref, gn_ref, sum_ref, mask_ref, o_ref, sout_ref, st_scr = refs
        st_scr[...] = jnp.zeros_like(st_scr)
    n_lev = mask_ref.shape[0] - 1
    summat = sum_ref[...]
    gn = gn_ref[...]

    def chunk(c, _):
        rows = pl.ds(pl.multiple_of(c * length, length), length)
        q = q_ref[rows, :]
        k = k_ref[rows, :]
        lf = lf_ref[rows, :]
        v = v_ref[rows, :]
        hi = lf.astype(BF16)
        r1 = lf - hi.astype(F32)
        mid = r1.astype(BF16)
        lo = (r1 - mid.astype(F32)).astype(BF16)
        e_all = jnp.exp(_dot(summat, hi) + _dot(summat, mid) + _dot(summat, lo))
        e_q = e_all[0:length]
        e_k = e_all[length:2 * length]
        st = st_scr[...]
        o = _dot_nt((q * e_q).astype(BF16), st.astype(BF16))
        att = mask_ref[0] * _dot_nt(q.astype(BF16), k.astype(BF16))
        for lev in range(n_lev):
            e = e_all[(2 + lev) * length:(3 + lev) * length]
            att = att + mask_ref[lev + 1] * _dot_nt((q * e).astype(BF16), (k * e).astype(BF16))
        o = o + _dot(att.astype(BF16), v)
        st_scr[...] = st * e_q[length - 1:length, :] + _dot_tn(v, (k * e_k).astype(BF16))
        o_ref[rows, :] = (_rms(o, gn) * g_ref[rows, :]).astype(o_ref.dtype)
        return 0

    lax.fori_loop(0, n_chunks, chunk, 0)
    sout_ref[...] = st_scr[...].T


def _hgrn(q, k, logf, v, gate, gn, state0, *, layer, row_block0, n_streams, stream_len, length):
    n_chunks = stream_len // length
    summat, masks = _hgrn_tables(length)
    summat = jnp.asarray(summat, BF16)
    masks = jnp.asarray(masks, F32)
    tok = lambda: pl.BlockSpec((stream_len, HG_DK), lambda n, h: (row_block0 + n, h))
    const2 = lambda a: pl.BlockSpec(a.shape, lambda n, h: (0, 0))
    const3 = lambda a: pl.BlockSpec(a.shape, lambda n, h: (0, 0, 0))
    st_spec = pl.BlockSpec((None, None, HG_DK, HG_DV), lambda n, h: (n, h, 0, 0))
    in_specs = [tok(), tok(), tok(), tok(), tok(), pl.BlockSpec((1, HG_DV), lambda n, h: (0, 0)),
                const2(summat), const3(masks)]
    args = [q, k, logf, v, gate, gn.reshape(1, HG_DV), summat, masks]
    if state0 is not None:
        in_specs.append(pl.BlockSpec((None, None, None, HG_DK, HG_DV), lambda n, h: (layer, n, h, 0, 0)))
        args.append(state0)
    return pl.pallas_call(
        functools.partial(_hgrn_body, length=length, n_chunks=n_chunks, has_state=state0 is not None),
        grid=(n_streams, HG_HEADS),
        in_specs=in_specs,
        out_specs=[pl.BlockSpec((stream_len, HG_DV), lambda n, h: (n, h)), st_spec],
        out_shape=[jax.ShapeDtypeStruct((n_streams * stream_len, HG_HEADS * HG_DV), BF16),
                   jax.ShapeDtypeStruct((n_streams, HG_HEADS, HG_DK, HG_DV), F32)],
        scratch_shapes=[pltpu.VMEM((HG_DV, HG_DK), F32)],
        compiler_params=_params("parallel", "parallel"),
        name="hgrn_state" if state0 is not None else "hgrn_prompt",
    )(*args)


def _merge_body(oa_ref, oh_ref, woa_ref, wob_ref, ga_ref, gb_ref, o_ref):
    a = _dot(oa_ref[...], woa_ref[...])
    b = _dot(oh_ref[...], wob_ref[...])
    o_ref[...] = (ga_ref[...] * a + gb_ref[...] * b).astype(o_ref.dtype)


def _merge(oa, oh, woa, wob, ga, gb, tm, tn):
    rows, k = oa.shape
    n = woa.shape[1]
    x_spec = pl.BlockSpec((tm, k), lambda j, i: (i, 0))
    w_spec = pl.BlockSpec((k, tn), lambda j, i: (0, j))
    t_spec = pl.BlockSpec((tm, tn), lambda j, i: (i, j))
    return pl.pallas_call(
        _merge_body,
        grid=(n // tn, rows // tm),
        in_specs=[x_spec, x_spec, w_spec, w_spec, t_spec, t_spec],
        out_specs=t_spec,
        out_shape=jax.ShapeDtypeStruct((rows, n), BF16),
        compiler_params=_params("parallel", "parallel"),
        name="gated_merge",
    )(oa, oh, woa, wob, ga, gb)


def _outproj_body(m_ref, w_ref, x_ref, g1_ref, g2_ref, x1_ref, h2_ref):
    y = _dot(m_ref[...], w_ref[...])
    x1 = x_ref[...] + _rms(y, g1_ref[...])
    x1_ref[...] = x1
    h2_ref[...] = _rms(x1, g2_ref[...]).astype(h2_ref.dtype)


def _outproj(mix, w, x, g_post, g_pre2, tm):
    rows, d = x.shape
    row = pl.BlockSpec((tm, d), lambda i: (i, 0))
    vec = pl.BlockSpec((1, d), lambda i: (0, 0))
    return pl.pallas_call(
        _outproj_body,
        grid=(rows // tm,),
        in_specs=[row, pl.BlockSpec((d, d), lambda i: (0, 0)), row, vec, vec],
        out_specs=[row, row],
        out_shape=[jax.ShapeDtypeStruct((rows, d), F32), jax.ShapeDtypeStruct((rows, d), BF16)],
        compiler_params=_params("parallel"),
        name="out_proj",
    )(mix, w, x, g_post.reshape(1, d), g_pre2.reshape(1, d))


def _mlp_body(h_ref, wu_ref, wd_ref, x_ref, g_ref, o_ref, acc_ref):
    f = pl.program_id(1)

    @pl.when(f == 0)
    def _():
        acc_ref[...] = jnp.zeros_like(acc_ref)

    u = jnp.maximum(_dot(h_ref[...], wu_ref[...]), 0.0)
    acc_ref[...] += _dot((u * u).astype(BF16), wd_ref[...])

    @pl.when(f == pl.num_programs(1) - 1)
    def _():
        o_ref[...] = x_ref[...] + _rms(acc_ref[...], g_ref[...])


def _mlp(h2, w_up, w_down, x1, g, tm, tf):
    rows, d = x1.shape
    ff = w_up.shape[1]
    row = pl.BlockSpec((tm, d), lambda i, f: (i, 0))
    return pl.pallas_call(
        _mlp_body,
        grid=(rows // tm, ff // tf),
        in_specs=[row, pl.BlockSpec((d, tf), lambda i, f: (0, f)),
                  pl.BlockSpec((tf, d), lambda i, f: (f, 0)), row,
                  pl.BlockSpec((1, d), lambda i, f: (0, 0))],
        out_specs=row,
        out_shape=jax.ShapeDtypeStruct((rows, d), F32),
        scratch_shapes=[pltpu.VMEM((tm, d), F32)],
        compiler_params=_params("parallel", "arbitrary"),
        name="mlp",
    )(h2, w_up, w_down, x1, g.reshape(1, d))


def _pad_lanes(w):
    return jnp.concatenate([w, jnp.zeros(w.shape[:-1] + (LANES - w.shape[-1],), w.dtype)], axis=-1)


def _swap_halves(w):
    half = w.shape[-1] // 2
    return jnp.concatenate([w[..., half:], w[..., :half]], axis=-1)


def _prep_layer(w_in, w_uq, w_uk, w_uv):
    d = w_in.shape[0]
    base = Q_LORA + KV_LORA
    kpe = w_in[:, base:base + QK_ROPE]
    w_a = jnp.concatenate([w_in[:, :base], _pad_lanes(kpe), _pad_lanes(_swap_halves(kpe))], axis=1).astype(BF16)
    off = base + QK_ROPE
    wide = [w_in[:, off + i * d:off + (i + 1) * d].astype(BF16) for i in range(6)]
    uq = w_uq.reshape(Q_LORA, MLA_HEADS // 2, 2, QK_HEAD)
    nope = uq[..., :QK_NOPE]
    pe = _pad_lanes(uq[..., QK_NOPE:])
    pe_sw = _pad_lanes(_swap_halves(uq[..., QK_NOPE:]))
    w_q = jnp.concatenate([nope[:, :, 0], pe[:, :, 0], nope[:, :, 1], pe[:, :, 1],
                           pe_sw[:, :, 0], pe_sw[:, :, 1]], axis=-1)
    w_q = w_q.reshape(Q_LORA, -1).astype(BF16)
    uk = w_uk.reshape(KV_LORA, MLA_HEADS // 2, 2 * QK_NOPE)
    uv = w_uv.reshape(KV_LORA, MLA_HEADS // 2, 2 * V_HEAD)
    w_kv = jnp.concatenate([uk, uv], axis=-1).reshape(KV_LORA, -1).astype(BF16)
    wuk_t = jnp.transpose(w_uk, (1, 2, 0)).astype(BF16)
    wuv_h = jnp.transpose(w_uv, (1, 0, 2)).astype(BF16)
    return w_a, wide, w_q, w_kv, wuk_t, wuv_h


def _rope_tables(positions):
    half = QK_ROPE // 2
    inv = ROPE_THETA ** (-jnp.arange(half, dtype=F32) / half)
    ang = positions.astype(F32)[:, None] * inv[None, :]
    cos, sin = jnp.cos(ang), jnp.sin(ang)
    zeros = jnp.zeros((positions.shape[0], LANES - QK_ROPE), F32)
    return (jnp.concatenate([cos, cos, zeros], axis=1), jnp.concatenate([-sin, sin, zeros], axis=1))


def kernel(x_prompt, x_sample, cache_ckv, cache_kpe, state_hgrn, pre_mix_g, w_in, q_norm_g, w_uq, kv_norm_g,
           w_uk, w_uv, w_oa, hg_lb, hg_norm_g, w_ob, w_out, post_mix_g, pre_mlp_g, w_up, w_down, post_mlp_g):
    batch, seq, d = x_prompt.shape
    dec_batch, dec_seq, _ = x_sample.shape
    depth = w_in.shape[0]
    past = cache_ckv.shape[2]
    n_p = batch * seq
    n_s = dec_batch * dec_seq
    rows = n_p + n_s
    tm = 768
    assert rows % tm == 0 and n_p % 256 == 0

    x = jnp.concatenate([x_prompt.reshape(n_p, d), x_sample.reshape(n_s, d)], axis=0)
    pos = jnp.concatenate([jnp.tile(jnp.arange(seq, dtype=jnp.int32), batch),
                           jnp.tile(past + jnp.arange(dec_seq, dtype=jnp.int32), dec_batch)])
    cos_t, sin_t = _rope_tables(pos)
    cos_q, sin_q = cos_t * ATTN_SCALE, sin_t * ATTN_SCALE

    ckv_out, kpe_out, st_p_out, st_s_out = [], [], [], []
    for l in range(depth):
        w_a, (w_hq, w_hf, w_hi, w_hg, w_ga, w_gb), w_q, w_kv, wuk_t, wuv_h = _prep_layer(
            w_in[l], w_uq[l], w_uk[l], w_uv[l])
        h = _rmsnorm(x, pre_mix_g[l], tm)
        qn, ckv, ckv_b, krot, krot_b = _latent(h, w_a, q_norm_g[l], kv_norm_g[l], cos_t, sin_t, tm)
        wide = functools.partial(_mm, h, tm=tm, tn=1024)
        (hq,) = wide(w_hq, _epi_silu, [(1024, F32)], name="proj_hq")
        hk, logf = wide(w_hf, functools.partial(_epi_forget, layer=l), [(1024, F32), (1024, F32)],
                        col_args=[(hg_lb, 1024)], name="proj_hf")
        (hv,) = wide(w_hi, _epi_id, [(1024, BF16)], name="proj_hi")
        (hgate,) = wide(w_hg, _epi_silu, [(1024, F32)], name="proj_hg")
        (ga,) = wide(w_ga, _epi_sigmoid, [(1024, F32)], name="proj_ga")
        (gb,) = wide(w_gb, _epi_sigmoid, [(1024, F32)], name="proj_gb")
        (q,) = _mm(qn, w_q, _epi_q, [(2 * QK_PAD, BF16)], tm=tm, tn=6 * LANES,
                   tile_args=[(cos_q, LANES, False), (sin_q, LANES, False)], name="proj_q")
        keys, vals = _mm(ckv_b, w_kv, _epi_kv, [(2 * QK_PAD, BF16), (2 * V_HEAD, BF16)], tm=512, tn=4 * LANES,
                         rows=n_p, tile_args=[(krot_b, LANES, False)], name="proj_kv")
        oa_p = _prompt_attention(q, keys, vals, batch, seq, 256)
        oa_s = _sample_attention(q, ckv_b, krot_b, cache_ckv, cache_kpe, wuk_t, wuv_h, l, n_p, dec_batch, dec_seq)
        oa = jnp.concatenate([oa_p, oa_s], axis=0)
        oh_p, st_p = _hgrn(hq, hk, logf, hv, hgate, hg_norm_g[l], None, layer=l,
                           row_block0=0, n_streams=batch, stream_len=seq, length=CHUNK)
        oh_s, st_s = _hgrn(hq, hk, logf, hv, hgate, hg_norm_g[l], state_hgrn, layer=l,
                           row_block0=n_p // dec_seq, n_streams=dec_batch, stream_len=dec_seq, length=dec_seq)
        oh = jnp.concatenate([oh_p, oh_s], axis=0)
        mix = _merge(oa, oh, w_oa[l].astype(BF16), w_ob[l].astype(BF16), ga, gb, tm, 512)
        x1, h2 = _outproj(mix, w_out[l].astype(BF16), x, post_mix_g[l], pre_mlp_g[l], 384)
        x = _mlp(h2, w_up[l].astype(BF16), w_down[l].astype(BF16), x1, post_mlp_g[l], tm, 512)

        ckv_out.append(ckv)
        kpe_out.append(krot)
        st_p_out.append(st_p)
        st_s_out.append(st_s)

    ckv_all = jnp.stack(ckv_out)
    kpe_all = jnp.stack(kpe_out)
    return (x[:n_p].reshape(batch, seq, d),
            x[n_p:].reshape(dec_batch, dec_seq, d),
            ckv_all[:, :n_p].reshape(depth, batch, seq, KV_LORA),
            kpe_all[:, :n_p].reshape(depth, batch, seq, QK_ROPE),
            jnp.stack(st_p_out),
            ckv_all[:, n_p:].reshape(depth, dec_batch, dec_seq, KV_LORA),
            kpe_all[:, n_p:].reshape(depth, dec_batch, dec_seq, QK_ROPE),
            jnp.stack(st_s_out))
```

```python
import functools

import jax
import jax.numpy as jnp
import numpy as np
from jax import lax
from jax.experimental import pallas as pl
from jax.experimental.pallas import tpu as pltpu

F32 = jnp.float32
BF16 = jnp.bfloat16

CHUNK = 64
MLA_HEADS = 16
QK_NOPE = 128
QK_ROPE = 64
QK_HEAD = QK_NOPE + QK_ROPE
V_HEAD = 128
Q_LORA = 512
KV_LORA = 512
ROPE_THETA = 10000.0
ATTN_SCALE = QK_HEAD ** -0.5
HG_HEADS = 16
HG_DK = 128
HG_DV = 128
EPS = 1e-6

LANES = 128
QK_PAD = 2 * LANES
NEG_BIG = -1e30
VMEM_LIMIT = 56 * 1024 * 1024


def _params(*sem):
    return pltpu.CompilerParams(dimension_semantics=sem, vmem_limit_bytes=VMEM_LIMIT)


def _rms(x, g):
    return x * lax.rsqrt(jnp.mean(x * x, axis=-1, keepdims=True) + EPS) * g


def _dot(a, b):
    return jnp.dot(a, b, preferred_element_type=F32)


def _dot_nt(a, b):
    return lax.dot_general(a, b, (((1,), (1,)), ((), ())), preferred_element_type=F32)


def _dot_tn(a, b):
    return lax.dot_general(a, b, (((0,), (0,)), ((), ())), preferred_element_type=F32)


def _rmsnorm_body(x_ref, g_ref, o_ref):
    o_ref[...] = _rms(x_ref[...], g_ref[...]).astype(o_ref.dtype)


def _rmsnorm(x, g, tm):
    rows, d = x.shape
    return pl.pallas_call(
        _rmsnorm_body,
        grid=(rows // tm,),
        in_specs=[pl.BlockSpec((tm, d), lambda i: (i, 0)),
                  pl.BlockSpec((1, d), lambda i: (0, 0))],
        out_specs=pl.BlockSpec((tm, d), lambda i: (i, 0)),
        out_shape=jax.ShapeDtypeStruct((rows, d), BF16),
        compiler_params=_params("parallel"),
        name="rmsnorm",
    )(x, g.reshape(1, d))


def _mm_body(x_ref, w_ref, *rest, epi, n_extra):
    extra = rest[:n_extra]
    outs = rest[n_extra:]
    acc = _dot(x_ref[...], w_ref[...])
    res = epi(acc, *[e[...] for e in extra])
    for o_ref, r in zip(outs, res):
        o_ref[...] = r.astype(o_ref.dtype)


def _mm(x, w, epi, outs, *, tm, tn, rows=None, col_args=(), tile_args=(), name):
    k = x.shape[1]
    rows = x.shape[0] if rows is None else rows
    n = w.shape[1]
    grid = (n // tn, rows // tm)
    in_specs = [pl.BlockSpec((tm, k), lambda j, i: (i, 0)),
                pl.BlockSpec((k, tn), lambda j, i: (0, j))]
    args = [x, w]
    for arr, width in col_args:
        in_specs.append(pl.BlockSpec((arr.shape[0], width), lambda j, i: (0, j)))
        args.append(arr)
    for arr, width, follows_n in tile_args:
        if follows_n:
            in_specs.append(pl.BlockSpec((tm, width), lambda j, i: (i, j)))
        else:
            in_specs.append(pl.BlockSpec((tm, width), lambda j, i: (i, 0)))
        args.append(arr)
    out_specs = [pl.BlockSpec((tm, width), lambda j, i: (i, j)) for width, _ in outs]
    out_shape = [jax.ShapeDtypeStruct((rows, width * (n // tn)), dt) for width, dt in outs]
    res = pl.pallas_call(
        functools.partial(_mm_body, epi=epi, n_extra=len(col_args) + len(tile_args)),
        grid=grid,
        in_specs=in_specs,
        out_specs=out_specs,
        out_shape=out_shape,
        compiler_params=_params("parallel", "parallel"),
        name=name,
    )(*args)
    return res


def _silu(z):
    return z * jax.nn.sigmoid(z)


def _epi_silu(acc):
    return (_silu(acc),)


def _epi_sigmoid(acc):
    return (jax.nn.sigmoid(acc),)


def _epi_id(acc):
    return (acc,)


def _epi_relu2(acc):
    r = jnp.maximum(acc, 0.0)
    return (r * r,)


def _epi_forget(acc, lb_logits, *, layer):
    mx = jnp.max(lb_logits, axis=0, keepdims=True)
    e = jnp.exp(lb_logits - mx)
    sm = e / jnp.sum(e, axis=0, keepdims=True)
    lb = jnp.zeros_like(mx)
    for i in range(1, layer + 1):
        lb = lb + sm[i:i + 1]
    z = acc
    log_sig = jnp.minimum(z, 0.0) - jnp.log1p(jnp.exp(-jnp.abs(z)))
    a = jnp.log(lb)
    c = jnp.log1p(-lb) + log_sig
    hi = jnp.maximum(a, c)
    lo = jnp.minimum(a, c)
    logf = hi + jnp.log1p(jnp.exp(lo - hi))
    kk = (1.0 - lb) * jax.nn.sigmoid(-z)
    return kk, logf


def _epi_q(acc, cos_t, sin_t):
    n0 = acc[:, 0 * LANES:1 * LANES] * ATTN_SCALE
    p0 = acc[:, 1 * LANES:2 * LANES] * cos_t + acc[:, 4 * LANES:5 * LANES] * sin_t
    n1 = acc[:, 2 * LANES:3 * LANES] * ATTN_SCALE
    p1 = acc[:, 3 * LANES:4 * LANES] * cos_t + acc[:, 5 * LANES:6 * LANES] * sin_t
    return (jnp.concatenate([n0, p0, n1, p1], axis=1),)


def _epi_kv(acc, krot):
    krot = krot.astype(F32)
    keys = jnp.concatenate([acc[:, 0:LANES], krot, acc[:, LANES:2 * LANES], krot], axis=1)
    return keys, acc[:, 2 * LANES:4 * LANES]


def _latent_body(h_ref, w_ref, gq_ref, gkv_ref, cos_ref, sin_ref,
                 qn_ref, ckv_ref, ckvb_ref, krot_ref, krotb_ref):
    acc = _dot(h_ref[...], w_ref[...])
    qn_ref[...] = _rms(acc[:, :Q_LORA], gq_ref[...]).astype(qn_ref.dtype)
    ckv = _rms(acc[:, Q_LORA:Q_LORA + KV_LORA], gkv_ref[...])
    ckv_ref[...] = ckv
    ckvb_ref[...] = ckv.astype(ckvb_ref.dtype)
    base = Q_LORA + KV_LORA
    kr = acc[:, base:base + LANES] * cos_ref[...] + acc[:, base + LANES:base + 2 * LANES] * sin_ref[...]
    krot_ref[...] = kr[:, :QK_ROPE]
    krotb_ref[...] = kr.astype(krotb_ref.dtype)


def _latent(h, w_a, gq, gkv, cos_t, sin_t, tm):
    rows, d = h.shape
    n = w_a.shape[1]
    row = lambda width: pl.BlockSpec((tm, width), lambda i: (i, 0))
    const = lambda r, width: pl.BlockSpec((r, width), lambda i: (0, 0))
    return pl.pallas_call(
        _latent_body,
        grid=(rows // tm,),
        in_specs=[row(d), const(d, n), const(1, Q_LORA), const(1, KV_LORA), row(LANES), row(LANES)],
        out_specs=[row(Q_LORA), row(KV_LORA), row(KV_LORA), row(QK_ROPE), row(LANES)],
        out_shape=[jax.ShapeDtypeStruct((rows, Q_LORA), BF16),
                   jax.ShapeDtypeStruct((rows, KV_LORA), F32),
                   jax.ShapeDtypeStruct((rows, KV_LORA), BF16),
                   jax.ShapeDtypeStruct((rows, QK_ROPE), F32),
                   jax.ShapeDtypeStruct((rows, LANES), BF16)],
        compiler_params=_params("parallel"),
        name="latent_proj",
    )(h, w_a, gq.reshape(1, -1), gkv.reshape(1, -1), cos_t, sin_t)


def _attn_body(q_ref, k_ref, v_ref, o_ref, *, tq, nq):
    qi = pl.program_id(2)
    r_chunk = lax.broadcasted_iota(jnp.int32, (tq, tq), 0) // CHUNK
    c_chunk = lax.broadcasted_iota(jnp.int32, (tq, tq), 1) // CHUNK
    visible = c_chunk <= r_chunk

    for n_past in range(nq):
        @pl.when(qi == n_past)
        def _(n_past=n_past):
            q = q_ref[...]
            lo = n_past * tq
            s_d = jnp.where(visible, _dot_nt(q, k_ref[lo:lo + tq, :]), NEG_BIG)
            m = jnp.max(s_d, axis=-1, keepdims=True)
            if n_past:
                s_p = _dot_nt(q, k_ref[0:lo, :])
                m = jnp.maximum(m, jnp.max(s_p, axis=-1, keepdims=True))
            p_d = jnp.exp(s_d - m)
            l = jnp.sum(p_d, axis=-1, keepdims=True)
            acc = _dot(p_d.astype(BF16), v_ref[lo:lo + tq, :])
            if n_past:
                p_p = jnp.exp(s_p - m)
                l = l + jnp.sum(p_p, axis=-1, keepdims=True)
                acc = acc + _dot(p_p.astype(BF16), v_ref[0:lo, :])
            o_ref[...] = (acc / l).astype(o_ref.dtype)


def _prompt_attention(q, k, v, batch, seq, tq):
    nq = seq // tq
    return pl.pallas_call(
        functools.partial(_attn_body, tq=tq, nq=nq),
        grid=(batch, MLA_HEADS, nq),
        in_specs=[pl.BlockSpec((tq, QK_PAD), lambda b, h, i: (b * nq + i, h)),
                  pl.BlockSpec((seq, QK_PAD), lambda b, h, i: (b, h)),
                  pl.BlockSpec((seq, V_HEAD), lambda b, h, i: (b, h))],
        out_specs=pl.BlockSpec((tq, V_HEAD), lambda b, h, i: (b * nq + i, h)),
        out_shape=jax.ShapeDtypeStruct((batch * seq, MLA_HEADS * V_HEAD), BF16),
        compiler_params=_params("parallel", "parallel", "arbitrary"),
        name="prompt_attention",
    )(q, k, v)


def _sattn_body(q_ref, cn_ref, kn_ref, cc_ref, ck_ref, wuk_ref, wuv_ref, o_ref, qa_scr, qp_scr,
                *, dec_seq, key_tile):
    for h in range(MLA_HEADS):
        qn = q_ref[:, h * QK_PAD:h * QK_PAD + QK_NOPE]
        qa_scr[h * dec_seq:(h + 1) * dec_seq, :] = _dot(qn, wuk_ref[h]).astype(BF16)
        qp_scr[h * dec_seq:(h + 1) * dec_seq, :] = q_ref[:, h * QK_PAD + QK_NOPE:(h + 1) * QK_PAD]
    qa = qa_scr[...]
    qp = qp_scr[...]
    rows = MLA_HEADS * dec_seq

    def update(carry, s, c):
        m, l, acc = carry
        m_new = jnp.maximum(m, jnp.max(s, axis=-1, keepdims=True))
        alpha = jnp.exp(m - m_new)
        p = jnp.exp(s - m_new)
        l = alpha * l + jnp.sum(p, axis=-1, keepdims=True)
        acc = alpha * acc + _dot(p.astype(BF16), c)
        return m_new, l, acc

    carry = (jnp.full((rows, 1), NEG_BIG, F32), jnp.zeros((rows, 1), F32), jnp.zeros((rows, KV_LORA), F32))
    past = cc_ref.shape[0]
    for t in range(past // key_tile):
        c = cc_ref[t * key_tile:(t + 1) * key_tile, :].astype(BF16)
        kp = ck_ref[t * key_tile:(t + 1) * key_tile, :].astype(BF16)
        s = _dot_nt(qa, c) + _dot_nt(qp[:, :QK_ROPE], kp)
        carry = update(carry, s, c)
    cn = cn_ref[...]
    s = _dot_nt(qa, cn) + _dot_nt(qp, kn_ref[...])
    _, l, acc = update(carry, s, cn)
    o_lat = (acc / l).astype(BF16)
    for h in range(MLA_HEADS):
        o_ref[:, h * V_HEAD:(h + 1) * V_HEAD] = _dot(
            o_lat[h * dec_seq:(h + 1) * dec_seq, :], wuv_ref[h]).astype(o_ref.dtype)


def _sample_attention(q, ckv_b, krot_b, cache_c, cache_k, wuk_t, wuv_h, layer, n_prompt_rows, dec_batch, dec_seq):
    rb = n_prompt_rows // dec_seq
    past = cache_c.shape[2]
    key_tile = min(past, 1024)
    const3 = lambda s: pl.BlockSpec(s, lambda b: (0, 0, 0))
    return pl.pallas_call(
        functools.partial(_sattn_body, dec_seq=dec_seq, key_tile=key_tile),
        grid=(dec_batch,),
        in_specs=[pl.BlockSpec((dec_seq, MLA_HEADS * QK_PAD), lambda b: (rb + b, 0)),
                  pl.BlockSpec((dec_seq, KV_LORA), lambda b: (rb + b, 0)),
                  pl.BlockSpec((dec_seq, LANES), lambda b: (rb + b, 0)),
                  pl.BlockSpec((None, None, past, KV_LORA), lambda b: (layer, b, 0, 0)),
                  pl.BlockSpec((None, None, past, QK_ROPE), lambda b: (layer, b, 0, 0)),
                  const3((MLA_HEADS, QK_NOPE, KV_LORA)),
                  const3((MLA_HEADS, KV_LORA, V_HEAD))],
        out_specs=pl.BlockSpec((dec_seq, MLA_HEADS * V_HEAD), lambda b: (b, 0)),
        out_shape=jax.ShapeDtypeStruct((dec_batch * dec_seq, MLA_HEADS * V_HEAD), BF16),
        scratch_shapes=[pltpu.VMEM((MLA_HEADS * dec_seq, KV_LORA), BF16),
                        pltpu.VMEM((MLA_HEADS * dec_seq, LANES), BF16)],
        compiler_params=_params("parallel"),
        name="sample_attention",
    )(q, ckv_b, krot_b, cache_c, cache_k, wuk_t, wuv_h)


def _hgrn_tables(length):
    t = np.arange(length)[:, None]
    r = np.arange(length)[None, :]
    groups = [(r <= t), (r > t)]
    masks = [(r == t)]
    m = length // 2
    while m >= 1:
        blk = t // (2 * m)
        start2 = blk * 2 * m + m
        second = (t % (2 * m)) >= m
        incl = second & (r >= start2) & (r <= t)
        excl = (~second) & (r > t) & (r < start2)
        groups.append(incl | excl)
        r_blk = r // (2 * m)
        r_first = (r % (2 * m)) < m
        masks.append(second & r_first & (r_blk == blk))
        m //= 2
    return (np.concatenate(groups, axis=0).astype(np.float32),
            np.stack(masks, axis=0).astype(np.float32))


def _hgrn_body(*refs, length, n_chunks, group, has_state):
    if has_state:
        q_ref, k_ref, lf_ref, v_ref, g_ref, gn_ref, sum_ref, mask_ref, s0_ref, o_ref, sout_ref, st_scr = refs
    else:
        q_ref, k_ref, lf_ref, v_ref, g_ref, gn_ref, sum_ref, mask_ref, o_ref, sout_ref, st_scr = refs
    tb = pl.program_id(2)

    @pl.when(tb == 0)
    def _():
        for g in range(group):
            st_scr[g] = s0_ref[g].T if has_state else jnp.zeros((HG_DV, HG_DK), F32)

    n_lev = mask_ref.shape[0] - 1
    summat = sum_ref[...]
    gn = gn_ref[...]

    def chunk(c, _):
        rows = pl.ds(pl.multiple_of(c * length, length), length)
        for g in range(group):
            cols = slice(g * HG_DK, (g + 1) * HG_DK)
            q = q_ref[rows, cols]
            k = k_ref[rows, cols]
            lf = lf_ref[rows, cols]
            v = v_ref[rows, cols]
            hi = lf.astype(BF16)
            mid = (lf - hi.astype(F32)).astype(BF16)
            sums = _dot(summat, jnp.concatenate([hi, mid], axis=1))
            e_all = jnp.exp(sums[:, :HG_DK] + sums[:, HG_DK:])
            e_q = e_all[0:length]
            e_k = e_all[length:2 * length]
            st = st_scr[g]
            o = _dot_nt((q * e_q).astype(BF16), st.astype(BF16))
            att = mask_ref[0] * _dot_nt(q.astype(BF16), k.astype(BF16))
            for lev in range(n_lev):
                e = e_all[(2 + lev) * length:(3 + lev) * length]
                att = att + mask_ref[lev + 1] * _dot_nt((q * e).astype(BF16), (k * e).astype(BF16))
            o = o + _dot(att.astype(BF16), v)
            st_scr[g] = st * e_q[length - 1:length, :] + _dot_tn(v, (k * e_k).astype(BF16))
            o_ref[rows, cols] = (_rms(o, gn) * g_ref[rows, cols]).astype(o_ref.dtype)
        return 0

    lax.fori_loop(0, n_chunks, chunk, 0)

    @pl.when(tb == pl.num_programs(2) - 1)
    def _():
        for g in range(group):
            sout_ref[g] = st_scr[g].T


def _hgrn(q, k, logf, v, gate, gn, state0, *, layer, row_block0, n_streams, stream_len, length, block_len, group):
    n_tb = stream_len // block_len
    n_chunks = block_len // length
    width = group * HG_DK
    summat, masks = _hgrn_tables(length)
    summat = jnp.asarray(summat, BF16)
    masks = jnp.asarray(masks, F32)
    tok = lambda: pl.BlockSpec((block_len, width), lambda n, h, t: (row_block0 + n * n_tb + t, h))
    st_spec = pl.BlockSpec((None, group, HG_DK, HG_DV), lambda n, h, t: (n, h, 0, 0))
    in_specs = [tok(), tok(), tok(), tok(), tok(), pl.BlockSpec((1, HG_DV), lambda n, h, t: (0, 0)),
                pl.BlockSpec(summat.shape, lambda n, h, t: (0, 0)),
                pl.BlockSpec(masks.shape, lambda n, h, t: (0, 0, 0))]
    args = [q, k, logf, v, gate, gn.reshape(1, HG_DV), summat, masks]
    if state0 is not None:
        in_specs.append(pl.BlockSpec((None, None, group, HG_DK, HG_DV), lambda n, h, t: (layer, n, h, 0, 0)))
        args.append(state0)
    return pl.pallas_call(
        functools.partial(_hgrn_body, length=length, n_chunks=n_chunks, group=group,
                          has_state=state0 is not None),
        grid=(n_streams, HG_HEADS // group, n_tb),
        in_specs=in_specs,
        out_specs=[pl.BlockSpec((block_len, width), lambda n, h, t: (n * n_tb + t, h)), st_spec],
        out_shape=[jax.ShapeDtypeStruct((n_streams * stream_len, HG_HEADS * HG_DV), BF16),
                   jax.ShapeDtypeStruct((n_streams, HG_HEADS, HG_DK, HG_DV), F32)],
        scratch_shapes=[pltpu.VMEM((group, HG_DV, HG_DK), F32)],
        compiler_params=_params("parallel", "parallel", "arbitrary"),
        name="hgrn_state" if state0 is not None else "hgrn_prompt",
    )(*args)


def _merge_body(oa_ref, oh_ref, woa_ref, wob_ref, ga_ref, gb_ref, o_ref):
    a = _dot(oa_ref[...], woa_ref[...])
    b = _dot(oh_ref[...], wob_ref[...])
    o_ref[...] = (ga_ref[...] * a + gb_ref[...] * b).astype(o_ref.dtype)


def _merge(oa, oh, woa, wob, ga, gb, tm, tn):
    rows, k = oa.shape
    n = woa.shape[1]
    x_spec = pl.BlockSpec((tm, k), lambda j, i: (i, 0))
    w_spec = pl.BlockSpec((k, tn), lambda j, i: (0, j))
    t_spec = pl.BlockSpec((tm, tn), lambda j, i: (i, j))
    return pl.pallas_call(
        _merge_body,
        grid=(n // tn, rows // tm),
        in_specs=[x_spec, x_spec, w_spec, w_spec, t_spec, t_spec],
        out_specs=t_spec,
        out_shape=jax.ShapeDtypeStruct((rows, n), BF16),
        compiler_params=_params("parallel", "parallel"),
        name="gated_merge",
    )(oa, oh, woa, wob, ga, gb)


def _outproj_body(m_ref, w_ref, x_ref, g1_ref, g2_ref, x1_ref, h2_ref):
    y = _dot(m_ref[...], w_ref[...])
    x1 = x_ref[...] + _rms(y, g1_ref[...])
    x1_ref[...] = x1
    h2_ref[...] = _rms(x1, g2_ref[...]).astype(h2_ref.dtype)


def _outproj(mix, w, x, g_post, g_pre2, tm):
    rows, d = x.shape
    row = pl.BlockSpec((tm, d), lambda i: (i, 0))
    vec = pl.BlockSpec((1, d), lambda i: (0, 0))
    return pl.pallas_call(
        _outproj_body,
        grid=(rows // tm,),
        in_specs=[row, pl.BlockSpec((d, d), lambda i: (0, 0)), row, vec, vec],
        out_specs=[row, row],
        out_shape=[jax.ShapeDtypeStruct((rows, d), F32), jax.ShapeDtypeStruct((rows, d), BF16)],
        compiler_params=_params("parallel"),
        name="out_proj",
    )(mix, w, x, g_post.reshape(1, d), g_pre2.reshape(1, d))


def _mlp_body(h_ref, wu_ref, wd_ref, x_ref, g_ref, o_ref, acc_ref):
    f = pl.program_id(1)

    @pl.when(f == 0)
    def _():
        acc_ref[...] = jnp.zeros_like(acc_ref)

    u = jnp.maximum(_dot(h_ref[...], wu_ref[...]), 0.0)
    acc_ref[...] += _dot((u * u).astype(BF16), wd_ref[...])

    @pl.when(f == pl.num_programs(1) - 1)
    def _():
        o_ref[...] = x_ref[...] + _rms(acc_ref[...], g_ref[...])


def _mlp(h2, w_up, w_down, x1, g, tm, tf):
    rows, d = x1.shape
    ff = w_up.shape[1]
    row = pl.BlockSpec((tm, d), lambda i, f: (i, 0))
    return pl.pallas_call(
        _mlp_body,
        grid=(rows // tm, ff // tf),
        in_specs=[row, pl.BlockSpec((d, tf), lambda i, f: (0, f)),
                  pl.BlockSpec((tf, d), lambda i, f: (f, 0)), row,
                  pl.BlockSpec((1, d), lambda i, f: (0, 0))],
        out_specs=row,
        out_shape=jax.ShapeDtypeStruct((rows, d), F32),
        scratch_shapes=[pltpu.VMEM((tm, d), F32)],
        compiler_params=_params("parallel", "arbitrary"),
        name="mlp",
    )(h2, w_up, w_down, x1, g.reshape(1, d))


def _pad_lanes(w):
    return jnp.concatenate([w, jnp.zeros(w.shape[:-1] + (LANES - w.shape[-1],), w.dtype)], axis=-1)


def _swap_halves(w):
    half = w.shape[-1] // 2
    return jnp.concatenate([w[..., half:], w[..., :half]], axis=-1)


def _prep_layer(w_in, w_uq, w_uk, w_uv):
    d = w_in.shape[0]
    base = Q_LORA + KV_LORA
    kpe = w_in[:, base:base + QK_ROPE]
    w_a = jnp.concatenate([w_in[:, :base], _pad_lanes(kpe), _pad_lanes(_swap_halves(kpe))], axis=1).astype(BF16)
    off = base + QK_ROPE
    wide = [w_in[:, off + i * d:off + (i + 1) * d].astype(BF16) for i in range(6)]
    uq = w_uq.reshape(Q_LORA, MLA_HEADS // 2, 2, QK_HEAD)
    nope = uq[..., :QK_NOPE]
    pe = _pad_lanes(uq[..., QK_NOPE:])
    pe_sw = _pad_lanes(_swap_halves(uq[..., QK_NOPE:]))
    w_q = jnp.concatenate([nope[:, :, 0], pe[:, :, 0], nope[:, :, 1], pe[:, :, 1],
                           pe_sw[:, :, 0], pe_sw[:, :, 1]], axis=-1)
    w_q = w_q.reshape(Q_LORA, -1).astype(BF16)
    uk = w_uk.reshape(KV_LORA, MLA_HEADS // 2, 2 * QK_NOPE)
    uv = w_uv.reshape(KV_LORA, MLA_HEADS // 2, 2 * V_HEAD)
    w_kv = jnp.concatenate([uk, uv], axis=-1).reshape(KV_LORA, -1).astype(BF16)
    wuk_t = jnp.transpose(w_uk, (1, 2, 0)).astype(BF16)
    wuv_h = jnp.transpose(w_uv, (1, 0, 2)).astype(BF16)
    return w_a, wide, w_q, w_kv, wuk_t, wuv_h


def _rope_tables(positions):
    half = QK_ROPE // 2
    inv = ROPE_THETA ** (-jnp.arange(half, dtype=F32) / half)
    ang = positions.astype(F32)[:, None] * inv[None, :]
    cos, sin = jnp.cos(ang), jnp.sin(ang)
    zeros = jnp.zeros((positions.shape[0], LANES - QK_ROPE), F32)
    return (jnp.concatenate([cos, cos, zeros], axis=1), jnp.concatenate([-sin, sin, zeros], axis=1))


def kernel(x_prompt, x_sample, cache_ckv, cache_kpe, state_hgrn, pre_mix_g, w_in, q_norm_g, w_uq, kv_norm_g,
           w_uk, w_uv, w_oa, hg_lb, hg_norm_g, w_ob, w_out, post_mix_g, pre_mlp_g, w_up, w_down, post_mlp_g):
    batch, seq, d = x_prompt.shape
    dec_batch, dec_seq, _ = x_sample.shape
    depth = w_in.shape[0]
    past = cache_ckv.shape[2]
    n_p = batch * seq
    n_s = dec_batch * dec_seq
    rows = n_p + n_s
    tm = 768
    assert rows % tm == 0 and n_p % 256 == 0

    x = jnp.concatenate([x_prompt.reshape(n_p, d), x_sample.reshape(n_s, d)], axis=0)
    pos = jnp.concatenate([jnp.tile(jnp.arange(seq, dtype=jnp.int32), batch),
                           jnp.tile(past + jnp.arange(dec_seq, dtype=jnp.int32), dec_batch)])
    cos_t, sin_t = _rope_tables(pos)
    cos_q, sin_q = cos_t * ATTN_SCALE, sin_t * ATTN_SCALE

    ckv_out, kpe_out, st_p_out, st_s_out = [], [], [], []
    for l in range(depth):
        w_a, (w_hq, w_hf, w_hi, w_hg, w_ga, w_gb), w_q, w_kv, wuk_t, wuv_h = _prep_layer(
            w_in[l], w_uq[l], w_uk[l], w_uv[l])
        h = _rmsnorm(x, pre_mix_g[l], tm)
        qn, ckv, ckv_b, krot, krot_b = _latent(h, w_a, q_norm_g[l], kv_norm_g[l], cos_t, sin_t, tm)
        wide = functools.partial(_mm, h, tm=tm, tn=1024)
        (hq,) = wide(w_hq, _epi_silu, [(1024, F32)], name="proj_hq")
        hk, logf = wide(w_hf, functools.partial(_epi_forget, layer=l), [(1024, F32), (1024, F32)],
                        col_args=[(hg_lb, 1024)], name="proj_hf")
        (hv,) = wide(w_hi, _epi_id, [(1024, BF16)], name="proj_hi")
        (hgate,) = wide(w_hg, _epi_silu, [(1024, F32)], name="proj_hg")
        (ga,) = wide(w_ga, _epi_sigmoid, [(1024, F32)], name="proj_ga")
        (gb,) = wide(w_gb, _epi_sigmoid, [(1024, F32)], name="proj_gb")
        (q,) = _mm(qn, w_q, _epi_q, [(2 * QK_PAD, BF16)], tm=tm, tn=6 * LANES,
                   tile_args=[(cos_q, LANES, False), (sin_q, LANES, False)], name="proj_q")
        keys, vals = _mm(ckv_b, w_kv, _epi_kv, [(2 * QK_PAD, BF16), (2 * V_HEAD, BF16)], tm=512, tn=4 * LANES,
                         rows=n_p, tile_args=[(krot_b, LANES, False)], name="proj_kv")
        oa_p = _prompt_attention(q, keys, vals, batch, seq, 256)
        oa_s = _sample_attention(q, ckv_b, krot_b, cache_ckv, cache_kpe, wuk_t, wuv_h, l, n_p, dec_batch, dec_seq)
        oa = jnp.concatenate([oa_p, oa_s], axis=0)
        oh_p, st_p = _hgrn(hq, hk, logf, hv, hgate, hg_norm_g[l], None, layer=l,
                           row_block0=0, n_streams=batch, stream_len=seq, length=CHUNK,
                           block_len=512, group=4)
        oh_s, st_s = _hgrn(hq, hk, logf, hv, hgate, hg_norm_g[l], state_hgrn, layer=l,
                           row_block0=n_p // dec_seq, n_streams=dec_batch, stream_len=dec_seq, length=dec_seq,
                           block_len=dec_seq, group=4)
        oh = jnp.concatenate([oh_p, oh_s], axis=0)
        mix = _merge(oa, oh, w_oa[l].astype(BF16), w_ob[l].astype(BF16), ga, gb, tm, 512)
        x1, h2 = _outproj(mix, w_out[l].astype(BF16), x, post_mix_g[l], pre_mlp_g[l], 384)
        x = _mlp(h2, w_up[l].astype(BF16), w_down[l].astype(BF16), x1, post_mlp_g[l], tm, 512)

        ckv_out.append(ckv)
        kpe_out.append(krot)
        st_p_out.append(st_p)
        st_s_out.append(st_s)

    ckv_all = jnp.stack(ckv_out)
    kpe_all = jnp.stack(kpe_out)
    return (x[:n_p].reshape(batch, seq, d),
            x[n_p:].reshape(dec_batch, dec_seq, d),
            ckv_all[:, :n_p].reshape(depth, batch, seq, KV_LORA),
            kpe_all[:, :n_p].reshape(depth, batch, seq, QK_ROPE),
            jnp.stack(st_p_out),
            ckv_all[:, n_p:].reshape(depth, dec_batch, dec_seq, KV_LORA),
            kpe_all[:, n_p:].reshape(depth, dec_batch, dec_seq, QK_ROPE),
            jnp.stack(st_s_out))
```

```python
import functools

import jax
import jax.numpy as jnp
import numpy as np
from jax import lax
from jax.experimental import pallas as pl
from jax.experimental.pallas import tpu as pltpu

F32 = jnp.float32
BF16 = jnp.bfloat16

CHUNK = 64
MLA_HEADS = 16
QK_NOPE = 128
QK_ROPE = 64
QK_HEAD = QK_NOPE + QK_ROPE
V_HEAD = 128
Q_LORA = 512
KV_LORA = 512
ROPE_THETA = 10000.0
ATTN_SCALE = QK_HEAD ** -0.5
HG_HEADS = 16
HG_DK = 128
HG_DV = 128
EPS = 1e-6

LANES = 128
QK_PAD = 2 * LANES
NEG_BIG = -1e30
LOG2_E = 1.4426950408889634
VMEM_LIMIT = 56 * 1024 * 1024


def _params(*sem):
    return pltpu.CompilerParams(dimension_semantics=sem, vmem_limit_bytes=VMEM_LIMIT)


def _rms(x, g):
    return x * lax.rsqrt(jnp.mean(x * x, axis=-1, keepdims=True) + EPS) * g


def _dot(a, b):
    return jnp.dot(a, b, preferred_element_type=F32)


def _dot_nt(a, b):
    return lax.dot_general(a, b, (((1,), (1,)), ((), ())), preferred_element_type=F32)


def _dot_tn(a, b):
    return lax.dot_general(a, b, (((0,), (0,)), ((), ())), preferred_element_type=F32)


def _rmsnorm_body(x_ref, g_ref, o_ref):
    o_ref[...] = _rms(x_ref[...], g_ref[...]).astype(o_ref.dtype)


def _rmsnorm(x, g, tm):
    rows, d = x.shape
    return pl.pallas_call(
        _rmsnorm_body,
        grid=(rows // tm,),
        in_specs=[pl.BlockSpec((tm, d), lambda i: (i, 0)),
                  pl.BlockSpec((1, d), lambda i: (0, 0))],
        out_specs=pl.BlockSpec((tm, d), lambda i: (i, 0)),
        out_shape=jax.ShapeDtypeStruct((rows, d), BF16),
        compiler_params=_params("parallel"),
        name="rmsnorm",
    )(x, g.reshape(1, d))


def _mm_body(x_ref, w_ref, *rest, epi, n_extra):
    extra = rest[:n_extra]
    outs = rest[n_extra:]
    acc = _dot(x_ref[...], w_ref[...])
    res = epi(acc, *[e[...] for e in extra])
    for o_ref, r in zip(outs, res):
        o_ref[...] = r.astype(o_ref.dtype)


def _mm(x, w, epi, outs, *, layer, tm, tn, n=None, col0=0, rows=None, col_args=(), tile_args=(), name):
    k = x.shape[1]
    rows = x.shape[0] if rows is None else rows
    n = w.shape[2] if n is None else n
    cb0 = col0 // tn
    grid = (n // tn, rows // tm)
    in_specs = [pl.BlockSpec((tm, k), lambda j, i: (i, 0)),
                pl.BlockSpec((None, k, tn), lambda j, i: (layer, 0, cb0 + j))]
    args = [x, w]
    for arr, width in col_args:
        in_specs.append(pl.BlockSpec((arr.shape[0], width), lambda j, i: (0, j)))
        args.append(arr)
    for arr, width, follows_n in tile_args:
        if follows_n:
            in_specs.append(pl.BlockSpec((tm, width), lambda j, i: (i, j)))
        else:
            in_specs.append(pl.BlockSpec((tm, width), lambda j, i: (i, 0)))
        args.append(arr)
    out_specs = [pl.BlockSpec((tm, width), lambda j, i: (i, j)) for width, _ in outs]
    out_shape = [jax.ShapeDtypeStruct((rows, width * (n // tn)), dt) for width, dt in outs]
    res = pl.pallas_call(
        functools.partial(_mm_body, epi=epi, n_extra=len(col_args) + len(tile_args)),
        grid=grid,
        in_specs=in_specs,
        out_specs=out_specs,
        out_shape=out_shape,
        compiler_params=_params("parallel", "parallel"),
        name=name,
    )(*args)
    return res


def _silu(z):
    return z * jax.nn.sigmoid(z)


def _epi_silu(acc):
    return (_silu(acc),)


def _epi_sigmoid(acc):
    return (jax.nn.sigmoid(acc),)


def _epi_id(acc):
    return (acc,)


def _epi_relu2(acc):
    r = jnp.maximum(acc, 0.0)
    return (r * r,)


def _epi_forget(acc, lb_logits, *, layer):
    mx = jnp.max(lb_logits, axis=0, keepdims=True)
    e = jnp.exp(lb_logits - mx)
    sm = e / jnp.sum(e, axis=0, keepdims=True)
    lb = jnp.zeros_like(mx)
    for i in range(1, layer + 1):
        lb = lb + sm[i:i + 1]
    z = acc
    log_sig = jnp.minimum(z, 0.0) - jnp.log1p(jnp.exp(-jnp.abs(z)))
    a = jnp.log(lb)
    c = jnp.log1p(-lb) + log_sig
    hi = jnp.maximum(a, c)
    lo = jnp.minimum(a, c)
    logf = hi + jnp.log1p(jnp.exp(lo - hi))
    kk = (1.0 - lb) * jax.nn.sigmoid(-z)
    return kk, logf


def _epi_q(acc, cos_t, sin_t):
    n0 = acc[:, 0 * LANES:1 * LANES] * ATTN_SCALE
    p0 = acc[:, 1 * LANES:2 * LANES] * cos_t + acc[:, 4 * LANES:5 * LANES] * sin_t
    n1 = acc[:, 2 * LANES:3 * LANES] * ATTN_SCALE
    p1 = acc[:, 3 * LANES:4 * LANES] * cos_t + acc[:, 5 * LANES:6 * LANES] * sin_t
    return (jnp.concatenate([n0, p0, n1, p1], axis=1),)


def _epi_kv(acc, krot):
    krot = krot.astype(F32)
    keys = jnp.concatenate([acc[:, 0:LANES], krot, acc[:, LANES:2 * LANES], krot], axis=1)
    return keys, acc[:, 2 * LANES:4 * LANES]


def _latent_body(h_ref, w_ref, gq_ref, gkv_ref, cos_ref, sin_ref,
                 qn_ref, ckv_ref, ckvb_ref, krot_ref, krotb_ref):
    acc = _dot(h_ref[...], w_ref[...])
    qn_ref[...] = _rms(acc[:, :Q_LORA], gq_ref[...]).astype(qn_ref.dtype)
    ckv = _rms(acc[:, Q_LORA:Q_LORA + KV_LORA], gkv_ref[...])
    ckv_ref[...] = ckv
    ckvb_ref[...] = ckv.astype(ckvb_ref.dtype)
    base = Q_LORA + KV_LORA
    kr = acc[:, base:base + LANES] * cos_ref[...] + acc[:, base + LANES:base + 2 * LANES] * sin_ref[...]
    krot_ref[...] = kr[:, :QK_ROPE]
    krotb_ref[...] = kr.astype(krotb_ref.dtype)


def _latent(h, w_a, gq, gkv, cos_t, sin_t, layer, tm):
    rows, d = h.shape
    n = w_a.shape[2]
    row = lambda width: pl.BlockSpec((tm, width), lambda i: (i, 0))
    const = lambda r, width: pl.BlockSpec((r, width), lambda i: (0, 0))
    return pl.pallas_call(
        _latent_body,
        grid=(rows // tm,),
        in_specs=[row(d), pl.BlockSpec((None, d, n), lambda i: (layer, 0, 0)),
                  const(1, Q_LORA), const(1, KV_LORA), row(LANES), row(LANES)],
        out_specs=[row(Q_LORA), row(KV_LORA), row(KV_LORA), row(QK_ROPE), row(LANES)],
        out_shape=[jax.ShapeDtypeStruct((rows, Q_LORA), BF16),
                   jax.ShapeDtypeStruct((rows, KV_LORA), F32),
                   jax.ShapeDtypeStruct((rows, KV_LORA), BF16),
                   jax.ShapeDtypeStruct((rows, QK_ROPE), F32),
                   jax.ShapeDtypeStruct((rows, LANES), BF16)],
        compiler_params=_params("parallel"),
        name="latent_proj",
    )(h, w_a, gq.reshape(1, -1), gkv.reshape(1, -1), cos_t, sin_t)


def _attn_body(q_ref, k_ref, v_ref, o_ref, *, tq, nq):
    qi = pl.program_id(2)
    r_chunk = lax.broadcasted_iota(jnp.int32, (tq, tq), 0) // CHUNK
    c_chunk = lax.broadcasted_iota(jnp.int32, (tq, tq), 1) // CHUNK
    visible = c_chunk <= r_chunk

    for n_past in range(nq):
        @pl.when(qi == n_past)
        def _(n_past=n_past):
            q = q_ref[...]
            lo = n_past * tq
            s_d = jnp.where(visible, _dot_nt(q, k_ref[lo:lo + tq, :]), NEG_BIG)
            m = jnp.max(s_d, axis=-1, keepdims=True)
            if n_past:
                s_p = _dot_nt(q, k_ref[0:lo, :])
                m = jnp.maximum(m, jnp.max(s_p, axis=-1, keepdims=True))
            p_d = jnp.exp(s_d - m)
            l = jnp.sum(p_d, axis=-1, keepdims=True)
            acc = _dot(p_d.astype(BF16), v_ref[lo:lo + tq, :])
            if n_past:
                p_p = jnp.exp(s_p - m)
                l = l + jnp.sum(p_p, axis=-1, keepdims=True)
                acc = acc + _dot(p_p.astype(BF16), v_ref[0:lo, :])
            o_ref[...] = (acc / l).astype(o_ref.dtype)


def _prompt_attention(q, k, v, batch, seq, tq):
    nq = seq // tq
    return pl.pallas_call(
        functools.partial(_attn_body, tq=tq, nq=nq),
        grid=(batch, MLA_HEADS, nq),
        in_specs=[pl.BlockSpec((tq, QK_PAD), lambda b, h, i: (b * nq + i, h)),
                  pl.BlockSpec((seq, QK_PAD), lambda b, h, i: (b, h)),
                  pl.BlockSpec((seq, V_HEAD), lambda b, h, i: (b, h))],
        out_specs=pl.BlockSpec((tq, V_HEAD), lambda b, h, i: (b * nq + i, h)),
        out_shape=jax.ShapeDtypeStruct((batch * seq, MLA_HEADS * V_HEAD), BF16),
        compiler_params=_params("parallel", "parallel", "arbitrary"),
        name="prompt_attention",
    )(q, k, v)


def _sattn_body(q_ref, cn_ref, kn_ref, cc_ref, ck_ref, wuk_ref, wuv_ref, o_ref, qa_scr, qp_scr,
                *, dec_seq, key_tile):
    for h in range(MLA_HEADS):
        qn = q_ref[:, h * QK_PAD:h * QK_PAD + QK_NOPE]
        qa_scr[h * dec_seq:(h + 1) * dec_seq, :] = _dot(qn, wuk_ref[h]).astype(BF16)
        qp_scr[h * dec_seq:(h + 1) * dec_seq, :] = q_ref[:, h * QK_PAD + QK_NOPE:(h + 1) * QK_PAD]
    qa = qa_scr[...]
    qp = qp_scr[...]
    rows = MLA_HEADS * dec_seq

    def update(carry, s, c):
        m, l, acc = carry
        m_new = jnp.maximum(m, jnp.max(s, axis=-1, keepdims=True))
        alpha = jnp.exp(m - m_new)
        p = jnp.exp(s - m_new)
        l = alpha * l + jnp.sum(p, axis=-1, keepdims=True)
        acc = alpha * acc + _dot(p.astype(BF16), c)
        return m_new, l, acc

    carry = (jnp.full((rows, 1), NEG_BIG, F32), jnp.zeros((rows, 1), F32), jnp.zeros((rows, KV_LORA), F32))
    past = cc_ref.shape[0]
    for t in range(past // key_tile):
        c = cc_ref[t * key_tile:(t + 1) * key_tile, :].astype(BF16)
        kp = ck_ref[t * key_tile:(t + 1) * key_tile, :].astype(BF16)
        s = _dot_nt(qa, c) + _dot_nt(qp[:, :QK_ROPE], kp)
        carry = update(carry, s, c)
    cn = cn_ref[...]
    s = _dot_nt(qa, cn) + _dot_nt(qp, kn_ref[...])
    _, l, acc = update(carry, s, cn)
    o_lat = (acc / l).astype(BF16)
    for h in range(MLA_HEADS):
        o_ref[:, h * V_HEAD:(h + 1) * V_HEAD] = _dot(
            o_lat[h * dec_seq:(h + 1) * dec_seq, :], wuv_ref[h]).astype(o_ref.dtype)


def _sample_attention(q, ckv_b, krot_b, cache_c, cache_k, wuk_t, wuv_h, layer, n_prompt_rows, dec_batch, dec_seq):
    rb = n_prompt_rows // dec_seq
    past = cache_c.shape[2]
    key_tile = min(past, 1024)
    per_layer = lambda s: pl.BlockSpec((None,) + s, lambda b: (layer, 0, 0, 0))
    return pl.pallas_call(
        functools.partial(_sattn_body, dec_seq=dec_seq, key_tile=key_tile),
        grid=(dec_batch,),
        in_specs=[pl.BlockSpec((dec_seq, MLA_HEADS * QK_PAD), lambda b: (rb + b, 0)),
                  pl.BlockSpec((dec_seq, KV_LORA), lambda b: (rb + b, 0)),
                  pl.BlockSpec((dec_seq, LANES), lambda b: (rb + b, 0)),
                  pl.BlockSpec((None, None, past, KV_LORA), lambda b: (layer, b, 0, 0)),
                  pl.BlockSpec((None, None, past, QK_ROPE), lambda b: (layer, b, 0, 0)),
                  per_layer((MLA_HEADS, QK_NOPE, KV_LORA)),
                  per_layer((MLA_HEADS, KV_LORA, V_HEAD))],
        out_specs=pl.BlockSpec((dec_seq, MLA_HEADS * V_HEAD), lambda b: (b, 0)),
        out_shape=jax.ShapeDtypeStruct((dec_batch * dec_seq, MLA_HEADS * V_HEAD), BF16),
        scratch_shapes=[pltpu.VMEM((MLA_HEADS * dec_seq, KV_LORA), BF16),
                        pltpu.VMEM((MLA_HEADS * dec_seq, LANES), BF16)],
        compiler_params=_params("parallel"),
        name="sample_attention",
    )(q, ckv_b, krot_b, cache_c, cache_k, wuk_t, wuv_h)


def _hgrn_tables(length):
    t = np.arange(length)[:, None]
    r = np.arange(length)[None, :]
    groups = [(r <= t), (r > t)]
    masks = [(r == t)]
    m = length // 2
    while m >= 1:
        blk = t // (2 * m)
        start2 = blk * 2 * m + m
        second = (t % (2 * m)) >= m
        incl = second & (r >= start2) & (r <= t)
        excl = (~second) & (r > t) & (r < start2)
        groups.append(incl | excl)
        r_blk = r // (2 * m)
        r_first = (r % (2 * m)) < m
        masks.append(second & r_first & (r_blk == blk))
        m //= 2
    return (np.concatenate(groups, axis=0).astype(np.float32),
            np.stack(masks, axis=0).astype(np.float32))


def _hgrn_body(*refs, length, n_chunks, group, has_state):
    if has_state:
        q_ref, k_ref, lf_ref, v_ref, g_ref, gn_ref, sum_ref, mask_ref, s0_ref, o_ref, sout_ref, st_scr = refs
    else:
        q_ref, k_ref, lf_ref, v_ref, g_ref, gn_ref, sum_ref, mask_ref, o_ref, sout_ref, st_scr = refs
    tb = pl.program_id(2)

    @pl.when(tb == 0)
    def _():
        for g in range(group):
            st_scr[g] = s0_ref[g].T if has_state else jnp.zeros((HG_DV, HG_DK), F32)

    n_lev = mask_ref.shape[0] - 1
    summat = sum_ref[...]
    gn = gn_ref[...]

    pack_levels = length % LANES == 0 and (n_lev + 1) % 2 == 0
    zeros_k = jnp.zeros((length, HG_DK), BF16)

    def mixed_rows(q, k, half):
        parts = []
        for r in range(0, length, half):
            parts.append((q if (r // half) % 2 else k)[r:r + half])
        return jnp.concatenate(parts, axis=0)

    def level_operands(rows, g, e_all):
        cols = slice(g * HG_DK, (g + 1) * HG_DK)
        q = q_ref[rows, cols]
        k = k_ref[rows, cols]
        qs = [q.astype(BF16)]
        ks = [k.astype(BF16)]
        half = length // 2
        for lev in range(n_lev):
            e = e_all[(2 + lev) * length:(3 + lev) * length]
            if half % 8 == 0:
                u = (mixed_rows(q, k, half) * e).astype(BF16)
                qs.append(u)
                ks.append(u)
            else:
                qs.append((q * e).astype(BF16))
                ks.append((k * e).astype(BF16))
            half //= 2
        q_in = (q * e_all[0:length]).astype(BF16)
        k_out = (k * e_all[length:2 * length]).astype(BF16)
        return qs, ks, q_in, k_out

    def level_scores(qs, ks):
        if pack_levels:
            out = []
            for p in range(0, n_lev + 1, 2):
                lhs = jnp.concatenate([qs[p], qs[p + 1]], axis=1)
                rhs = jnp.concatenate([jnp.concatenate([ks[p], zeros_k], axis=1),
                                       jnp.concatenate([zeros_k, ks[p + 1]], axis=1)], axis=0)
                a = _dot_nt(lhs, rhs)
                out += [a[:, :length], a[:, length:]]
            return out
        return [_dot_nt(qs[p], ks[p]) for p in range(n_lev + 1)]

    def chunk(c, _):
        rows = pl.ds(pl.multiple_of(c * length, length), length)
        e_heads = []
        for pair in range(group // 2):
            cols2 = slice(2 * pair * HG_DK, (2 * pair + 2) * HG_DK)
            lf2 = lf_ref[rows, cols2] * LOG2_E
            hi = lf2.astype(BF16)
            mid = (lf2 - hi.astype(F32)).astype(BF16)
            e_pair = jnp.exp2(_dot(summat, jnp.concatenate([hi, mid], axis=0)))
            e_heads += [e_pair[:, :HG_DK], e_pair[:, HG_DK:]]
        work = []
        for g in range(group):
            qs, ks, q_in, k_out = level_operands(rows, g, e_heads[g])
            st = st_scr[g]
            work.append((level_scores(qs, ks), _dot_nt(q_in, st.astype(BF16)), k_out, st))
        for g in range(group):
            cols = slice(g * HG_DK, (g + 1) * HG_DK)
            scores, o_in, k_out, st = work[g]
            v = v_ref[rows, cols]
            att = mask_ref[0] * scores[0]
            for p in range(1, n_lev + 1):
                att = att + mask_ref[p] * scores[p]
            o = o_in + _dot(att.astype(BF16), v)
            e_last = e_heads[g][length - 1:length, :]
            st_scr[g] = st * e_last + _dot_tn(v, k_out)
            o_ref[rows, cols] = (_rms(o, gn) * g_ref[rows, cols]).astype(o_ref.dtype)
        return 0

    lax.fori_loop(0, n_chunks, chunk, 0)

    @pl.when(tb == pl.num_programs(2) - 1)
    def _():
        for g in range(group):
            sout_ref[g] = st_scr[g].T


def _hgrn(q, k, logf, v, gate, gn, state0, *, layer, row_block0, n_streams, stream_len, length, block_len, group):
    n_tb = stream_len // block_len
    n_chunks = block_len // length
    width = group * HG_DK
    summat, masks = _hgrn_tables(length)
    summat = jnp.asarray(np.concatenate([summat, summat], axis=1), BF16)
    masks = jnp.asarray(masks, F32)
    tok = lambda: pl.BlockSpec((block_len, width), lambda n, h, t: (row_block0 + n * n_tb + t, h))
    st_spec = pl.BlockSpec((None, group, HG_DK, HG_DV), lambda n, h, t: (n, h, 0, 0))
    in_specs = [tok(), tok(), tok(), tok(), tok(), pl.BlockSpec((1, HG_DV), lambda n, h, t: (0, 0)),
                pl.BlockSpec(summat.shape, lambda n, h, t: (0, 0)),
                pl.BlockSpec(masks.shape, lambda n, h, t: (0, 0, 0))]
    args = [q, k, logf, v, gate, gn.reshape(1, HG_DV), summat, masks]
    if state0 is not None:
        in_specs.append(pl.BlockSpec((None, None, group, HG_DK, HG_DV), lambda n, h, t: (layer, n, h, 0, 0)))
        args.append(state0)
    return pl.pallas_call(
        functools.partial(_hgrn_body, length=length, n_chunks=n_chunks, group=group,
                          has_state=state0 is not None),
        grid=(n_streams, HG_HEADS // group, n_tb),
        in_specs=in_specs,
        out_specs=[pl.BlockSpec((block_len, width), lambda n, h, t: (n * n_tb + t, h)), st_spec],
        out_shape=[jax.ShapeDtypeStruct((n_streams * stream_len, HG_HEADS * HG_DV), BF16),
                   jax.ShapeDtypeStruct((n_streams, HG_HEADS, HG_DK, HG_DV), F32)],
        scratch_shapes=[pltpu.VMEM((group, HG_DV, HG_DK), F32)],
        compiler_params=_params("parallel", "parallel", "arbitrary"),
        name="hgrn_state" if state0 is not None else "hgrn_prompt",
    )(*args)


def _merge_body(oa_ref, oh_ref, woa_ref, wob_ref, ga_ref, gb_ref, o_ref):
    a = _dot(oa_ref[...], woa_ref[...])
    b = _dot(oh_ref[...], wob_ref[...])
    o_ref[...] = (ga_ref[...] * a + gb_ref[...] * b).astype(o_ref.dtype)


def _merge(oa, oh, woa, wob, ga, gb, layer, tm, tn):
    rows, k = oa.shape
    n = woa.shape[2]
    x_spec = pl.BlockSpec((tm, k), lambda j, i: (i, 0))
    w_spec = pl.BlockSpec((None, k, tn), lambda j, i: (layer, 0, j))
    t_spec = pl.BlockSpec((tm, tn), lambda j, i: (i, j))
    return pl.pallas_call(
        _merge_body,
        grid=(n // tn, rows // tm),
        in_specs=[x_spec, x_spec, w_spec, w_spec, t_spec, t_spec],
        out_specs=t_spec,
        out_shape=jax.ShapeDtypeStruct((rows, n), BF16),
        compiler_params=_params("parallel", "parallel"),
        name="gated_merge",
    )(oa, oh, woa, wob, ga, gb)


def _outproj_body(m_ref, w_ref, x_ref, g1_ref, g2_ref, x1_ref, h2_ref):
    y = _dot(m_ref[...], w_ref[...])
    x1 = x_ref[...] + _rms(y, g1_ref[...])
    x1_ref[...] = x1
    h2_ref[...] = _rms(x1, g2_ref[...]).astype(h2_ref.dtype)


def _outproj(mix, w, x, g_post, g_pre2, layer, tm):
    rows, d = x.shape
    row = pl.BlockSpec((tm, d), lambda i: (i, 0))
    vec = pl.BlockSpec((1, d), lambda i: (0, 0))
    return pl.pallas_call(
        _outproj_body,
        grid=(rows // tm,),
        in_specs=[row, pl.BlockSpec((None, d, d), lambda i: (layer, 0, 0)), row, vec, vec],
        out_specs=[row, row],
        out_shape=[jax.ShapeDtypeStruct((rows, d), F32), jax.ShapeDtypeStruct((rows, d), BF16)],
        compiler_params=_params("parallel"),
        name="out_proj",
    )(mix, w, x, g_post.reshape(1, d), g_pre2.reshape(1, d))


def _mlp_body(h_ref, wu_ref, wd_ref, x_ref, g_ref, o_ref, acc_ref):
    f = pl.program_id(1)

    @pl.when(f == 0)
    def _():
        acc_ref[...] = jnp.zeros_like(acc_ref)

    u = jnp.maximum(_dot(h_ref[...], wu_ref[...]), 0.0)
    acc_ref[...] += _dot((u * u).astype(BF16), wd_ref[...])

    @pl.when(f == pl.num_programs(1) - 1)
    def _():
        o_ref[...] = x_ref[...] + _rms(acc_ref[...], g_ref[...])


def _mlp(h2, w_up, w_down, x1, g, layer, tm, tf):
    rows, d = x1.shape
    ff = w_up.shape[2]
    row = pl.BlockSpec((tm, d), lambda i, f: (i, 0))
    return pl.pallas_call(
        _mlp_body,
        grid=(rows // tm, ff // tf),
        in_specs=[row, pl.BlockSpec((None, d, tf), lambda i, f: (layer, 0, f)),
                  pl.BlockSpec((None, tf, d), lambda i, f: (layer, f, 0)), row,
                  pl.BlockSpec((1, d), lambda i, f: (0, 0))],
        out_specs=row,
        out_shape=jax.ShapeDtypeStruct((rows, d), F32),
        scratch_shapes=[pltpu.VMEM((tm, d), F32)],
        compiler_params=_params("parallel", "arbitrary"),
        name="mlp",
    )(h2, w_up, w_down, x1, g.reshape(1, d))


def _pad_lanes(w):
    return jnp.concatenate([w, jnp.zeros(w.shape[:-1] + (LANES - w.shape[-1],), w.dtype)], axis=-1)


def _swap_halves(w):
    half = w.shape[-1] // 2
    return jnp.concatenate([w[..., half:], w[..., :half]], axis=-1)


def _prep_weights(w_in, w_uq, w_uk, w_uv):
    depth = w_in.shape[0]
    base = Q_LORA + KV_LORA
    kpe = w_in[:, :, base:base + QK_ROPE]
    w_a = jnp.concatenate([w_in[:, :, :base], _pad_lanes(kpe), _pad_lanes(_swap_halves(kpe))], axis=2).astype(BF16)
    w_wide = w_in[:, :, base + QK_ROPE:].astype(BF16)
    uq = w_uq.reshape(depth, Q_LORA, MLA_HEADS // 2, 2, QK_HEAD)
    nope = uq[..., :QK_NOPE]
    pe = _pad_lanes(uq[..., QK_NOPE:])
    pe_sw = _pad_lanes(_swap_halves(uq[..., QK_NOPE:]))
    w_q = jnp.concatenate([nope[:, :, :, 0], pe[:, :, :, 0], nope[:, :, :, 1], pe[:, :, :, 1],
                           pe_sw[:, :, :, 0], pe_sw[:, :, :, 1]], axis=-1)
    w_q = w_q.reshape(depth, Q_LORA, -1).astype(BF16)
    uk = w_uk.reshape(depth, KV_LORA, MLA_HEADS // 2, 2 * QK_NOPE)
    uv = w_uv.reshape(depth, KV_LORA, MLA_HEADS // 2, 2 * V_HEAD)
    w_kv = jnp.concatenate([uk, uv], axis=-1).reshape(depth, KV_LORA, -1).astype(BF16)
    wuk_t = jnp.transpose(w_uk, (0, 2, 3, 1)).astype(BF16)
    wuv_h = jnp.transpose(w_uv, (0, 2, 1, 3)).astype(BF16)
    return w_a, w_wide, w_q, w_kv, wuk_t, wuv_h


def _rope_tables(positions):
    half = QK_ROPE // 2
    inv = ROPE_THETA ** (-jnp.arange(half, dtype=F32) / half)
    ang = positions.astype(F32)[:, None] * inv[None, :]
    cos, sin = jnp.cos(ang), jnp.sin(ang)
    zeros = jnp.zeros((positions.shape[0], LANES - QK_ROPE), F32)
    return (jnp.concatenate([cos, cos, zeros], axis=1), jnp.concatenate([-sin, sin, zeros], axis=1))


def kernel(x_prompt, x_sample, cache_ckv, cache_kpe, state_hgrn, pre_mix_g, w_in, q_norm_g, w_uq, kv_norm_g,
           w_uk, w_uv, w_oa, hg_lb, hg_norm_g, w_ob, w_out, post_mix_g, pre_mlp_g, w_up, w_down, post_mlp_g):
    batch, seq, d = x_prompt.shape
    dec_batch, dec_seq, _ = x_sample.shape
    depth = w_in.shape[0]
    past = cache_ckv.shape[2]
    n_p = batch * seq
    n_s = dec_batch * dec_seq
    rows = n_p + n_s
    tm = 768
    assert rows % tm == 0 and n_p % 256 == 0

    x = jnp.concatenate([x_prompt.reshape(n_p, d), x_sample.reshape(n_s, d)], axis=0)
    pos = jnp.concatenate([jnp.tile(jnp.arange(seq, dtype=jnp.int32), batch),
                           jnp.tile(past + jnp.arange(dec_seq, dtype=jnp.int32), dec_batch)])
    cos_t, sin_t = _rope_tables(pos)
    cos_q, sin_q = cos_t * ATTN_SCALE, sin_t * ATTN_SCALE

    w_a, w_wide, w_q, w_kv, wuk_t, wuv_h = _prep_weights(w_in, w_uq, w_uk, w_uv)
    w_oa_b, w_ob_b, w_out_b = w_oa.astype(BF16), w_ob.astype(BF16), w_out.astype(BF16)
    w_up_b, w_down_b = w_up.astype(BF16), w_down.astype(BF16)

    ckv_out, kpe_out, st_p_out, st_s_out = [], [], [], []
    for l in range(depth):
        h = _rmsnorm(x, pre_mix_g[l], tm)
        qn, ckv, ckv_b, krot, krot_b = _latent(h, w_a, q_norm_g[l], kv_norm_g[l], cos_t, sin_t, l, tm)
        wide = lambda i, *a, **kw: _mm(h, w_wide, *a, layer=l, tm=tm, tn=1024, n=d, col0=i * d, **kw)
        (hq,) = wide(0, _epi_silu, [(1024, F32)], name="proj_hq")
        hk, logf = wide(1, functools.partial(_epi_forget, layer=l), [(1024, F32), (1024, F32)],
                        col_args=[(hg_lb, 1024)], name="proj_hf")
        (hv,) = wide(2, _epi_id, [(1024, BF16)], name="proj_hi")
        (hgate,) = wide(3, _epi_silu, [(1024, F32)], name="proj_hg")
        (ga,) = wide(4, _epi_sigmoid, [(1024, F32)], name="proj_ga")
        (gb,) = wide(5, _epi_sigmoid, [(1024, F32)], name="proj_gb")
        (q,) = _mm(qn, w_q, _epi_q, [(2 * QK_PAD, BF16)], layer=l, tm=tm, tn=6 * LANES,
                   tile_args=[(cos_q, LANES, False), (sin_q, LANES, False)], name="proj_q")
        keys, vals = _mm(ckv_b, w_kv, _epi_kv, [(2 * QK_PAD, BF16), (2 * V_HEAD, BF16)], layer=l, tm=512,
                         tn=4 * LANES, rows=n_p, tile_args=[(krot_b, LANES, False)], name="proj_kv")
        oa_p = _prompt_attention(q, keys, vals, batch, seq, 256)
        oa_s = _sample_attention(q, ckv_b, krot_b, cache_ckv, cache_kpe, wuk_t, wuv_h, l, n_p, dec_batch, dec_seq)
        oa = jnp.concatenate([oa_p, oa_s], axis=0)
        oh_p, st_p = _hgrn(hq, hk, logf, hv, hgate, hg_norm_g[l], None, layer=l,
                           row_block0=0, n_streams=batch, stream_len=seq, length=2 * CHUNK,
                           block_len=512, group=8)
        oh_s, st_s = _hgrn(hq, hk, logf, hv, hgate, hg_norm_g[l], state_hgrn, layer=l,
                           row_block0=n_p // dec_seq, n_streams=dec_batch, stream_len=dec_seq, length=dec_seq,
                           block_len=dec_seq, group=4)
        oh = jnp.concatenate([oh_p, oh_s], axis=0)
        mix = _merge(oa, oh, w_oa_b, w_ob_b, ga, gb, l, tm, 512)
        x1, h2 = _outproj(mix, w_out_b, x, post_mix_g[l], pre_mlp_g[l], l, 384)
        x = _mlp(h2, w_up_b, w_down_b, x1, post_mlp_g[l], l, tm, 512)

        ckv_out.append(ckv)
        kpe_out.append(krot)
        st_p_out.append(st_p)
        st_s_out.append(st_s)

    ckv_all = jnp.stack(ckv_out)
    kpe_all = jnp.stack(kpe_out)
    return (x[:n_p].reshape(batch, seq, d),
            x[n_p:].reshape(dec_batch, dec_seq, d),
            ckv_all[:, :n_p].reshape(depth, batch, seq, KV_LORA),
            kpe_all[:, :n_p].reshape(depth, batch, seq, QK_ROPE),
            jnp.stack(st_p_out),
            ckv_all[:, n_p:].reshape(depth, dec_batch, dec_seq, KV_LORA),
            kpe_all[:, n_p:].reshape(depth, dec_batch, dec_seq, QK_ROPE),
            jnp.stack(st_s_out))
```

```python
import functools

import jax
import jax.numpy as jnp
import numpy as np
from jax import lax
from jax.experimental import pallas as pl
from jax.experimental.pallas import tpu as pltpu

F32 = jnp.float32
BF16 = jnp.bfloat16

CHUNK = 64
MLA_HEADS = 16
QK_NOPE = 128
QK_ROPE = 64
QK_HEAD = QK_NOPE + QK_ROPE
V_HEAD = 128
Q_LORA = 512
KV_LORA = 512
ROPE_THETA = 10000.0
ATTN_SCALE = QK_HEAD ** -0.5
HG_HEADS = 16
HG_DK = 128
HG_DV = 128
EPS = 1e-6

LANES = 128
QK_PAD = 2 * LANES
NEG_BIG = -1e30
LOG2_E = 1.4426950408889634
VMEM_LIMIT = 56 * 1024 * 1024


def _params(*sem):
    return pltpu.CompilerParams(dimension_semantics=sem, vmem_limit_bytes=VMEM_LIMIT)


def _rms(x, g):
    return x * lax.rsqrt(jnp.mean(x * x, axis=-1, keepdims=True) + EPS) * g


def _dot(a, b):
    return jnp.dot(a, b, preferred_element_type=F32)


def _dot_nt(a, b):
    return lax.dot_general(a, b, (((1,), (1,)), ((), ())), preferred_element_type=F32)


def _dot_tn(a, b):
    return lax.dot_general(a, b, (((0,), (0,)), ((), ())), preferred_element_type=F32)


def _rmsnorm_body(x_ref, g_ref, o_ref):
    o_ref[...] = _rms(x_ref[...], g_ref[...]).astype(o_ref.dtype)


def _rmsnorm(x, g, tm):
    rows, d = x.shape
    return pl.pallas_call(
        _rmsnorm_body,
        grid=(rows // tm,),
        in_specs=[pl.BlockSpec((tm, d), lambda i: (i, 0)),
                  pl.BlockSpec((1, d), lambda i: (0, 0))],
        out_specs=pl.BlockSpec((tm, d), lambda i: (i, 0)),
        out_shape=jax.ShapeDtypeStruct((rows, d), BF16),
        compiler_params=_params("parallel"),
        name="rmsnorm",
    )(x, g.reshape(1, d))


def _mm_body(x_ref, w_ref, *rest, epi, n_extra):
    extra = rest[:n_extra]
    outs = rest[n_extra:]
    acc = _dot(x_ref[...], w_ref[...])
    res = epi(acc, *[e[...] for e in extra])
    for o_ref, r in zip(outs, res):
        o_ref[...] = r.astype(o_ref.dtype)


def _mm(x, w, epi, outs, *, layer, tm, tn, n=None, col0=0, rows=None, col_args=(), tile_args=(), name):
    k = x.shape[1]
    rows = x.shape[0] if rows is None else rows
    n = w.shape[2] if n is None else n
    cb0 = col0 // tn
    grid = (n // tn, rows // tm)
    in_specs = [pl.BlockSpec((tm, k), lambda j, i: (i, 0)),
                pl.BlockSpec((None, k, tn), lambda j, i: (layer, 0, cb0 + j))]
    args = [x, w]
    for arr, width in col_args:
        in_specs.append(pl.BlockSpec((arr.shape[0], width), lambda j, i: (0, j)))
        args.append(arr)
    for arr, width, follows_n in tile_args:
        if follows_n:
            in_specs.append(pl.BlockSpec((tm, width), lambda j, i: (i, j)))
        else:
            in_specs.append(pl.BlockSpec((tm, width), lambda j, i: (i, 0)))
        args.append(arr)
    out_specs = [pl.BlockSpec((tm, width), lambda j, i: (i, j)) for width, _ in outs]
    out_shape = [jax.ShapeDtypeStruct((rows, width * (n // tn)), dt) for width, dt in outs]
    res = pl.pallas_call(
        functools.partial(_mm_body, epi=epi, n_extra=len(col_args) + len(tile_args)),
        grid=grid,
        in_specs=in_specs,
        out_specs=out_specs,
        out_shape=out_shape,
        compiler_params=_params("parallel", "parallel"),
        name=name,
    )(*args)
    return res


def _mm_rows_body(x_ref, w_ref, *rest, epi, n_extra, tn, widths):
    extra = [e[...] for e in rest[:n_extra]]
    outs = rest[n_extra:]
    x = x_ref[...]
    for p in range(w_ref.shape[1] // tn):
        res = epi(_dot(x, w_ref[:, p * tn:(p + 1) * tn]), *extra)
        for o_ref, r, width in zip(outs, res, widths):
            o_ref[:, p * width:(p + 1) * width] = r.astype(o_ref.dtype)


def _mm_rows(x, w, epi, outs, *, layer, tm, tn, rows=None, tile_args=(), name):
    k = x.shape[1]
    rows = x.shape[0] if rows is None else rows
    n = w.shape[2]
    in_specs = [pl.BlockSpec((tm, k), lambda i: (i, 0)),
                pl.BlockSpec((None, k, n), lambda i: (layer, 0, 0))]
    in_specs += [pl.BlockSpec((tm, width), lambda i: (i, 0)) for _, width in tile_args]
    return pl.pallas_call(
        functools.partial(_mm_rows_body, epi=epi, n_extra=len(tile_args), tn=tn,
                          widths=[width for width, _ in outs]),
        grid=(rows // tm,),
        in_specs=in_specs,
        out_specs=[pl.BlockSpec((tm, width * (n // tn)), lambda i: (i, 0)) for width, _ in outs],
        out_shape=[jax.ShapeDtypeStruct((rows, width * (n // tn)), dt) for width, dt in outs],
        compiler_params=_params("parallel"),
        name=name,
    )(x, w, *[arr for arr, _ in tile_args])


def _silu(z):
    return z * jax.nn.sigmoid(z)


def _epi_silu(acc):
    return (_silu(acc),)


def _epi_sigmoid(acc):
    return (jax.nn.sigmoid(acc),)


def _epi_id(acc):
    return (acc,)


def _epi_relu2(acc):
    r = jnp.maximum(acc, 0.0)
    return (r * r,)


def _epi_forget(acc, lb_logits, *, layer):
    mx = jnp.max(lb_logits, axis=0, keepdims=True)
    e = jnp.exp(lb_logits - mx)
    sm = e / jnp.sum(e, axis=0, keepdims=True)
    lb = jnp.zeros_like(mx)
    for i in range(1, layer + 1):
        lb = lb + sm[i:i + 1]
    z = acc
    log_sig = jnp.minimum(z, 0.0) - jnp.log1p(jnp.exp(-jnp.abs(z)))
    a = jnp.log(lb)
    c = jnp.log1p(-lb) + log_sig
    hi = jnp.maximum(a, c)
    lo = jnp.minimum(a, c)
    logf = hi + jnp.log1p(jnp.exp(lo - hi))
    kk = (1.0 - lb) * jax.nn.sigmoid(-z)
    return kk, logf


def _epi_q(acc, cos_t, sin_t):
    n0 = acc[:, 0 * LANES:1 * LANES] * ATTN_SCALE
    p0 = acc[:, 1 * LANES:2 * LANES] * cos_t + acc[:, 4 * LANES:5 * LANES] * sin_t
    n1 = acc[:, 2 * LANES:3 * LANES] * ATTN_SCALE
    p1 = acc[:, 3 * LANES:4 * LANES] * cos_t + acc[:, 5 * LANES:6 * LANES] * sin_t
    return (jnp.concatenate([n0, p0, n1, p1], axis=1),)


def _epi_kv(acc, krot):
    krot = krot.astype(F32)
    keys = jnp.concatenate([acc[:, 0:LANES], krot, acc[:, LANES:2 * LANES], krot], axis=1)
    return keys, acc[:, 2 * LANES:4 * LANES]


def _latent_body(h_ref, w_ref, gq_ref, gkv_ref, cos_ref, sin_ref,
                 qn_ref, ckv_ref, ckvb_ref, krot_ref, krotb_ref):
    acc = _dot(h_ref[...], w_ref[...])
    qn_ref[...] = _rms(acc[:, :Q_LORA], gq_ref[...]).astype(qn_ref.dtype)
    ckv = _rms(acc[:, Q_LORA:Q_LORA + KV_LORA], gkv_ref[...])
    ckv_ref[...] = ckv
    ckvb_ref[...] = ckv.astype(ckvb_ref.dtype)
    base = Q_LORA + KV_LORA
    kr = acc[:, base:base + LANES] * cos_ref[...] + acc[:, base + LANES:base + 2 * LANES] * sin_ref[...]
    krot_ref[...] = kr[:, :QK_ROPE]
    krotb_ref[...] = kr.astype(krotb_ref.dtype)


def _latent(h, w_a, gq, gkv, cos_t, sin_t, layer, tm):
    rows, d = h.shape
    n = w_a.shape[2]
    row = lambda width: pl.BlockSpec((tm, width), lambda i: (i, 0))
    const = lambda r, width: pl.BlockSpec((r, width), lambda i: (0, 0))
    return pl.pallas_call(
        _latent_body,
        grid=(rows // tm,),
        in_specs=[row(d), pl.BlockSpec((None, d, n), lambda i: (layer, 0, 0)),
                  const(1, Q_LORA), const(1, KV_LORA), row(LANES), row(LANES)],
        out_specs=[row(Q_LORA), row(KV_LORA), row(KV_LORA), row(QK_ROPE), row(LANES)],
        out_shape=[jax.ShapeDtypeStruct((rows, Q_LORA), BF16),
                   jax.ShapeDtypeStruct((rows, KV_LORA), F32),
                   jax.ShapeDtypeStruct((rows, KV_LORA), BF16),
                   jax.ShapeDtypeStruct((rows, QK_ROPE), F32),
                   jax.ShapeDtypeStruct((rows, LANES), BF16)],
        compiler_params=_params("parallel"),
        name="latent_proj",
    )(h, w_a, gq.reshape(1, -1), gkv.reshape(1, -1), cos_t, sin_t)


def _attn_body(q_ref, k_ref, v_ref, o_ref, *, tq, nq):
    r_chunk = lax.broadcasted_iota(jnp.int32, (tq, tq), 0) // CHUNK
    c_chunk = lax.broadcasted_iota(jnp.int32, (tq, tq), 1) // CHUNK
    visible = c_chunk <= r_chunk

    for n_past in range(nq):
        lo = n_past * tq
        q = q_ref[lo:lo + tq, :]
        s_d = jnp.where(visible, _dot_nt(q, k_ref[lo:lo + tq, :]), NEG_BIG)
        m = jnp.max(s_d, axis=-1, keepdims=True)
        if n_past:
            s_p = _dot_nt(q, k_ref[0:lo, :])
            m = jnp.maximum(m, jnp.max(s_p, axis=-1, keepdims=True))
        p_d = jnp.exp(s_d - m)
        l = jnp.sum(p_d, axis=-1, keepdims=True)
        acc = _dot(p_d.astype(BF16), v_ref[lo:lo + tq, :])
        if n_past:
            p_p = jnp.exp(s_p - m)
            l = l + jnp.sum(p_p, axis=-1, keepdims=True)
            acc = acc + _dot(p_p.astype(BF16), v_ref[0:lo, :])
        o_ref[lo:lo + tq, :] = (acc / l).astype(o_ref.dtype)


def _prompt_attention(q, k, v, batch, seq, tq):
    return pl.pallas_call(
        functools.partial(_attn_body, tq=tq, nq=seq // tq),
        grid=(batch, MLA_HEADS),
        in_specs=[pl.BlockSpec((seq, QK_PAD), lambda b, h: (b, h)),
                  pl.BlockSpec((seq, QK_PAD), lambda b, h: (b, h)),
                  pl.BlockSpec((seq, V_HEAD), lambda b, h: (b, h))],
        out_specs=pl.BlockSpec((seq, V_HEAD), lambda b, h: (b, h)),
        out_shape=jax.ShapeDtypeStruct((batch * seq, MLA_HEADS * V_HEAD), BF16),
        compiler_params=_params("parallel", "parallel"),
        name="prompt_attention",
    )(q, k, v)


def _sattn_body(q_ref, cn_ref, kn_ref, cc_ref, ck_ref, wuk_ref, wuv_ref, o_ref, qa_scr, qp_scr,
                *, dec_seq, key_tile):
    for h in range(MLA_HEADS):
        qn = q_ref[:, h * QK_PAD:h * QK_PAD + QK_NOPE]
        qa_scr[h * dec_seq:(h + 1) * dec_seq, :] = _dot(qn, wuk_ref[h]).astype(BF16)
        qp_scr[h * dec_seq:(h + 1) * dec_seq, :] = q_ref[:, h * QK_PAD + QK_NOPE:(h + 1) * QK_PAD]
    qa = qa_scr[...]
    qp = qp_scr[...]
    rows = MLA_HEADS * dec_seq

    def update(carry, s, c):
        m, l, acc = carry
        m_new = jnp.maximum(m, jnp.max(s, axis=-1, keepdims=True))
        alpha = jnp.exp(m - m_new)
        p = jnp.exp(s - m_new)
        l = alpha * l + jnp.sum(p, axis=-1, keepdims=True)
        acc = alpha * acc + _dot(p.astype(BF16), c)
        return m_new, l, acc

    carry = (jnp.full((rows, 1), NEG_BIG, F32), jnp.zeros((rows, 1), F32), jnp.zeros((rows, KV_LORA), F32))
    past = cc_ref.shape[0]
    for t in range(past // key_tile):
        c = cc_ref[t * key_tile:(t + 1) * key_tile, :].astype(BF16)
        kp = ck_ref[t * key_tile:(t + 1) * key_tile, :].astype(BF16)
        s = _dot_nt(qa, c) + _dot_nt(qp[:, :QK_ROPE], kp)
        carry = update(carry, s, c)
    cn = cn_ref[...]
    s = _dot_nt(qa, cn) + _dot_nt(qp, kn_ref[...])
    _, l, acc = update(carry, s, cn)
    o_lat = (acc / l).astype(BF16)
    for h in range(MLA_HEADS):
        o_ref[:, h * V_HEAD:(h + 1) * V_HEAD] = _dot(
            o_lat[h * dec_seq:(h + 1) * dec_seq, :], wuv_ref[h]).astype(o_ref.dtype)


def _sample_attention(q, ckv_b, krot_b, cache_c, cache_k, wuk_t, wuv_h, layer, n_prompt_rows, dec_batch, dec_seq):
    rb = n_prompt_rows // dec_seq
    past = cache_c.shape[2]
    key_tile = min(past, 1024)
    per_layer = lambda s: pl.BlockSpec((None,) + s, lambda b: (layer, 0, 0, 0))
    return pl.pallas_call(
        functools.partial(_sattn_body, dec_seq=dec_seq, key_tile=key_tile),
        grid=(dec_batch,),
        in_specs=[pl.BlockSpec((dec_seq, MLA_HEADS * QK_PAD), lambda b: (rb + b, 0)),
                  pl.BlockSpec((dec_seq, KV_LORA), lambda b: (rb + b, 0)),
                  pl.BlockSpec((dec_seq, LANES), lambda b: (rb + b, 0)),
                  pl.BlockSpec((None, None, past, KV_LORA), lambda b: (layer, b, 0, 0)),
                  pl.BlockSpec((None, None, past, QK_ROPE), lambda b: (layer, b, 0, 0)),
                  per_layer((MLA_HEADS, QK_NOPE, KV_LORA)),
                  per_layer((MLA_HEADS, KV_LORA, V_HEAD))],
        out_specs=pl.BlockSpec((dec_seq, MLA_HEADS * V_HEAD), lambda b: (b, 0)),
        out_shape=jax.ShapeDtypeStruct((dec_batch * dec_seq, MLA_HEADS * V_HEAD), BF16),
        scratch_shapes=[pltpu.VMEM((MLA_HEADS * dec_seq, KV_LORA), BF16),
                        pltpu.VMEM((MLA_HEADS * dec_seq, LANES), BF16)],
        compiler_params=_params("parallel"),
        name="sample_attention",
    )(q, ckv_b, krot_b, cache_c, cache_k, wuk_t, wuv_h)


def _hgrn_tables(length):
    t = np.arange(length)[:, None]
    r = np.arange(length)[None, :]
    groups = [(r <= t), (r > t)]
    masks = [(r == t)]
    m = length // 2
    while m >= 1:
        blk = t // (2 * m)
        start2 = blk * 2 * m + m
        second = (t % (2 * m)) >= m
        incl = second & (r >= start2) & (r <= t)
        excl = (~second) & (r > t) & (r < start2)
        groups.append(incl | excl)
        r_blk = r // (2 * m)
        r_first = (r % (2 * m)) < m
        masks.append(second & r_first & (r_blk == blk))
        m //= 2
    return (np.concatenate(groups, axis=0).astype(np.float32),
            np.stack(masks, axis=0).astype(np.float32))


def _hgrn_body(*refs, length, n_chunks, group, has_state):
    if has_state:
        q_ref, k_ref, lf_ref, v_ref, g_ref, gn_ref, sum_ref, mask_ref, s0_ref, o_ref, sout_ref, st_scr = refs
    else:
        q_ref, k_ref, lf_ref, v_ref, g_ref, gn_ref, sum_ref, mask_ref, o_ref, sout_ref, st_scr = refs
    tb = pl.program_id(2)

    @pl.when(tb == 0)
    def _():
        for g in range(group):
            st_scr[g] = s0_ref[g].T if has_state else jnp.zeros((HG_DV, HG_DK), F32)

    n_lev = mask_ref.shape[0] - 1
    summat = sum_ref[...]
    gn = gn_ref[...]

    pack_levels = length % LANES == 0 and (n_lev + 1) % 2 == 0
    zeros_k = jnp.zeros((length, HG_DK), BF16)

    def mixed_rows(q, k, half):
        parts = []
        for r in range(0, length, half):
            parts.append((q if (r // half) % 2 else k)[r:r + half])
        return jnp.concatenate(parts, axis=0)

    def level_operands(rows, g, e_all):
        cols = slice(g * HG_DK, (g + 1) * HG_DK)
        q = q_ref[rows, cols]
        k = k_ref[rows, cols]
        qs = [q.astype(BF16)]
        ks = [k.astype(BF16)]
        half = length // 2
        for lev in range(n_lev):
            e = e_all[(2 + lev) * length:(3 + lev) * length]
            if half % 8 == 0:
                u = (mixed_rows(q, k, half) * e).astype(BF16)
                qs.append(u)
                ks.append(u)
            else:
                qs.append((q * e).astype(BF16))
                ks.append((k * e).astype(BF16))
            half //= 2
        q_in = (q * e_all[0:length]).astype(BF16)
        k_out = (k * e_all[length:2 * length]).astype(BF16)
        return qs, ks, q_in, k_out

    def level_scores(qs, ks):
        if pack_levels:
            out = []
            for p in range(0, n_lev + 1, 2):
                lhs = jnp.concatenate([qs[p], qs[p + 1]], axis=1)
                rhs = jnp.concatenate([jnp.concatenate([ks[p], zeros_k], axis=1),
                                       jnp.concatenate([zeros_k, ks[p + 1]], axis=1)], axis=0)
                a = _dot_nt(lhs, rhs)
                out += [a[:, :length], a[:, length:]]
            return out
        return [_dot_nt(qs[p], ks[p]) for p in range(n_lev + 1)]

    def chunk(c, _):
        rows = pl.ds(pl.multiple_of(c * length, length), length)
        e_heads = []
        for pair in range(group // 2):
            cols2 = slice(2 * pair * HG_DK, (2 * pair + 2) * HG_DK)
            lf2 = lf_ref[rows, cols2] * LOG2_E
            hi = lf2.astype(BF16)
            mid = (lf2 - hi.astype(F32)).astype(BF16)
            e_pair = jnp.exp2(_dot(summat, jnp.concatenate([hi, mid], axis=0)))
            e_heads += [e_pair[:, :HG_DK], e_pair[:, HG_DK:]]
        work = []
        for g in range(group):
            qs, ks, q_in, k_out = level_operands(rows, g, e_heads[g])
            st = st_scr[g]
            work.append((level_scores(qs, ks), _dot_nt(q_in, st.astype(BF16)), k_out, st))
        for g in range(group):
            cols = slice(g * HG_DK, (g + 1) * HG_DK)
            scores, o_in, k_out, st = work[g]
            v = v_ref[rows, cols]
            att = mask_ref[0] * scores[0]
            for p in range(1, n_lev + 1):
                att = att + mask_ref[p] * scores[p]
            o = o_in + _dot(att.astype(BF16), v)
            e_last = e_heads[g][length - 1:length, :]
            st_scr[g] = st * e_last + _dot_tn(v, k_out)
            o_ref[rows, cols] = (_rms(o, gn) * g_ref[rows, cols]).astype(o_ref.dtype)
        return 0

    lax.fori_loop(0, n_chunks, chunk, 0)

    @pl.when(tb == pl.num_programs(2) - 1)
    def _():
        for g in range(group):
            sout_ref[g] = st_scr[g].T


def _hgrn(q, k, logf, v, gate, gn, state0, *, layer, row_block0, n_streams, stream_len, length, block_len, group):
    n_tb = stream_len // block_len
    n_chunks = block_len // length
    width = group * HG_DK
    summat, masks = _hgrn_tables(length)
    summat = jnp.asarray(np.concatenate([summat, summat], axis=1), BF16)
    masks = jnp.asarray(masks, F32)
    tok = lambda: pl.BlockSpec((block_len, width), lambda n, h, t: (row_block0 + n * n_tb + t, h))
    st_spec = pl.BlockSpec((None, group, HG_DK, HG_DV), lambda n, h, t: (n, h, 0, 0))
    in_specs = [tok(), tok(), tok(), tok(), tok(), pl.BlockSpec((1, HG_DV), lambda n, h, t: (0, 0)),
                pl.BlockSpec(summat.shape, lambda n, h, t: (0, 0)),
                pl.BlockSpec(masks.shape, lambda n, h, t: (0, 0, 0))]
    args = [q, k, logf, v, gate, gn.reshape(1, HG_DV), summat, masks]
    if state0 is not None:
        in_specs.append(pl.BlockSpec((None, None, group, HG_DK, HG_DV), lambda n, h, t: (layer, n, h, 0, 0)))
        args.append(state0)
    return pl.pallas_call(
        functools.partial(_hgrn_body, length=length, n_chunks=n_chunks, group=group,
                          has_state=state0 is not None),
        grid=(n_streams, HG_HEADS // group, n_tb),
        in_specs=in_specs,
        out_specs=[pl.BlockSpec((block_len, width), lambda n, h, t: (n * n_tb + t, h)), st_spec],
        out_shape=[jax.ShapeDtypeStruct((n_streams * stream_len, HG_HEADS * HG_DV), BF16),
                   jax.ShapeDtypeStruct((n_streams, HG_HEADS, HG_DK, HG_DV), F32)],
        scratch_shapes=[pltpu.VMEM((group, HG_DV, HG_DK), F32)],
        compiler_params=_params("parallel", "parallel", "arbitrary"),
        name="hgrn_state" if state0 is not None else "hgrn_prompt",
    )(*args)


def _merge_body(oa_ref, oh_ref, woa_ref, wob_ref, ga_ref, gb_ref, o_ref):
    a = _dot(oa_ref[...], woa_ref[...])
    b = _dot(oh_ref[...], wob_ref[...])
    o_ref[...] = (ga_ref[...] * a + gb_ref[...] * b).astype(o_ref.dtype)


def _merge(oa, oh, woa, wob, ga, gb, layer, tm, tn):
    rows, k = oa.shape
    n = woa.shape[2]
    x_spec = pl.BlockSpec((tm, k), lambda j, i: (i, 0))
    w_spec = pl.BlockSpec((None, k, tn), lambda j, i: (layer, 0, j))
    t_spec = pl.BlockSpec((tm, tn), lambda j, i: (i, j))
    return pl.pallas_call(
        _merge_body,
        grid=(n // tn, rows // tm),
        in_specs=[x_spec, x_spec, w_spec, w_spec, t_spec, t_spec],
        out_specs=t_spec,
        out_shape=jax.ShapeDtypeStruct((rows, n), BF16),
        compiler_params=_params("parallel", "parallel"),
        name="gated_merge",
    )(oa, oh, woa, wob, ga, gb)


def _outproj_body(m_ref, w_ref, x_ref, g1_ref, g2_ref, x1_ref, h2_ref):
    y = _dot(m_ref[...], w_ref[...])
    x1 = x_ref[...] + _rms(y, g1_ref[...])
    x1_ref[...] = x1
    h2_ref[...] = _rms(x1, g2_ref[...]).astype(h2_ref.dtype)


def _outproj(mix, w, x, g_post, g_pre2, layer, tm):
    rows, d = x.shape
    row = pl.BlockSpec((tm, d), lambda i: (i, 0))
    vec = pl.BlockSpec((1, d), lambda i: (0, 0))
    return pl.pallas_call(
        _outproj_body,
        grid=(rows // tm,),
        in_specs=[row, pl.BlockSpec((None, d, d), lambda i: (layer, 0, 0)), row, vec, vec],
        out_specs=[row, row],
        out_shape=[jax.ShapeDtypeStruct((rows, d), F32), jax.ShapeDtypeStruct((rows, d), BF16)],
        compiler_params=_params("parallel"),
        name="out_proj",
    )(mix, w, x, g_post.reshape(1, d), g_pre2.reshape(1, d))


def _mlp_body(h_ref, wu_ref, wd_ref, x_ref, g_ref, *rest, tail_start):
    o_ref, acc_ref = rest[0], rest[-1]
    f = pl.program_id(1)

    @pl.when(f == 0)
    def _():
        acc_ref[...] = jnp.zeros_like(acc_ref)

    u = jnp.maximum(_dot(h_ref[...], wu_ref[...]), 0.0)
    acc_ref[...] += _dot((u * u).astype(BF16), wd_ref[...])

    @pl.when(f == pl.num_programs(1) - 1)
    def _():
        y = x_ref[...] + _rms(acc_ref[...], g_ref[...])
        o_ref[...] = y
        if tail_start is not None:
            @pl.when(pl.program_id(0) == pl.num_programs(0) - 1)
            def _():
                rest[1][...] = y[tail_start:, :]


def _mlp(h2, w_up, w_down, x1, g, layer, tm, tf, split_rows=None):
    rows, d = x1.shape
    ff = w_up.shape[2]
    row = pl.BlockSpec((tm, d), lambda i, f: (i, 0))
    if split_rows is None:
        out_specs, out_shape, tail_start = row, jax.ShapeDtypeStruct((rows, d), F32), None
    else:
        tail = rows - split_rows
        tail_start = tm - tail
        assert 0 <= tail_start and tail % 8 == 0
        out_specs = [row, pl.BlockSpec((tail, d), lambda i, f: (0, 0))]
        out_shape = [jax.ShapeDtypeStruct((split_rows, d), F32), jax.ShapeDtypeStruct((tail, d), F32)]
    return pl.pallas_call(
        functools.partial(_mlp_body, tail_start=tail_start),
        grid=(rows // tm, ff // tf),
        in_specs=[row, pl.BlockSpec((None, d, tf), lambda i, f: (layer, 0, f)),
                  pl.BlockSpec((None, tf, d), lambda i, f: (layer, f, 0)), row,
                  pl.BlockSpec((1, d), lambda i, f: (0, 0))],
        out_specs=out_specs,
        out_shape=out_shape,
        scratch_shapes=[pltpu.VMEM((tm, d), F32)],
        compiler_params=_params("arbitrary", "arbitrary"),
        name="mlp",
    )(h2, w_up, w_down, x1, g.reshape(1, d))


def _shift_cast_body(a_ref, b_ref, o_ref, *, shift):
    tn = o_ref.shape[1]
    o_ref[:, :tn - shift] = a_ref[:, shift:].astype(o_ref.dtype)
    o_ref[:, tn - shift:] = b_ref[:, :shift].astype(o_ref.dtype)


def _shift_cast(w, col0, tr, tn):
    depth, k, n_all = w.shape
    n = n_all - col0
    shift = col0 % LANES
    base = col0 - shift
    assert shift and base % tn == 0 and n % tn == 0 and k % tr == 0
    return pl.pallas_call(
        functools.partial(_shift_cast_body, shift=shift),
        grid=(depth, k // tr, n // tn),
        in_specs=[pl.BlockSpec((None, tr, tn), lambda l, r, j: (l, r, base // tn + j)),
                  pl.BlockSpec((None, tr, LANES), lambda l, r, j: (l, r, (base + (j + 1) * tn) // LANES))],
        out_specs=pl.BlockSpec((None, tr, tn), lambda l, r, j: (l, r, j)),
        out_shape=jax.ShapeDtypeStruct((depth, k, n), BF16),
        compiler_params=_params("parallel", "parallel", "parallel"),
        name="shift_cast",
    )(w, w)


def _pad_lanes(w):
    return jnp.concatenate([w, jnp.zeros(w.shape[:-1] + (LANES - w.shape[-1],), w.dtype)], axis=-1)


def _swap_halves(w):
    half = w.shape[-1] // 2
    return jnp.concatenate([w[..., half:], w[..., :half]], axis=-1)


def _prep_weights(w_in, w_uq, w_uk, w_uv):
    depth = w_in.shape[0]
    base = Q_LORA + KV_LORA
    kpe = w_in[:, :, base:base + QK_ROPE]
    w_a = jnp.concatenate([w_in[:, :, :base], _pad_lanes(kpe), _pad_lanes(_swap_halves(kpe))], axis=2).astype(BF16)
    w_wide = _shift_cast(w_in, base + QK_ROPE, 512, 1024)
    uq = w_uq.reshape(depth, Q_LORA, MLA_HEADS // 2, 2, QK_HEAD)
    nope = uq[..., :QK_NOPE]
    pe = _pad_lanes(uq[..., QK_NOPE:])
    pe_sw = _pad_lanes(_swap_halves(uq[..., QK_NOPE:]))
    w_q = jnp.concatenate([nope[:, :, :, 0], pe[:, :, :, 0], nope[:, :, :, 1], pe[:, :, :, 1],
                           pe_sw[:, :, :, 0], pe_sw[:, :, :, 1]], axis=-1)
    w_q = w_q.reshape(depth, Q_LORA, -1).astype(BF16)
    uk = w_uk.reshape(depth, KV_LORA, MLA_HEADS // 2, 2 * QK_NOPE)
    uv = w_uv.reshape(depth, KV_LORA, MLA_HEADS // 2, 2 * V_HEAD)
    w_kv = jnp.concatenate([uk, uv], axis=-1).reshape(depth, KV_LORA, -1).astype(BF16)
    wuk_t = jnp.transpose(w_uk, (0, 2, 3, 1)).astype(BF16)
    wuv_h = jnp.transpose(w_uv, (0, 2, 1, 3)).astype(BF16)
    return w_a, w_wide, w_q, w_kv, wuk_t, wuv_h


def _rope_tables(positions):
    half = QK_ROPE // 2
    inv = ROPE_THETA ** (-jnp.arange(half, dtype=F32) / half)
    ang = positions.astype(F32)[:, None] * inv[None, :]
    cos, sin = jnp.cos(ang), jnp.sin(ang)
    zeros = jnp.zeros((positions.shape[0], LANES - QK_ROPE), F32)
    return (jnp.concatenate([cos, cos, zeros], axis=1), jnp.concatenate([-sin, sin, zeros], axis=1))


def kernel(x_prompt, x_sample, cache_ckv, cache_kpe, state_hgrn, pre_mix_g, w_in, q_norm_g, w_uq, kv_norm_g,
           w_uk, w_uv, w_oa, hg_lb, hg_norm_g, w_ob, w_out, post_mix_g, pre_mlp_g, w_up, w_down, post_mlp_g):
    batch, seq, d = x_prompt.shape
    dec_batch, dec_seq, _ = x_sample.shape
    depth = w_in.shape[0]
    past = cache_ckv.shape[2]
    n_p = batch * seq
    n_s = dec_batch * dec_seq
    rows = n_p + n_s
    tm = 768
    tm_wide = 1408
    assert rows % tm == 0 and rows % tm_wide == 0 and n_p % 1024 == 0

    x = jnp.concatenate([x_prompt.reshape(n_p, d), x_sample.reshape(n_s, d)], axis=0)
    pos = jnp.concatenate([jnp.tile(jnp.arange(seq, dtype=jnp.int32), batch),
                           jnp.tile(past + jnp.arange(dec_seq, dtype=jnp.int32), dec_batch)])
    cos_t, sin_t = _rope_tables(pos)
    cos_q, sin_q = cos_t * ATTN_SCALE, sin_t * ATTN_SCALE

    w_a, w_wide, w_q, w_kv, wuk_t, wuv_h = _prep_weights(w_in, w_uq, w_uk, w_uv)
    w_oa_b, w_ob_b, w_out_b = w_oa.astype(BF16), w_ob.astype(BF16), w_out.astype(BF16)
    w_up_b, w_down_b = w_up.astype(BF16), w_down.astype(BF16)

    ckv_out, kpe_out, st_p_out, st_s_out = [], [], [], []
    for l in range(depth):
        h = _rmsnorm(x, pre_mix_g[l], tm)
        qn, ckv, ckv_b, krot, krot_b = _latent(h, w_a, q_norm_g[l], kv_norm_g[l], cos_t, sin_t, l, tm)
        wide = lambda i, *a, tm=tm_wide, **kw: _mm(h, w_wide, *a, layer=l, tm=tm, tn=1024, n=d, col0=i * d, **kw)
        (hq,) = wide(0, _epi_silu, [(1024, F32)], name="proj_hq")
        hk, logf = wide(1, functools.partial(_epi_forget, layer=l), [(1024, F32), (1024, F32)],
                        col_args=[(hg_lb, 1024)], tm=tm, name="proj_hf")
        (hv,) = wide(2, _epi_id, [(1024, BF16)], name="proj_hi")
        (hgate,) = wide(3, _epi_silu, [(1024, F32)], name="proj_hg")
        (ga,) = wide(4, _epi_sigmoid, [(1024, F32)], name="proj_ga")
        (gb,) = wide(5, _epi_sigmoid, [(1024, F32)], name="proj_gb")
        (q,) = _mm_rows(qn, w_q, _epi_q, [(2 * QK_PAD, BF16)], layer=l, tm=tm, tn=6 * LANES,
                        tile_args=[(cos_q, LANES), (sin_q, LANES)], name="proj_q")
        keys, vals = _mm_rows(ckv_b, w_kv, _epi_kv, [(2 * QK_PAD, BF16), (2 * V_HEAD, BF16)], layer=l, tm=1024,
                              tn=4 * LANES, rows=n_p, tile_args=[(krot_b, LANES)], name="proj_kv")
        oa_p = _prompt_attention(q, keys, vals, batch, seq, 256)
        oa_s = _sample_attention(q, ckv_b, krot_b, cache_ckv, cache_kpe, wuk_t, wuv_h, l, n_p, dec_batch, dec_seq)
        oa = jnp.concatenate([oa_p, oa_s], axis=0)
        oh_p, st_p = _hgrn(hq, hk, logf, hv, hgate, hg_norm_g[l], None, layer=l,
                           row_block0=0, n_streams=batch, stream_len=seq, length=2 * CHUNK,
                           block_len=512, group=8)
        oh_s, st_s = _hgrn(hq, hk, logf, hv, hgate, hg_norm_g[l], state_hgrn, layer=l,
                           row_block0=n_p // dec_seq, n_streams=dec_batch, stream_len=dec_seq, length=dec_seq,
                           block_len=dec_seq, group=4)
        oh = jnp.concatenate([oh_p, oh_s], axis=0)
        mix = _merge(oa, oh, w_oa_b, w_ob_b, ga, gb, l, tm, 512)
        x1, h2 = _outproj(mix, w_out_b, x, post_mix_g[l], pre_mlp_g[l], l, 384)
        if l + 1 < depth:
            x = _mlp(h2, w_up_b, w_down_b, x1, post_mlp_g[l], l, tm, 512)
        else:
            y_p, y_s = _mlp(h2, w_up_b, w_down_b, x1, post_mlp_g[l], l, tm, 512, split_rows=n_p)

        ckv_out.append(ckv)
        kpe_out.append(krot)
        st_p_out.append(st_p)
        st_s_out.append(st_s)

    ckv_all = jnp.stack(ckv_out)
    kpe_all = jnp.stack(kpe_out)
    return (y_p.reshape(batch, seq, d),
            y_s.reshape(dec_batch, dec_seq, d),
            ckv_all[:, :n_p].reshape(depth, batch, seq, KV_LORA),
            kpe_all[:, :n_p].reshape(depth, batch, seq, QK_ROPE),
            jnp.stack(st_p_out),
            ckv_all[:, n_p:].reshape(depth, dec_batch, dec_seq, KV_LORA),
            kpe_all[:, n_p:].reshape(depth, dec_batch, dec_seq, QK_ROPE),
            jnp.stack(st_s_out))
```

```python
import functools

import jax
import jax.numpy as jnp
import numpy as np
from jax import lax
from jax.experimental import pallas as pl
from jax.experimental.pallas import tpu as pltpu

F32 = jnp.float32
BF16 = jnp.bfloat16

CHUNK = 64
MLA_HEADS = 16
QK_NOPE = 128
QK_ROPE = 64
QK_HEAD = QK_NOPE + QK_ROPE
V_HEAD = 128
Q_LORA = 512
KV_LORA = 512
ROPE_THETA = 10000.0
ATTN_SCALE = QK_HEAD ** -0.5
HG_HEADS = 16
HG_DK = 128
HG_DV = 128
EPS = 1e-6

LANES = 128
QK_PAD = 2 * LANES
NEG_BIG = -1e30
LOG2_E = 1.4426950408889634
VMEM_LIMIT = 56 * 1024 * 1024


def _params(*sem):
    return pltpu.CompilerParams(dimension_semantics=sem, vmem_limit_bytes=VMEM_LIMIT)


def _rms(x, g):
    return x * lax.rsqrt(jnp.mean(x * x, axis=-1, keepdims=True) + EPS) * g


def _dot(a, b):
    return jnp.dot(a, b, preferred_element_type=F32)


def _dot_nt(a, b):
    return lax.dot_general(a, b, (((1,), (1,)), ((), ())), preferred_element_type=F32)


def _dot_tn(a, b):
    return lax.dot_general(a, b, (((0,), (0,)), ((), ())), preferred_element_type=F32)


def _rmsnorm_body(x_ref, g_ref, o_ref):
    o_ref[...] = _rms(x_ref[...], g_ref[...]).astype(o_ref.dtype)


def _rmsnorm(x, g, tm):
    rows, d = x.shape
    return pl.pallas_call(
        _rmsnorm_body,
        grid=(rows // tm,),
        in_specs=[pl.BlockSpec((tm, d), lambda i: (i, 0)),
                  pl.BlockSpec((1, d), lambda i: (0, 0))],
        out_specs=pl.BlockSpec((tm, d), lambda i: (i, 0)),
        out_shape=jax.ShapeDtypeStruct((rows, d), BF16),
        compiler_params=_params("parallel"),
        name="rmsnorm",
    )(x, g.reshape(1, d))


def _mm_body(x_ref, w_ref, *rest, epi, n_extra):
    extra = rest[:n_extra]
    outs = rest[n_extra:-1]
    w_scr = rest[-1]

    @pl.when(pl.program_id(1) == 0)
    def _():
        w_scr[...] = w_ref[0].astype(w_scr.dtype)

    acc = _dot_nt(x_ref[...], w_scr[...])
    res = epi(acc, *[e[...] for e in extra])
    for o_ref, r in zip(outs, res):
        o_ref[...] = r.astype(o_ref.dtype)


def _mm(x, w_t, epi, outs, *, layer, row0, n, tm, tn, col_args=(), tile_args=(), name):
    rows, k = x.shape
    assert row0 % 8 == 0 and tn % 8 == 0
    grid = (n // tn, rows // tm)
    in_specs = [pl.BlockSpec((tm, k), lambda j, i: (i, 0)),
                pl.BlockSpec((pl.Element(1), pl.Element(tn), pl.Element(k)),
                             lambda j, i: (layer, pl.multiple_of(row0 + j * tn, 8), 0))]
    args = [x, w_t]
    for arr, width in col_args:
        in_specs.append(pl.BlockSpec((arr.shape[0], width), lambda j, i: (0, j)))
        args.append(arr)
    for arr, width, follows_n in tile_args:
        if follows_n:
            in_specs.append(pl.BlockSpec((tm, width), lambda j, i: (i, j)))
        else:
            in_specs.append(pl.BlockSpec((tm, width), lambda j, i: (i, 0)))
        args.append(arr)
    out_specs = [pl.BlockSpec((tm, width), lambda j, i: (i, j)) for width, _ in outs]
    out_shape = [jax.ShapeDtypeStruct((rows, width * (n // tn)), dt) for width, dt in outs]
    res = pl.pallas_call(
        functools.partial(_mm_body, epi=epi, n_extra=len(col_args) + len(tile_args)),
        grid=grid,
        in_specs=in_specs,
        out_specs=out_specs,
        out_shape=out_shape,
        scratch_shapes=[pltpu.VMEM((tn, k), BF16)],
        compiler_params=_params("parallel", "arbitrary"),
        name=name,
    )(*args)
    return res


def _mm_rows_body(x_ref, w_ref, *rest, epi, n_extra, tn, widths):
    extra = [e[...] for e in rest[:n_extra]]
    outs = rest[n_extra:]
    x = x_ref[...]
    for p in range(w_ref.shape[1] // tn):
        res = epi(_dot(x, w_ref[:, p * tn:(p + 1) * tn]), *extra)
        for o_ref, r, width in zip(outs, res, widths):
            o_ref[:, p * width:(p + 1) * width] = r.astype(o_ref.dtype)


def _mm_rows(x, w, epi, outs, *, layer, tm, tn, rows=None, tile_args=(), name):
    k = x.shape[1]
    rows = x.shape[0] if rows is None else rows
    n = w.shape[2]
    in_specs = [pl.BlockSpec((tm, k), lambda i: (i, 0)),
                pl.BlockSpec((None, k, n), lambda i: (layer, 0, 0))]
    in_specs += [pl.BlockSpec((tm, width), lambda i: (i, 0)) for _, width in tile_args]
    return pl.pallas_call(
        functools.partial(_mm_rows_body, epi=epi, n_extra=len(tile_args), tn=tn,
                          widths=[width for width, _ in outs]),
        grid=(rows // tm,),
        in_specs=in_specs,
        out_specs=[pl.BlockSpec((tm, width * (n // tn)), lambda i: (i, 0)) for width, _ in outs],
        out_shape=[jax.ShapeDtypeStruct((rows, width * (n // tn)), dt) for width, dt in outs],
        compiler_params=_params("parallel"),
        name=name,
    )(x, w, *[arr for arr, _ in tile_args])


def _silu(z):
    return z * jax.nn.sigmoid(z)


def _epi_silu(acc):
    return (_silu(acc),)


def _epi_sigmoid(acc):
    return (jax.nn.sigmoid(acc),)


def _epi_id(acc):
    return (acc,)


def _epi_relu2(acc):
    r = jnp.maximum(acc, 0.0)
    return (r * r,)


def _epi_forget(acc, lb_logits, *, layer):
    mx = jnp.max(lb_logits, axis=0, keepdims=True)
    e = jnp.exp(lb_logits - mx)
    sm = e / jnp.sum(e, axis=0, keepdims=True)
    lb = jnp.zeros_like(mx)
    for i in range(1, layer + 1):
        lb = lb + sm[i:i + 1]
    z = acc
    log_sig = jnp.minimum(z, 0.0) - jnp.log1p(jnp.exp(-jnp.abs(z)))
    a = jnp.log(lb)
    c = jnp.log1p(-lb) + log_sig
    hi = jnp.maximum(a, c)
    lo = jnp.minimum(a, c)
    logf = hi + jnp.log1p(jnp.exp(lo - hi))
    kk = (1.0 - lb) * jax.nn.sigmoid(-z)
    return kk, logf


def _epi_q(acc, cos_t, sin_t):
    n0 = acc[:, 0 * LANES:1 * LANES] * ATTN_SCALE
    p0 = acc[:, 1 * LANES:2 * LANES] * cos_t + acc[:, 4 * LANES:5 * LANES] * sin_t
    n1 = acc[:, 2 * LANES:3 * LANES] * ATTN_SCALE
    p1 = acc[:, 3 * LANES:4 * LANES] * cos_t + acc[:, 5 * LANES:6 * LANES] * sin_t
    return (jnp.concatenate([n0, p0, n1, p1], axis=1),)


def _epi_kv(acc, krot):
    krot = krot.astype(F32)
    keys = jnp.concatenate([acc[:, 0:LANES], krot, acc[:, LANES:2 * LANES], krot], axis=1)
    return keys, acc[:, 2 * LANES:4 * LANES]


def _latent_body(h_ref, w_ref, gq_ref, gkv_ref, cos_ref, sin_ref,
                 qn_ref, ckv_ref, ckvb_ref, krot_ref, krotb_ref, w_scr):
    base = Q_LORA + KV_LORA
    half = QK_ROPE // 2

    @pl.when(pl.program_id(0) == 0)
    def _():
        w_scr[...] = jnp.zeros_like(w_scr)
        w_scr[0:base + QK_ROPE, :] = w_ref[...].astype(w_scr.dtype)
        w_scr[base + LANES:base + LANES + half, :] = w_ref[base + half:base + QK_ROPE, :].astype(w_scr.dtype)
        w_scr[base + LANES + half:base + LANES + QK_ROPE, :] = w_ref[base:base + half, :].astype(w_scr.dtype)

    acc = _dot_nt(h_ref[...], w_scr[...])
    qn_ref[...] = _rms(acc[:, :Q_LORA], gq_ref[...]).astype(qn_ref.dtype)
    ckv = _rms(acc[:, Q_LORA:base], gkv_ref[...])
    ckv_ref[...] = ckv
    ckvb_ref[...] = ckv.astype(ckvb_ref.dtype)
    kr = acc[:, base:base + LANES] * cos_ref[...] + acc[:, base + LANES:base + 2 * LANES] * sin_ref[...]
    krot_ref[...] = kr[:, :QK_ROPE]
    krotb_ref[...] = kr.astype(krotb_ref.dtype)


def _latent(h, w_t, gq, gkv, cos_t, sin_t, layer, tm):
    rows, d = h.shape
    n_in = Q_LORA + KV_LORA + QK_ROPE
    row = lambda width: pl.BlockSpec((tm, width), lambda i: (i, 0))
    const = lambda r, width: pl.BlockSpec((r, width), lambda i: (0, 0))
    return pl.pallas_call(
        _latent_body,
        grid=(rows // tm,),
        in_specs=[row(d), pl.BlockSpec((None, n_in, d), lambda i: (layer, 0, 0)),
                  const(1, Q_LORA), const(1, KV_LORA), row(LANES), row(LANES)],
        out_specs=[row(Q_LORA), row(KV_LORA), row(KV_LORA), row(QK_ROPE), row(LANES)],
        out_shape=[jax.ShapeDtypeStruct((rows, Q_LORA), BF16),
                   jax.ShapeDtypeStruct((rows, KV_LORA), F32),
                   jax.ShapeDtypeStruct((rows, KV_LORA), BF16),
                   jax.ShapeDtypeStruct((rows, QK_ROPE), F32),
                   jax.ShapeDtypeStruct((rows, LANES), BF16)],
        scratch_shapes=[pltpu.VMEM((Q_LORA + KV_LORA + 2 * LANES, d), BF16)],
        compiler_params=_params("arbitrary"),
        name="latent_proj",
    )(h, w_t, gq.reshape(1, -1), gkv.reshape(1, -1), cos_t, sin_t)


def _attn_body(q_ref, k_ref, v_ref, o_ref, *, tq, nq):
    r_chunk = lax.broadcasted_iota(jnp.int32, (tq, tq), 0) // CHUNK
    c_chunk = lax.broadcasted_iota(jnp.int32, (tq, tq), 1) // CHUNK
    visible = c_chunk <= r_chunk

    for n_past in range(nq):
        lo = n_past * tq
        q = q_ref[lo:lo + tq, :]
        s_d = jnp.where(visible, _dot_nt(q, k_ref[lo:lo + tq, :]), NEG_BIG)
        m = jnp.max(s_d, axis=-1, keepdims=True)
        if n_past:
            s_p = _dot_nt(q, k_ref[0:lo, :])
            m = jnp.maximum(m, jnp.max(s_p, axis=-1, keepdims=True))
        p_d = jnp.exp(s_d - m)
        l = jnp.sum(p_d, axis=-1, keepdims=True)
        acc = _dot(p_d.astype(BF16), v_ref[lo:lo + tq, :])
        if n_past:
            p_p = jnp.exp(s_p - m)
            l = l + jnp.sum(p_p, axis=-1, keepdims=True)
            acc = acc + _dot(p_p.astype(BF16), v_ref[0:lo, :])
        o_ref[lo:lo + tq, :] = (acc / l).astype(o_ref.dtype)


def _prompt_attention(q, k, v, batch, seq, tq):
    return pl.pallas_call(
        functools.partial(_attn_body, tq=tq, nq=seq // tq),
        grid=(batch, MLA_HEADS),
        in_specs=[pl.BlockSpec((seq, QK_PAD), lambda b, h: (b, h)),
                  pl.BlockSpec((seq, QK_PAD), lambda b, h: (b, h)),
                  pl.BlockSpec((seq, V_HEAD), lambda b, h: (b, h))],
        out_specs=pl.BlockSpec((seq, V_HEAD), lambda b, h: (b, h)),
        out_shape=jax.ShapeDtypeStruct((batch * seq, MLA_HEADS * V_HEAD), BF16),
        compiler_params=_params("parallel", "parallel"),
        name="prompt_attention",
    )(q, k, v)


def _sattn_body(q_ref, cn_ref, kn_ref, cc_ref, ck_ref, wuk_ref, wuv_ref, o_ref, qa_scr, qp_scr,
                *, dec_seq, key_tile):
    for h in range(MLA_HEADS):
        qn = q_ref[:, h * QK_PAD:h * QK_PAD + QK_NOPE]
        qa_scr[h * dec_seq:(h + 1) * dec_seq, :] = _dot(qn, wuk_ref[h]).astype(BF16)
        qp_scr[h * dec_seq:(h + 1) * dec_seq, :] = q_ref[:, h * QK_PAD + QK_NOPE:(h + 1) * QK_PAD]
    qa = qa_scr[...]
    qp = qp_scr[...]
    rows = MLA_HEADS * dec_seq

    def update(carry, s, c):
        m, l, acc = carry
        m_new = jnp.maximum(m, jnp.max(s, axis=-1, keepdims=True))
        alpha = jnp.exp(m - m_new)
        p = jnp.exp(s - m_new)
        l = alpha * l + jnp.sum(p, axis=-1, keepdims=True)
        acc = alpha * acc + _dot(p.astype(BF16), c)
        return m_new, l, acc

    carry = (jnp.full((rows, 1), NEG_BIG, F32), jnp.zeros((rows, 1), F32), jnp.zeros((rows, KV_LORA), F32))
    past = cc_ref.shape[0]
    for t in range(past // key_tile):
        c = cc_ref[t * key_tile:(t + 1) * key_tile, :].astype(BF16)
        kp_t = ck_ref[:, t * key_tile:(t + 1) * key_tile].astype(BF16)
        s = _dot_nt(qa, c) + _dot(qp[:, :QK_ROPE], kp_t)
        carry = update(carry, s, c)
    cn = cn_ref[...]
    s = _dot_nt(qa, cn) + _dot_nt(qp, kn_ref[...])
    _, l, acc = update(carry, s, cn)
    o_lat = (acc / l).astype(BF16)
    for h in range(MLA_HEADS):
        o_ref[:, h * V_HEAD:(h + 1) * V_HEAD] = _dot(
            o_lat[h * dec_seq:(h + 1) * dec_seq, :], wuv_ref[h]).astype(o_ref.dtype)


def _sample_attention(q, ckv_b, krot_b, cache_c, cache_k, wuk_t, wuv_h, layer, n_prompt_rows, dec_batch, dec_seq):
    rb = n_prompt_rows // dec_seq
    past = cache_c.shape[2]
    key_tile = min(past, 1024)
    per_layer = lambda s: pl.BlockSpec((None,) + s, lambda b: (layer, 0, 0, 0))
    return pl.pallas_call(
        functools.partial(_sattn_body, dec_seq=dec_seq, key_tile=key_tile),
        grid=(dec_batch,),
        in_specs=[pl.BlockSpec((dec_seq, MLA_HEADS * QK_PAD), lambda b: (rb + b, 0)),
                  pl.BlockSpec((dec_seq, KV_LORA), lambda b: (rb + b, 0)),
                  pl.BlockSpec((dec_seq, LANES), lambda b: (rb + b, 0)),
                  pl.BlockSpec((None, None, past, KV_LORA), lambda b: (layer, b, 0, 0)),
                  pl.BlockSpec((None, None, QK_ROPE, past), lambda b: (layer, b, 0, 0)),
                  per_layer((MLA_HEADS, QK_NOPE, KV_LORA)),
                  per_layer((MLA_HEADS, KV_LORA, V_HEAD))],
        out_specs=pl.BlockSpec((dec_seq, MLA_HEADS * V_HEAD), lambda b: (b, 0)),
        out_shape=jax.ShapeDtypeStruct((dec_batch * dec_seq, MLA_HEADS * V_HEAD), BF16),
        scratch_shapes=[pltpu.VMEM((MLA_HEADS * dec_seq, KV_LORA), BF16),
                        pltpu.VMEM((MLA_HEADS * dec_seq, LANES), BF16)],
        compiler_params=_params("parallel"),
        name="sample_attention",
    )(q, ckv_b, krot_b, cache_c, cache_k, wuk_t, wuv_h)


def _hgrn_tables(length):
    t = np.arange(length)[:, None]
    r = np.arange(length)[None, :]
    groups = [(r <= t), (r > t)]
    masks = [(r == t)]
    m = length // 2
    while m >= 1:
        blk = t // (2 * m)
        start2 = blk * 2 * m + m
        second = (t % (2 * m)) >= m
        incl = second & (r >= start2) & (r <= t)
        excl = (~second) & (r > t) & (r < start2)
        groups.append(incl | excl)
        r_blk = r // (2 * m)
        r_first = (r % (2 * m)) < m
        masks.append(second & r_first & (r_blk == blk))
        m //= 2
    return (np.concatenate(groups, axis=0).astype(np.float32),
            np.stack(masks, axis=0).astype(np.float32))


def _hgrn_body(*refs, length, n_chunks, group, has_state):
    if has_state:
        q_ref, k_ref, lf_ref, v_ref, g_ref, gn_ref, sum_ref, mask_ref, s0_ref, o_ref, sout_ref, st_scr = refs
    else:
        q_ref, k_ref, lf_ref, v_ref, g_ref, gn_ref, sum_ref, mask_ref, o_ref, sout_ref, st_scr = refs
    tb = pl.program_id(2)

    @pl.when(tb == 0)
    def _():
        for g in range(group):
            st_scr[g] = s0_ref[g].T if has_state else jnp.zeros((HG_DV, HG_DK), F32)

    n_lev = mask_ref.shape[0] - 1
    summat = sum_ref[...]
    gn = gn_ref[...]

    pack_levels = length % LANES == 0 and (n_lev + 1) % 2 == 0
    zeros_k = jnp.zeros((length, HG_DK), BF16)

    def mixed_rows(q, k, half):
        parts = []
        for r in range(0, length, half):
            parts.append((q if (r // half) % 2 else k)[r:r + half])
        return jnp.concatenate(parts, axis=0)

    def level_operands(rows, g, e_all):
        cols = slice(g * HG_DK, (g + 1) * HG_DK)
        q = q_ref[rows, cols]
        k = k_ref[rows, cols]
        qs = [q.astype(BF16)]
        ks = [k.astype(BF16)]
        half = length // 2
        for lev in range(n_lev):
            e = e_all[(2 + lev) * length:(3 + lev) * length]
            if half % 8 == 0:
                u = (mixed_rows(q, k, half) * e).astype(BF16)
                qs.append(u)
                ks.append(u)
            else:
                qs.append((q * e).astype(BF16))
                ks.append((k * e).astype(BF16))
            half //= 2
        q_in = (q * e_all[0:length]).astype(BF16)
        k_out = (k * e_all[length:2 * length]).astype(BF16)
        return qs, ks, q_in, k_out

    def level_scores(qs, ks):
        if pack_levels:
            out = []
            for p in range(0, n_lev + 1, 2):
                lhs = jnp.concatenate([qs[p], qs[p + 1]], axis=1)
                rhs = jnp.concatenate([jnp.concatenate([ks[p], zeros_k], axis=1),
                                       jnp.concatenate([zeros_k, ks[p + 1]], axis=1)], axis=0)
                a = _dot_nt(lhs, rhs)
                out += [a[:, :length], a[:, length:]]
            return out
        return [_dot_nt(qs[p], ks[p]) for p in range(n_lev + 1)]

    def chunk(c, _):
        rows = pl.ds(pl.multiple_of(c * length, length), length)
        e_heads = []
        for pair in range(group // 2):
            cols2 = slice(2 * pair * HG_DK, (2 * pair + 2) * HG_DK)
            lf2 = lf_ref[rows, cols2] * LOG2_E
            hi = lf2.astype(BF16)
            mid = (lf2 - hi.astype(F32)).astype(BF16)
            e_pair = jnp.exp2(_dot(summat, jnp.concatenate([hi, mid], axis=0)))
            e_heads += [e_pair[:, :HG_DK], e_pair[:, HG_DK:]]
        work = []
        for g in range(group):
            qs, ks, q_in, k_out = level_operands(rows, g, e_heads[g])
            st = st_scr[g]
            work.append((level_scores(qs, ks), _dot_nt(q_in, st.astype(BF16)), k_out, st))
        for g in range(group):
            cols = slice(g * HG_DK, (g + 1) * HG_DK)
            scores, o_in, k_out, st = work[g]
            v = v_ref[rows, cols]
            att = mask_ref[0] * scores[0]
            for p in range(1, n_lev + 1):
                att = att + mask_ref[p] * scores[p]
            o = o_in + _dot(att.astype(BF16), v)
            e_last = e_heads[g][length - 1:length, :]
            st_scr[g] = st * e_last + _dot_tn(v, k_out)
            o_ref[rows, cols] = (_rms(o, gn) * g_ref[rows, cols]).astype(o_ref.dtype)
        return 0

    lax.fori_loop(0, n_chunks, chunk, 0)

    @pl.when(tb == pl.num_programs(2) - 1)
    def _():
        for g in range(group):
            sout_ref[g] = st_scr[g].T


def _hgrn(q, k, logf, v, gate, gn, state0, *, layer, row_block0, n_streams, stream_len, length, block_len, group):
    n_tb = stream_len // block_len
    n_chunks = block_len // length
    width = group * HG_DK
    summat, masks = _hgrn_tables(length)
    summat = jnp.asarray(np.concatenate([summat, summat], axis=1), BF16)
    masks = jnp.asarray(masks, F32)
    tok = lambda: pl.BlockSpec((block_len, width), lambda n, h, t: (row_block0 + n * n_tb + t, h))
    st_spec = pl.BlockSpec((None, group, HG_DK, HG_DV), lambda n, h, t: (n, h, 0, 0))
    in_specs = [tok(), tok(), tok(), tok(), tok(), pl.BlockSpec((1, HG_DV), lambda n, h, t: (0, 0)),
                pl.BlockSpec(summat.shape, lambda n, h, t: (0, 0)),
                pl.BlockSpec(masks.shape, lambda n, h, t: (0, 0, 0))]
    args = [q, k, logf, v, gate, gn.reshape(1, HG_DV), summat, masks]
    if state0 is not None:
        in_specs.append(pl.BlockSpec((None, None, group, HG_DK, HG_DV), lambda n, h, t: (layer, n, h, 0, 0)))
        args.append(state0)
    return pl.pallas_call(
        functools.partial(_hgrn_body, length=length, n_chunks=n_chunks, group=group,
                          has_state=state0 is not None),
        grid=(n_streams, HG_HEADS // group, n_tb),
        in_specs=in_specs,
        out_specs=[pl.BlockSpec((block_len, width), lambda n, h, t: (n * n_tb + t, h)), st_spec],
        out_shape=[jax.ShapeDtypeStruct((n_streams * stream_len, HG_HEADS * HG_DV), BF16),
                   jax.ShapeDtypeStruct((n_streams, HG_HEADS, HG_DK, HG_DV), F32)],
        scratch_shapes=[pltpu.VMEM((group, HG_DV, HG_DK), F32)],
        compiler_params=_params("parallel", "parallel", "arbitrary"),
        name="hgrn_state" if state0 is not None else "hgrn_prompt",
    )(*args)


def _merge_body(oa_ref, oh_ref, woa_ref, wob_ref, ga_ref, gb_ref, o_ref):
    a = _dot(oa_ref[...], woa_ref[...])
    b = _dot(oh_ref[...], wob_ref[...])
    o_ref[...] = (ga_ref[...] * a + gb_ref[...] * b).astype(o_ref.dtype)


def _merge(oa, oh, woa, wob, ga, gb, layer, tm, tn):
    rows, k = oa.shape
    n = woa.shape[2]
    x_spec = pl.BlockSpec((tm, k), lambda j, i: (i, 0))
    w_spec = pl.BlockSpec((None, k, tn), lambda j, i: (layer, 0, j))
    t_spec = pl.BlockSpec((tm, tn), lambda j, i: (i, j))
    return pl.pallas_call(
        _merge_body,
        grid=(n // tn, rows // tm),
        in_specs=[x_spec, x_spec, w_spec, w_spec, t_spec, t_spec],
        out_specs=t_spec,
        out_shape=jax.ShapeDtypeStruct((rows, n), BF16),
        compiler_params=_params("parallel", "parallel"),
        name="gated_merge",
    )(oa, oh, woa, wob, ga, gb)


def _outproj_body(m_ref, w_ref, x_ref, g1_ref, g2_ref, x1_ref, h2_ref):
    y = _dot(m_ref[...], w_ref[...])
    x1 = x_ref[...] + _rms(y, g1_ref[...])
    x1_ref[...] = x1
    h2_ref[...] = _rms(x1, g2_ref[...]).astype(h2_ref.dtype)


def _outproj(mix, w, x, g_post, g_pre2, layer, tm):
    rows, d = x.shape
    row = pl.BlockSpec((tm, d), lambda i: (i, 0))
    vec = pl.BlockSpec((1, d), lambda i: (0, 0))
    return pl.pallas_call(
        _outproj_body,
        grid=(rows // tm,),
        in_specs=[row, pl.BlockSpec((None, d, d), lambda i: (layer, 0, 0)), row, vec, vec],
        out_specs=[row, row],
        out_shape=[jax.ShapeDtypeStruct((rows, d), F32), jax.ShapeDtypeStruct((rows, d), BF16)],
        compiler_params=_params("parallel"),
        name="out_proj",
    )(mix, w, x, g_post.reshape(1, d), g_pre2.reshape(1, d))


def _mlp_body(h_ref, wu_ref, wd_ref, x_ref, g_ref, *rest, tail_start):
    o_ref, acc_ref = rest[0], rest[-1]
    f = pl.program_id(1)

    @pl.when(f == 0)
    def _():
        acc_ref[...] = jnp.zeros_like(acc_ref)

    u = jnp.maximum(_dot(h_ref[...], wu_ref[...]), 0.0)
    acc_ref[...] += _dot((u * u).astype(BF16), wd_ref[...])

    @pl.when(f == pl.num_programs(1) - 1)
    def _():
        y = x_ref[...] + _rms(acc_ref[...], g_ref[...])
        o_ref[...] = y
        if tail_start is not None:
            @pl.when(pl.program_id(0) == pl.num_programs(0) - 1)
            def _():
                rest[1][...] = y[tail_start:, :]


def _mlp(h2, w_up, w_down, x1, g, layer, tm, tf, split_rows=None):
    rows, d = x1.shape
    ff = w_up.shape[2]
    row = pl.BlockSpec((tm, d), lambda i, f: (i, 0))
    if split_rows is None:
        out_specs, out_shape, tail_start = row, jax.ShapeDtypeStruct((rows, d), F32), None
    else:
        tail = rows - split_rows
        tail_start = tm - tail
        assert 0 <= tail_start and tail % 8 == 0
        out_specs = [row, pl.BlockSpec((tail, d), lambda i, f: (0, 0))]
        out_shape = [jax.ShapeDtypeStruct((split_rows, d), F32), jax.ShapeDtypeStruct((tail, d), F32)]
    return pl.pallas_call(
        functools.partial(_mlp_body, tail_start=tail_start),
        grid=(rows // tm, ff // tf),
        in_specs=[row, pl.BlockSpec((None, d, tf), lambda i, f: (layer, 0, f)),
                  pl.BlockSpec((None, tf, d), lambda i, f: (layer, f, 0)), row,
                  pl.BlockSpec((1, d), lambda i, f: (0, 0))],
        out_specs=out_specs,
        out_shape=out_shape,
        scratch_shapes=[pltpu.VMEM((tm, d), F32)],
        compiler_params=_params("arbitrary", "arbitrary"),
        name="mlp",
    )(h2, w_up, w_down, x1, g.reshape(1, d))


def _pad_lanes(w):
    return jnp.concatenate([w, jnp.zeros(w.shape[:-1] + (LANES - w.shape[-1],), w.dtype)], axis=-1)


def _swap_halves(w):
    half = w.shape[-1] // 2
    return jnp.concatenate([w[..., half:], w[..., :half]], axis=-1)


def _prep_weights(w_uq, w_uk, w_uv):
    depth = w_uq.shape[0]
    uq = w_uq.reshape(depth, Q_LORA, MLA_HEADS // 2, 2, QK_HEAD)
    nope = uq[..., :QK_NOPE]
    pe = _pad_lanes(uq[..., QK_NOPE:])
    pe_sw = _pad_lanes(_swap_halves(uq[..., QK_NOPE:]))
    w_q = jnp.concatenate([nope[:, :, :, 0], pe[:, :, :, 0], nope[:, :, :, 1], pe[:, :, :, 1],
                           pe_sw[:, :, :, 0], pe_sw[:, :, :, 1]], axis=-1)
    w_q = w_q.reshape(depth, Q_LORA, -1).astype(BF16)
    uk = w_uk.reshape(depth, KV_LORA, MLA_HEADS // 2, 2 * QK_NOPE)
    uv = w_uv.reshape(depth, KV_LORA, MLA_HEADS // 2, 2 * V_HEAD)
    w_kv = jnp.concatenate([uk, uv], axis=-1).reshape(depth, KV_LORA, -1).astype(BF16)
    wuk_t = jnp.transpose(w_uk, (0, 2, 3, 1)).astype(BF16)
    wuv_h = jnp.transpose(w_uv, (0, 2, 1, 3)).astype(BF16)
    return w_q, w_kv, wuk_t, wuv_h


def _rope_tables(positions):
    half = QK_ROPE // 2
    inv = ROPE_THETA ** (-jnp.arange(half, dtype=F32) / half)
    ang = positions.astype(F32)[:, None] * inv[None, :]
    cos, sin = jnp.cos(ang), jnp.sin(ang)
    zeros = jnp.zeros((positions.shape[0], LANES - QK_ROPE), F32)
    return (jnp.concatenate([cos, cos, zeros], axis=1), jnp.concatenate([-sin, sin, zeros], axis=1))


def kernel(x_prompt, x_sample, cache_ckv, cache_kpe, state_hgrn, pre_mix_g, w_in, q_norm_g, w_uq, kv_norm_g,
           w_uk, w_uv, w_oa, hg_lb, hg_norm_g, w_ob, w_out, post_mix_g, pre_mlp_g, w_up, w_down, post_mlp_g):
    batch, seq, d = x_prompt.shape
    dec_batch, dec_seq, _ = x_sample.shape
    depth = w_in.shape[0]
    past = cache_ckv.shape[2]
    n_p = batch * seq
    n_s = dec_batch * dec_seq
    rows = n_p + n_s
    tm = 768
    tm_wide = 1408
    assert rows % tm == 0 and rows % tm_wide == 0 and n_p % 1024 == 0

    x = jnp.concatenate([x_prompt.reshape(n_p, d), x_sample.reshape(n_s, d)], axis=0)
    pos = jnp.concatenate([jnp.tile(jnp.arange(seq, dtype=jnp.int32), batch),
                           jnp.tile(past + jnp.arange(dec_seq, dtype=jnp.int32), dec_batch)])
    cos_t, sin_t = _rope_tables(pos)
    cos_q, sin_q = cos_t * ATTN_SCALE, sin_t * ATTN_SCALE

    w_in_t = jnp.transpose(w_in, (0, 2, 1))
    cache_kpe_t = jnp.transpose(cache_kpe, (0, 1, 3, 2))
    wide0 = Q_LORA + KV_LORA + QK_ROPE
    w_q, w_kv, wuk_t, wuv_h = _prep_weights(w_uq, w_uk, w_uv)
    w_oa_b, w_ob_b, w_out_b = w_oa.astype(BF16), w_ob.astype(BF16), w_out.astype(BF16)
    w_up_b, w_down_b = w_up.astype(BF16), w_down.astype(BF16)

    ckv_out, kpe_out, st_p_out, st_s_out = [], [], [], []
    for l in range(depth):
        h = _rmsnorm(x, pre_mix_g[l], tm)
        qn, ckv, ckv_b, krot, krot_b = _latent(h, w_in_t, q_norm_g[l], kv_norm_g[l], cos_t, sin_t, l, tm)
        wide = lambda i, *a, tm=tm_wide, **kw: _mm(h, w_in_t, *a, layer=l, row0=wide0 + i * d, n=d, tm=tm,
                                                   tn=1024, **kw)
        (hq,) = wide(0, _epi_silu, [(1024, F32)], name="proj_hq")
        hk, logf = wide(1, functools.partial(_epi_forget, layer=l), [(1024, F32), (1024, F32)],
                        col_args=[(hg_lb, 1024)], tm=tm, name="proj_hf")
        (hv,) = wide(2, _epi_id, [(1024, BF16)], name="proj_hi")
        (hgate,) = wide(3, _epi_silu, [(1024, F32)], name="proj_hg")
        (ga,) = wide(4, _epi_sigmoid, [(1024, F32)], name="proj_ga")
        (gb,) = wide(5, _epi_sigmoid, [(1024, F32)], name="proj_gb")
        (q,) = _mm_rows(qn, w_q, _epi_q, [(2 * QK_PAD, BF16)], layer=l, tm=tm, tn=6 * LANES,
                        tile_args=[(cos_q, LANES), (sin_q, LANES)], name="proj_q")
        keys, vals = _mm_rows(ckv_b, w_kv, _epi_kv, [(2 * QK_PAD, BF16), (2 * V_HEAD, BF16)], layer=l, tm=1024,
                              tn=4 * LANES, rows=n_p, tile_args=[(krot_b, LANES)], name="proj_kv")
        oa_p = _prompt_attention(q, keys, vals, batch, seq, 256)
        oa_s = _sample_attention(q, ckv_b, krot_b, cache_ckv, cache_kpe_t, wuk_t, wuv_h, l, n_p, dec_batch, dec_seq)
        oa = jnp.concatenate([oa_p, oa_s], axis=0)
        oh_p, st_p = _hgrn(hq, hk, logf, hv, hgate, hg_norm_g[l], None, layer=l,
                           row_block0=0, n_streams=batch, stream_len=seq, length=2 * CHUNK,
                           block_len=512, group=8)
        oh_s, st_s = _hgrn(hq, hk, logf, hv, hgate, hg_norm_g[l], state_hgrn, layer=l,
                           row_block0=n_p // dec_seq, n_streams=dec_batch, stream_len=dec_seq, length=dec_seq,
                           block_len=dec_seq, group=4)
        oh = jnp.concatenate([oh_p, oh_s], axis=0)
        mix = _merge(oa, oh, w_oa_b, w_ob_b, ga, gb, l, tm, 512)
        x1, h2 = _outproj(mix, w_out_b, x, post_mix_g[l], pre_mlp_g[l], l, 384)
        if l + 1 < depth:
            x = _mlp(h2, w_up_b, w_down_b, x1, post_mlp_g[l], l, tm, 512)
        else:
            y_p, y_s = _mlp(h2, w_up_b, w_down_b, x1, post_mlp_g[l], l, tm, 512, split_rows=n_p)

        ckv_out.append(ckv)
        kpe_out.append(krot)
        st_p_out.append(st_p)
        st_s_out.append(st_s)

    ckv_all = jnp.stack(ckv_out)
    kpe_all = jnp.stack(kpe_out)
    return (y_p.reshape(batch, seq, d),
            y_s.reshape(dec_batch, dec_seq, d),
            ckv_all[:, :n_p].reshape(depth, batch, seq, KV_LORA),
            kpe_all[:, :n_p].reshape(depth, batch, seq, QK_ROPE),
            jnp.stack(st_p_out),
            ckv_all[:, n_p:].reshape(depth, dec_batch, dec_seq, KV_LORA),
            kpe_all[:, n_p:].reshape(depth, dec_batch, dec_seq, QK_ROPE),
            jnp.stack(st_s_out))
```

```python
import functools

import jax
import jax.numpy as jnp
import numpy as np
from jax import lax
from jax.experimental import pallas as pl
from jax.experimental.pallas import tpu as pltpu

F32 = jnp.float32
BF16 = jnp.bfloat16

CHUNK = 64
MLA_HEADS = 16
QK_NOPE = 128
QK_ROPE = 64
QK_HEAD = QK_NOPE + QK_ROPE
V_HEAD = 128
Q_LORA = 512
KV_LORA = 512
ROPE_THETA = 10000.0
ATTN_SCALE = QK_HEAD ** -0.5
HG_HEADS = 16
HG_DK = 128
HG_DV = 128
EPS = 1e-6

LANES = 128
QK_PAD = 2 * LANES
NEG_BIG = -1e30
LOG2_E = 1.4426950408889634
VMEM_LIMIT = 56 * 1024 * 1024


def _params(*sem):
    return pltpu.CompilerParams(dimension_semantics=sem, vmem_limit_bytes=VMEM_LIMIT)


def _rms(x, g):
    return x * lax.rsqrt(jnp.mean(x * x, axis=-1, keepdims=True) + EPS) * g


def _dot(a, b):
    return jnp.dot(a, b, preferred_element_type=F32)


def _dot_nt(a, b):
    return lax.dot_general(a, b, (((1,), (1,)), ((), ())), preferred_element_type=F32)


def _dot_tn(a, b):
    return lax.dot_general(a, b, (((0,), (0,)), ((), ())), preferred_element_type=F32)


def _part_specs(parts, tm, index_map, tail_index_map):
    specs = [pl.BlockSpec((tm, parts[0].shape[1]), index_map)]
    if len(parts) == 2:
        specs.append(pl.BlockSpec(parts[1].shape, tail_index_map))
    return specs


def _stacked_rows(parts):
    return sum(p.shape[0] for p in parts)


def _on_row_tiles(operands, row_axis, tm, fn):
    if all(len(o) == 1 for o in operands):
        fn([o[0][...] for o in operands])
        return
    is_last = pl.program_id(row_axis) == pl.num_programs(row_axis) - 1

    @pl.when(jnp.logical_not(is_last))
    def _():
        fn([o[0][...] for o in operands])

    @pl.when(is_last)
    def _():
        fn([o[0][...] if len(o) == 1 else
            jnp.concatenate([o[0][:tm - o[1].shape[0]], o[1][...]], axis=0) for o in operands])


def _rmsnorm_body(*refs, tm):
    x_parts, (g_ref, o_ref) = refs[:-2], refs[-2:]

    def emit(tiles):
        o_ref[...] = _rms(tiles[0], g_ref[...]).astype(o_ref.dtype)

    _on_row_tiles([x_parts], 0, tm, emit)


def _rmsnorm(x_parts, g, tm):
    rows, d = _stacked_rows(x_parts), x_parts[0].shape[1]
    return pl.pallas_call(
        functools.partial(_rmsnorm_body, tm=tm),
        grid=(rows // tm,),
        in_specs=_part_specs(x_parts, tm, lambda i: (i, 0), lambda i: (0, 0))
        + [pl.BlockSpec((1, d), lambda i: (0, 0))],
        out_specs=pl.BlockSpec((tm, d), lambda i: (i, 0)),
        out_shape=jax.ShapeDtypeStruct((rows, d), BF16),
        compiler_params=_params("arbitrary"),
        name="rmsnorm",
    )(*x_parts, g.reshape(1, d))


def _mm_body(x_ref, w_ref, *rest, epi, n_extra):
    extra = rest[:n_extra]
    outs = rest[n_extra:-1]
    w_scr = rest[-1]

    @pl.when(pl.program_id(1) == 0)
    def _():
        w_scr[...] = w_ref[0].astype(w_scr.dtype)

    acc = _dot_nt(x_ref[...], w_scr[...])
    res = epi(acc, *[e[...] for e in extra])
    for o_ref, r in zip(outs, res):
        o_ref[...] = r.astype(o_ref.dtype)


def _mm(x, w_t, epi, outs, *, layer, row0, n, tm, tn, col_args=(), tile_args=(), name):
    rows, k = x.shape
    assert row0 % 8 == 0 and tn % 8 == 0
    grid = (n // tn, rows // tm)
    in_specs = [pl.BlockSpec((tm, k), lambda j, i: (i, 0)),
                pl.BlockSpec((pl.Element(1), pl.Element(tn), pl.Element(k)),
                             lambda j, i: (layer, pl.multiple_of(row0 + j * tn, 8), 0))]
    args = [x, w_t]
    for arr, width in col_args:
        in_specs.append(pl.BlockSpec((arr.shape[0], width), lambda j, i: (0, j)))
        args.append(arr)
    for arr, width, follows_n in tile_args:
        if follows_n:
            in_specs.append(pl.BlockSpec((tm, width), lambda j, i: (i, j)))
        else:
            in_specs.append(pl.BlockSpec((tm, width), lambda j, i: (i, 0)))
        args.append(arr)
    out_specs = [pl.BlockSpec((tm, width), lambda j, i: (i, j)) for width, _ in outs]
    out_shape = [jax.ShapeDtypeStruct((rows, width * (n // tn)), dt) for width, dt in outs]
    res = pl.pallas_call(
        functools.partial(_mm_body, epi=epi, n_extra=len(col_args) + len(tile_args)),
        grid=grid,
        in_specs=in_specs,
        out_specs=out_specs,
        out_shape=out_shape,
        scratch_shapes=[pltpu.VMEM((tn, k), BF16)],
        compiler_params=_params("parallel", "arbitrary"),
        name=name,
    )(*args)
    return res


def _mm_rows_body(x_ref, w_ref, *rest, epi, n_extra, tn, widths):
    extra = [e[...] for e in rest[:n_extra]]
    outs = rest[n_extra:]
    x = x_ref[...]
    for p in range(w_ref.shape[1] // tn):
        res = epi(_dot(x, w_ref[:, p * tn:(p + 1) * tn]), *extra)
        for o_ref, r, width in zip(outs, res, widths):
            o_ref[:, p * width:(p + 1) * width] = r.astype(o_ref.dtype)


def _mm_rows(x, w, epi, outs, *, layer, tm, tn, rows=None, tile_args=(), name):
    k = x.shape[1]
    rows = x.shape[0] if rows is None else rows
    n = w.shape[2]
    in_specs = [pl.BlockSpec((tm, k), lambda i: (i, 0)),
                pl.BlockSpec((None, k, n), lambda i: (layer, 0, 0))]
    in_specs += [pl.BlockSpec((tm, width), lambda i: (i, 0)) for _, width in tile_args]
    return pl.pallas_call(
        functools.partial(_mm_rows_body, epi=epi, n_extra=len(tile_args), tn=tn,
                          widths=[width for width, _ in outs]),
        grid=(rows // tm,),
        in_specs=in_specs,
        out_specs=[pl.BlockSpec((tm, width * (n // tn)), lambda i: (i, 0)) for width, _ in outs],
        out_shape=[jax.ShapeDtypeStruct((rows, width * (n // tn)), dt) for width, dt in outs],
        compiler_params=_params("parallel"),
        name=name,
    )(x, w, *[arr for arr, _ in tile_args])


def _silu(z):
    return z * jax.nn.sigmoid(z)


def _epi_silu(acc):
    return (_silu(acc),)


def _epi_sigmoid(acc):
    return (jax.nn.sigmoid(acc),)


def _epi_id(acc):
    return (acc,)


def _epi_relu2(acc):
    r = jnp.maximum(acc, 0.0)
    return (r * r,)


def _epi_forget(acc, lb_logits, *, layer):
    mx = jnp.max(lb_logits, axis=0, keepdims=True)
    e = jnp.exp(lb_logits - mx)
    sm = e / jnp.sum(e, axis=0, keepdims=True)
    lb = jnp.zeros_like(mx)
    for i in range(1, layer + 1):
        lb = lb + sm[i:i + 1]
    z = acc
    e = jnp.exp(-jnp.abs(z))
    r = 1.0 / (1.0 + e)
    sig_neg = jnp.where(z >= 0.0, e * r, r)
    log_sig = jnp.minimum(z, 0.0) - jnp.log(1.0 + e)
    a = jnp.log(lb)
    c = jnp.log1p(-lb) + log_sig
    hi = jnp.maximum(a, c)
    lo = jnp.minimum(a, c)
    logf = hi + jnp.log(1.0 + jnp.exp(lo - hi))
    kk = (1.0 - lb) * sig_neg
    return kk, logf


def _epi_q(acc, cos_t, sin_t):
    n0 = acc[:, 0 * LANES:1 * LANES] * ATTN_SCALE
    p0 = acc[:, 1 * LANES:2 * LANES] * cos_t + acc[:, 4 * LANES:5 * LANES] * sin_t
    n1 = acc[:, 2 * LANES:3 * LANES] * ATTN_SCALE
    p1 = acc[:, 3 * LANES:4 * LANES] * cos_t + acc[:, 5 * LANES:6 * LANES] * sin_t
    return (jnp.concatenate([n0, p0, n1, p1], axis=1),)


def _epi_kv(acc, krot):
    krot = krot.astype(F32)
    keys = jnp.concatenate([acc[:, 0:LANES], krot, acc[:, LANES:2 * LANES], krot], axis=1)
    return keys, acc[:, 2 * LANES:4 * LANES]


def _latent_body(h_ref, w_ref, gq_ref, gkv_ref, cos_ref, sin_ref,
                 qn_ref, ckv_ref, ckvb_ref, krot_ref, krotb_ref, w_scr):
    base = Q_LORA + KV_LORA
    half = QK_ROPE // 2

    @pl.when(pl.program_id(0) == 0)
    def _():
        w_scr[...] = jnp.zeros_like(w_scr)
        w_scr[0:base + QK_ROPE, :] = w_ref[...].astype(w_scr.dtype)
        w_scr[base + LANES:base + LANES + half, :] = w_ref[base + half:base + QK_ROPE, :].astype(w_scr.dtype)
        w_scr[base + LANES + half:base + LANES + QK_ROPE, :] = w_ref[base:base + half, :].astype(w_scr.dtype)

    acc = _dot_nt(h_ref[...], w_scr[...])
    qn_ref[...] = _rms(acc[:, :Q_LORA], gq_ref[...]).astype(qn_ref.dtype)
    ckv = _rms(acc[:, Q_LORA:base], gkv_ref[...])
    ckv_ref[...] = ckv
    ckvb_ref[...] = ckv.astype(ckvb_ref.dtype)
    kr = acc[:, base:base + LANES] * cos_ref[...] + acc[:, base + LANES:base + 2 * LANES] * sin_ref[...]
    krot_ref[...] = kr[:, :QK_ROPE]
    krotb_ref[...] = kr.astype(krotb_ref.dtype)


def _latent(h, w_t, gq, gkv, cos_t, sin_t, layer, tm):
    rows, d = h.shape
    n_in = Q_LORA + KV_LORA + QK_ROPE
    row = lambda width: pl.BlockSpec((tm, width), lambda i: (i, 0))
    const = lambda r, width: pl.BlockSpec((r, width), lambda i: (0, 0))
    return pl.pallas_call(
        _latent_body,
        grid=(rows // tm,),
        in_specs=[row(d), pl.BlockSpec((None, n_in, d), lambda i: (layer, 0, 0)),
                  const(1, Q_LORA), const(1, KV_LORA), row(LANES), row(LANES)],
        out_specs=[row(Q_LORA), row(KV_LORA), row(KV_LORA), row(QK_ROPE), row(LANES)],
        out_shape=[jax.ShapeDtypeStruct((rows, Q_LORA), BF16),
                   jax.ShapeDtypeStruct((rows, KV_LORA), F32),
                   jax.ShapeDtypeStruct((rows, KV_LORA), BF16),
                   jax.ShapeDtypeStruct((rows, QK_ROPE), F32),
                   jax.ShapeDtypeStruct((rows, LANES), BF16)],
        scratch_shapes=[pltpu.VMEM((Q_LORA + KV_LORA + 2 * LANES, d), BF16)],
        compiler_params=_params("arbitrary"),
        name="latent_proj",
    )(h, w_t, gq.reshape(1, -1), gkv.reshape(1, -1), cos_t, sin_t)


def _attn_body(q_ref, k_ref, v_ref, o_ref, *, tq, nq):
    r_chunk = lax.broadcasted_iota(jnp.int32, (tq, tq), 0) // CHUNK
    c_chunk = lax.broadcasted_iota(jnp.int32, (tq, tq), 1) // CHUNK
    visible = c_chunk <= r_chunk

    for n_past in range(nq):
        lo = n_past * tq
        q = q_ref[lo:lo + tq, :]
        s_d = jnp.where(visible, _dot_nt(q, k_ref[lo:lo + tq, :]), NEG_BIG)
        m = jnp.max(s_d, axis=-1, keepdims=True)
        if n_past:
            s_p = _dot_nt(q, k_ref[0:lo, :])
            m = jnp.maximum(m, jnp.max(s_p, axis=-1, keepdims=True))
        p_d = jnp.exp(s_d - m)
        l = jnp.sum(p_d, axis=-1, keepdims=True)
        acc = _dot(p_d.astype(BF16), v_ref[lo:lo + tq, :])
        if n_past:
            p_p = jnp.exp(s_p - m)
            l = l + jnp.sum(p_p, axis=-1, keepdims=True)
            acc = acc + _dot(p_p.astype(BF16), v_ref[0:lo, :])
        o_ref[lo:lo + tq, :] = (acc / l).astype(o_ref.dtype)


def _prompt_attention(q, k, v, batch, seq, tq):
    return pl.pallas_call(
        functools.partial(_attn_body, tq=tq, nq=seq // tq),
        grid=(batch, MLA_HEADS),
        in_specs=[pl.BlockSpec((seq, QK_PAD), lambda b, h: (b, h)),
                  pl.BlockSpec((seq, QK_PAD), lambda b, h: (b, h)),
                  pl.BlockSpec((seq, V_HEAD), lambda b, h: (b, h))],
        out_specs=pl.BlockSpec((seq, V_HEAD), lambda b, h: (b, h)),
        out_shape=jax.ShapeDtypeStruct((batch * seq, MLA_HEADS * V_HEAD), BF16),
        compiler_params=_params("parallel", "parallel"),
        name="prompt_attention",
    )(q, k, v)


def _sattn_body(q_ref, cn_ref, kn_ref, cc_ref, ck_ref, wuk_ref, wuv_ref, o_ref, qa_scr, qp_scr,
                *, dec_seq, key_tile):
    for h in range(MLA_HEADS):
        qn = q_ref[:, h * QK_PAD:h * QK_PAD + QK_NOPE]
        qa_scr[h * dec_seq:(h + 1) * dec_seq, :] = _dot(qn, wuk_ref[h]).astype(BF16)
        qp_scr[h * dec_seq:(h + 1) * dec_seq, :] = q_ref[:, h * QK_PAD + QK_NOPE:(h + 1) * QK_PAD]
    qa = qa_scr[...]
    qp = qp_scr[...]
    rows = MLA_HEADS * dec_seq

    def update(carry, s, c):
        m, l, acc = carry
        m_new = jnp.maximum(m, jnp.max(s, axis=-1, keepdims=True))
        alpha = jnp.exp(m - m_new)
        p = jnp.exp(s - m_new)
        l = alpha * l + jnp.sum(p, axis=-1, keepdims=True)
        acc = alpha * acc + _dot(p.astype(BF16), c)
        return m_new, l, acc

    carry = (jnp.full((rows, 1), NEG_BIG, F32), jnp.zeros((rows, 1), F32), jnp.zeros((rows, KV_LORA), F32))
    past = cc_ref.shape[0]
    for t in range(past // key_tile):
        c = cc_ref[t * key_tile:(t + 1) * key_tile, :].astype(BF16)
        kp_t = ck_ref[:, t * key_tile:(t + 1) * key_tile].astype(BF16)
        s = _dot_nt(qa, c) + _dot(qp[:, :QK_ROPE], kp_t)
        carry = update(carry, s, c)
    cn = cn_ref[...]
    s = _dot_nt(qa, cn) + _dot_nt(qp, kn_ref[...])
    _, l, acc = update(carry, s, cn)
    o_lat = (acc / l).astype(BF16)
    for h in range(MLA_HEADS):
        o_ref[:, h * V_HEAD:(h + 1) * V_HEAD] = _dot(
            o_lat[h * dec_seq:(h + 1) * dec_seq, :], wuv_ref[h]).astype(o_ref.dtype)


def _sample_attention(q, ckv_b, krot_b, cache_c, cache_k, wuk_t, wuv_h, layer, n_prompt_rows, dec_batch, dec_seq):
    rb = n_prompt_rows // dec_seq
    past = cache_c.shape[2]
    key_tile = min(past, 1024)
    per_layer = lambda s: pl.BlockSpec((None,) + s, lambda b: (layer, 0, 0, 0))
    return pl.pallas_call(
        functools.partial(_sattn_body, dec_seq=dec_seq, key_tile=key_tile),
        grid=(dec_batch,),
        in_specs=[pl.BlockSpec((dec_seq, MLA_HEADS * QK_PAD), lambda b: (rb + b, 0)),
                  pl.BlockSpec((dec_seq, KV_LORA), lambda b: (rb + b, 0)),
                  pl.BlockSpec((dec_seq, LANES), lambda b: (rb + b, 0)),
                  pl.BlockSpec((None, None, past, KV_LORA), lambda b: (layer, b, 0, 0)),
                  pl.BlockSpec((None, None, QK_ROPE, past), lambda b: (layer, b, 0, 0)),
                  per_layer((MLA_HEADS, QK_NOPE, KV_LORA)),
                  per_layer((MLA_HEADS, KV_LORA, V_HEAD))],
        out_specs=pl.BlockSpec((dec_seq, MLA_HEADS * V_HEAD), lambda b: (b, 0)),
        out_shape=jax.ShapeDtypeStruct((dec_batch * dec_seq, MLA_HEADS * V_HEAD), BF16),
        scratch_shapes=[pltpu.VMEM((MLA_HEADS * dec_seq, KV_LORA), BF16),
                        pltpu.VMEM((MLA_HEADS * dec_seq, LANES), BF16)],
        compiler_params=_params("parallel"),
        name="sample_attention",
    )(q, ckv_b, krot_b, cache_c, cache_k, wuk_t, wuv_h)


def _hgrn_tables(length):
    t = np.arange(length)[:, None]
    r = np.arange(length)[None, :]
    groups = [(r <= t), (r > t)]
    masks = [(r == t)]
    m = length // 2
    while m >= 1:
        blk = t // (2 * m)
        start2 = blk * 2 * m + m
        second = (t % (2 * m)) >= m
        incl = second & (r >= start2) & (r <= t)
        excl = (~second) & (r > t) & (r < start2)
        groups.append(incl | excl)
        r_blk = r // (2 * m)
        r_first = (r % (2 * m)) < m
        masks.append(second & r_first & (r_blk == blk))
        m //= 2
    return (np.concatenate(groups, axis=0).astype(np.float32),
            np.stack(masks, axis=0).astype(np.float32))


def _hgrn_body(*refs, length, n_chunks, group, has_state):
    if has_state:
        q_ref, k_ref, lf_ref, v_ref, g_ref, gn_ref, sum_ref, mask_ref, s0_ref, o_ref, sout_ref, st_scr = refs
    else:
        q_ref, k_ref, lf_ref, v_ref, g_ref, gn_ref, sum_ref, mask_ref, o_ref, sout_ref, st_scr = refs
    tb = pl.program_id(2)

    @pl.when(tb == 0)
    def _():
        for g in range(group):
            st_scr[g] = s0_ref[g].T if has_state else jnp.zeros((HG_DV, HG_DK), F32)

    n_lev = mask_ref.shape[0] - 1
    summat = sum_ref[...]
    gn = gn_ref[...]

    pack_levels = length % LANES == 0 and (n_lev + 1) % 2 == 0
    zeros_k = jnp.zeros((length, HG_DK), BF16)

    def mixed_rows(q, k, half):
        parts = []
        for r in range(0, length, half):
            parts.append((q if (r // half) % 2 else k)[r:r + half])
        return jnp.concatenate(parts, axis=0)

    def level_operands(rows, g, e_all):
        cols = slice(g * HG_DK, (g + 1) * HG_DK)
        qs = [q_ref[rows, cols]]
        ks = [k_ref[rows, cols]]
        q = qs[0].astype(F32)
        k = ks[0].astype(F32)
        half = length // 2
        for lev in range(n_lev):
            e = e_all[(2 + lev) * length:(3 + lev) * length]
            if half % 8 == 0:
                u = (mixed_rows(q, k, half) * e).astype(BF16)
                qs.append(u)
                ks.append(u)
            else:
                qs.append((q * e).astype(BF16))
                ks.append((k * e).astype(BF16))
            half //= 2
        q_in = (q * e_all[0:length]).astype(BF16)
        k_out = (k * e_all[length:2 * length]).astype(BF16)
        return qs, ks, q_in, k_out

    def level_scores(qs, ks):
        if pack_levels:
            out = []
            for p in range(0, n_lev + 1, 2):
                lhs = jnp.concatenate([qs[p], qs[p + 1]], axis=1)
                rhs = jnp.concatenate([jnp.concatenate([ks[p], zeros_k], axis=1),
                                       jnp.concatenate([zeros_k, ks[p + 1]], axis=1)], axis=0)
                a = _dot_nt(lhs, rhs)
                out += [a[:, :length], a[:, length:]]
            return out
        return [_dot_nt(qs[p], ks[p]) for p in range(n_lev + 1)]

    def chunk(c, _):
        rows = pl.ds(pl.multiple_of(c * length, length), length)
        e_heads = []
        for pair in range(group // 2):
            cols2 = slice(2 * pair * HG_DK, (2 * pair + 2) * HG_DK)
            lf2 = lf_ref[rows, cols2] * LOG2_E
            hi = lf2.astype(BF16)
            mid = (lf2 - hi.astype(F32)).astype(BF16)
            e_pair = jnp.exp2(_dot(summat, jnp.concatenate([hi, mid], axis=0)))
            e_heads += [e_pair[:, :HG_DK], e_pair[:, HG_DK:]]
        work = []
        for g in range(group):
            qs, ks, q_in, k_out = level_operands(rows, g, e_heads[g])
            st = st_scr[g]
            work.append((level_scores(qs, ks), _dot_nt(q_in, st.astype(BF16)), k_out, st))
        for g in range(group):
            cols = slice(g * HG_DK, (g + 1) * HG_DK)
            scores, o_in, k_out, st = work[g]
            v = v_ref[rows, cols]
            att = mask_ref[0] * scores[0]
            for p in range(1, n_lev + 1):
                att = att + mask_ref[p] * scores[p]
            o = o_in + _dot(att.astype(BF16), v)
            e_last = e_heads[g][length - 1:length, :]
            st_scr[g] = st * e_last + _dot_tn(v, k_out)
            o_ref[rows, cols] = (_rms(o, gn) * g_ref[rows, cols]).astype(o_ref.dtype)
        return 0

    lax.fori_loop(0, n_chunks, chunk, 0)

    @pl.when(tb == pl.num_programs(2) - 1)
    def _():
        for g in range(group):
            sout_ref[g] = st_scr[g].T


def _hgrn(q, k, logf, v, gate, gn, state0, *, layer, row_block0, n_streams, stream_len, length, block_len, group):
    n_tb = stream_len // block_len
    n_chunks = block_len // length
    width = group * HG_DK
    summat, masks = _hgrn_tables(length)
    summat = jnp.asarray(np.concatenate([summat, summat], axis=1), BF16)
    masks = jnp.asarray(masks, F32)
    tok = lambda: pl.BlockSpec((block_len, width), lambda n, h, t: (row_block0 + n * n_tb + t, h))
    st_spec = pl.BlockSpec((None, group, HG_DK, HG_DV), lambda n, h, t: (n, h, 0, 0))
    in_specs = [tok(), tok(), tok(), tok(), tok(), pl.BlockSpec((1, HG_DV), lambda n, h, t: (0, 0)),
                pl.BlockSpec(summat.shape, lambda n, h, t: (0, 0)),
                pl.BlockSpec(masks.shape, lambda n, h, t: (0, 0, 0))]
    args = [q, k, logf, v, gate, gn.reshape(1, HG_DV), summat, masks]
    if state0 is not None:
        in_specs.append(pl.BlockSpec((None, None, group, HG_DK, HG_DV), lambda n, h, t: (layer, n, h, 0, 0)))
        args.append(state0)
    return pl.pallas_call(
        functools.partial(_hgrn_body, length=length, n_chunks=n_chunks, group=group,
                          has_state=state0 is not None),
        grid=(n_streams, HG_HEADS // group, n_tb),
        in_specs=in_specs,
        out_specs=[pl.BlockSpec((block_len, width), lambda n, h, t: (n * n_tb + t, h)), st_spec],
        out_shape=[jax.ShapeDtypeStruct((n_streams * stream_len, HG_HEADS * HG_DV), BF16),
                   jax.ShapeDtypeStruct((n_streams, HG_HEADS, HG_DK, HG_DV), F32)],
        scratch_shapes=[pltpu.VMEM((group, HG_DV, HG_DK), F32)],
        compiler_params=_params("parallel", "parallel", "arbitrary"),
        name="hgrn_state" if state0 is not None else "hgrn_prompt",
    )(*args)


def _merge_body(*refs, tm, n_a):
    oa_parts, oh_parts = refs[:n_a], refs[n_a:-5]
    woa_ref, wob_ref, ga_ref, gb_ref, o_ref = refs[-5:]

    def emit(tiles):
        a = _dot(tiles[0], woa_ref[...])
        b = _dot(tiles[1], wob_ref[...])
        o_ref[...] = (ga_ref[...] * a + gb_ref[...] * b).astype(o_ref.dtype)

    _on_row_tiles([oa_parts, oh_parts], 1, tm, emit)


def _merge(oa_parts, oh_parts, woa, wob, ga, gb, layer, tm, tn):
    rows, k = _stacked_rows(oa_parts), oa_parts[0].shape[1]
    n = woa.shape[2]
    x_specs = lambda parts: _part_specs(parts, tm, lambda j, i: (i, 0), lambda j, i: (0, 0))
    w_spec = pl.BlockSpec((None, k, tn), lambda j, i: (layer, 0, j))
    t_spec = pl.BlockSpec((tm, tn), lambda j, i: (i, j))
    return pl.pallas_call(
        functools.partial(_merge_body, tm=tm, n_a=len(oa_parts)),
        grid=(n // tn, rows // tm),
        in_specs=x_specs(oa_parts) + x_specs(oh_parts) + [w_spec, w_spec, t_spec, t_spec],
        out_specs=t_spec,
        out_shape=jax.ShapeDtypeStruct((rows, n), BF16),
        compiler_params=_params("parallel", "arbitrary"),
        name="gated_merge",
    )(*oa_parts, *oh_parts, woa, wob, ga, gb)


def _outproj_body(m_ref, w_ref, *refs, tm):
    x_parts, (g1_ref, g2_ref, x1_ref, h2_ref) = refs[:-4], refs[-4:]

    def emit(tiles):
        y = _dot(m_ref[...], w_ref[...])
        x1 = tiles[0] + _rms(y, g1_ref[...])
        x1_ref[...] = x1
        h2_ref[...] = _rms(x1, g2_ref[...]).astype(h2_ref.dtype)

    _on_row_tiles([x_parts], 0, tm, emit)


def _outproj(mix, w, x_parts, g_post, g_pre2, layer, tm):
    rows, d = mix.shape
    row = pl.BlockSpec((tm, d), lambda i: (i, 0))
    vec = pl.BlockSpec((1, d), lambda i: (0, 0))
    return pl.pallas_call(
        functools.partial(_outproj_body, tm=tm),
        grid=(rows // tm,),
        in_specs=[row, pl.BlockSpec((None, d, d), lambda i: (layer, 0, 0))]
        + _part_specs(x_parts, tm, lambda i: (i, 0), lambda i: (0, 0)) + [vec, vec],
        out_specs=[row, row],
        out_shape=[jax.ShapeDtypeStruct((rows, d), F32), jax.ShapeDtypeStruct((rows, d), BF16)],
        compiler_params=_params("arbitrary"),
        name="out_proj",
    )(mix, w, *x_parts, g_post.reshape(1, d), g_pre2.reshape(1, d))


def _mlp_body(h_ref, wu_ref, wd_ref, x_ref, g_ref, *rest, tail_start):
    o_ref, acc_ref = rest[0], rest[-1]
    f = pl.program_id(1)

    @pl.when(f == 0)
    def _():
        acc_ref[...] = jnp.zeros_like(acc_ref)

    u = jnp.maximum(_dot(h_ref[...], wu_ref[...]), 0.0)
    acc_ref[...] += _dot((u * u).astype(BF16), wd_ref[...])

    @pl.when(f == pl.num_programs(1) - 1)
    def _():
        y = x_ref[...] + _rms(acc_ref[...], g_ref[...])
        o_ref[...] = y
        if tail_start is not None:
            @pl.when(pl.program_id(0) == pl.num_programs(0) - 1)
            def _():
                rest[1][...] = y[tail_start:, :]


def _mlp(h2, w_up, w_down, x1, g, layer, tm, tf, split_rows=None):
    rows, d = x1.shape
    ff = w_up.shape[2]
    row = pl.BlockSpec((tm, d), lambda i, f: (i, 0))
    if split_rows is None:
        out_specs, out_shape, tail_start = row, jax.ShapeDtypeStruct((rows, d), F32), None
    else:
        tail = rows - split_rows
        tail_start = tm - tail
        assert 0 <= tail_start and tail % 8 == 0
        out_specs = [row, pl.BlockSpec((tail, d), lambda i, f: (0, 0))]
        out_shape = [jax.ShapeDtypeStruct((split_rows, d), F32), jax.ShapeDtypeStruct((tail, d), F32)]
    return pl.pallas_call(
        functools.partial(_mlp_body, tail_start=tail_start),
        grid=(rows // tm, ff // tf),
        in_specs=[row, pl.BlockSpec((None, d, tf), lambda i, f: (layer, 0, f)),
                  pl.BlockSpec((None, tf, d), lambda i, f: (layer, f, 0)), row,
                  pl.BlockSpec((1, d), lambda i, f: (0, 0))],
        out_specs=out_specs,
        out_shape=out_shape,
        scratch_shapes=[pltpu.VMEM((tm, d), F32)],
        compiler_params=_params("arbitrary", "arbitrary"),
        name="mlp",
    )(h2, w_up, w_down, x1, g.reshape(1, d))


def _pad_lanes(w):
    return jnp.concatenate([w, jnp.zeros(w.shape[:-1] + (LANES - w.shape[-1],), w.dtype)], axis=-1)


def _swap_halves(w):
    half = w.shape[-1] // 2
    return jnp.concatenate([w[..., half:], w[..., :half]], axis=-1)


def _prep_weights(w_uq, w_uk, w_uv):
    depth = w_uq.shape[0]
    uq = w_uq.reshape(depth, Q_LORA, MLA_HEADS // 2, 2, QK_HEAD)
    nope = uq[..., :QK_NOPE]
    pe = _pad_lanes(uq[..., QK_NOPE:])
    pe_sw = _pad_lanes(_swap_halves(uq[..., QK_NOPE:]))
    w_q = jnp.concatenate([nope[:, :, :, 0], pe[:, :, :, 0], nope[:, :, :, 1], pe[:, :, :, 1],
                           pe_sw[:, :, :, 0], pe_sw[:, :, :, 1]], axis=-1)
    w_q = w_q.reshape(depth, Q_LORA, -1).astype(BF16)
    uk = w_uk.reshape(depth, KV_LORA, MLA_HEADS // 2, 2 * QK_NOPE)
    uv = w_uv.reshape(depth, KV_LORA, MLA_HEADS // 2, 2 * V_HEAD)
    w_kv = jnp.concatenate([uk, uv], axis=-1).reshape(depth, KV_LORA, -1).astype(BF16)
    wuk_t = jnp.transpose(w_uk, (0, 2, 3, 1)).astype(BF16)
    wuv_h = jnp.transpose(w_uv, (0, 2, 1, 3)).astype(BF16)
    return w_q, w_kv, wuk_t, wuv_h


def _rope_tables(positions):
    half = QK_ROPE // 2
    inv = ROPE_THETA ** (-jnp.arange(half, dtype=F32) / half)
    ang = positions.astype(F32)[:, None] * inv[None, :]
    cos, sin = jnp.cos(ang), jnp.sin(ang)
    zeros = jnp.zeros((positions.shape[0], LANES - QK_ROPE), F32)
    return (jnp.concatenate([cos, cos, zeros], axis=1), jnp.concatenate([-sin, sin, zeros], axis=1))


def kernel(x_prompt, x_sample, cache_ckv, cache_kpe, state_hgrn, pre_mix_g, w_in, q_norm_g, w_uq, kv_norm_g,
           w_uk, w_uv, w_oa, hg_lb, hg_norm_g, w_ob, w_out, post_mix_g, pre_mlp_g, w_up, w_down, post_mlp_g):
    batch, seq, d = x_prompt.shape
    dec_batch, dec_seq, _ = x_sample.shape
    depth = w_in.shape[0]
    past = cache_ckv.shape[2]
    n_p = batch * seq
    n_s = dec_batch * dec_seq
    rows = n_p + n_s
    tm = 768
    tm_wide = 1408
    assert rows % tm == 0 and rows % tm_wide == 0 and n_p % 1024 == 0

    x_parts = (x_prompt.reshape(n_p, d), x_sample.reshape(n_s, d))
    pos = jnp.concatenate([jnp.tile(jnp.arange(seq, dtype=jnp.int32), batch),
                           jnp.tile(past + jnp.arange(dec_seq, dtype=jnp.int32), dec_batch)])
    cos_t, sin_t = _rope_tables(pos)
    cos_q, sin_q = cos_t * ATTN_SCALE, sin_t * ATTN_SCALE

    w_in_t = jnp.transpose(w_in, (0, 2, 1))
    cache_kpe_t = jnp.transpose(cache_kpe, (0, 1, 3, 2))
    wide0 = Q_LORA + KV_LORA + QK_ROPE
    w_q, w_kv, wuk_t, wuv_h = _prep_weights(w_uq, w_uk, w_uv)
    w_oa_b, w_ob_b, w_out_b = w_oa.astype(BF16), w_ob.astype(BF16), w_out.astype(BF16)
    w_up_b, w_down_b = w_up.astype(BF16), w_down.astype(BF16)

    ckv_out, kpe_out, st_p_out, st_s_out = [], [], [], []
    for l in range(depth):
        h = _rmsnorm(x_parts, pre_mix_g[l], tm)
        qn, ckv, ckv_b, krot, krot_b = _latent(h, w_in_t, q_norm_g[l], kv_norm_g[l], cos_t, sin_t, l, tm)
        wide = lambda i, *a, tm=tm_wide, **kw: _mm(h, w_in_t, *a, layer=l, row0=wide0 + i * d, n=d, tm=tm,
                                                   tn=1024, **kw)
        (hq,) = wide(0, _epi_silu, [(1024, BF16)], name="proj_hq")
        hk, logf = wide(1, functools.partial(_epi_forget, layer=l), [(1024, BF16), (1024, F32)],
                        col_args=[(hg_lb, 1024)], tm=tm, name="proj_hf")
        (hv,) = wide(2, _epi_id, [(1024, BF16)], name="proj_hi")
        (hgate,) = wide(3, _epi_silu, [(1024, BF16)], name="proj_hg")
        (ga,) = wide(4, _epi_sigmoid, [(1024, BF16)], name="proj_ga")
        (gb,) = wide(5, _epi_sigmoid, [(1024, BF16)], name="proj_gb")
        (q,) = _mm_rows(qn, w_q, _epi_q, [(2 * QK_PAD, BF16)], layer=l, tm=tm, tn=6 * LANES,
                        tile_args=[(cos_q, LANES), (sin_q, LANES)], name="proj_q")
        keys, vals = _mm_rows(ckv_b, w_kv, _epi_kv, [(2 * QK_PAD, BF16), (2 * V_HEAD, BF16)], layer=l, tm=1024,
                              tn=4 * LANES, rows=n_p, tile_args=[(krot_b, LANES)], name="proj_kv")
        oa_p = _prompt_attention(q, keys, vals, batch, seq, 256)
        oa_s = _sample_attention(q, ckv_b, krot_b, cache_ckv, cache_kpe_t, wuk_t, wuv_h, l, n_p, dec_batch, dec_seq)
        oh_p, st_p = _hgrn(hq, hk, logf, hv, hgate, hg_norm_g[l], None, layer=l,
                           row_block0=0, n_streams=batch, stream_len=seq, length=2 * CHUNK,
                           block_len=512, group=8)
        oh_s, st_s = _hgrn(hq, hk, logf, hv, hgate, hg_norm_g[l], state_hgrn, layer=l,
                           row_block0=n_p // dec_seq, n_streams=dec_batch, stream_len=dec_seq, length=dec_seq,
                           block_len=dec_seq, group=4)
        mix = _merge((oa_p, oa_s), (oh_p, oh_s), w_oa_b, w_ob_b, ga, gb, l, tm, 1024)
        x1, h2 = _outproj(mix, w_out_b, x_parts, post_mix_g[l], pre_mlp_g[l], l, 384)
        if l + 1 < depth:
            x_parts = (_mlp(h2, w_up_b, w_down_b, x1, post_mlp_g[l], l, tm, 512),)
        else:
            y_p, y_s = _mlp(h2, w_up_b, w_down_b, x1, post_mlp_g[l], l, tm, 512, split_rows=n_p)

        ckv_out.append(ckv)
        kpe_out.append(krot)
        st_p_out.append(st_p)
        st_s_out.append(st_s)

    ckv_all = jnp.stack(ckv_out)
    kpe_all = jnp.stack(kpe_out)
    return (y_p.reshape(batch, seq, d),
            y_s.reshape(dec_batch, dec_seq, d),
            ckv_all[:, :n_p].reshape(depth, batch, seq, KV_LORA),
            kpe_all[:, :n_p].reshape(depth, batch, seq, QK_ROPE),
            jnp.stack(st_p_out),
            ckv_all[:, n_p:].reshape(depth, dec_batch, dec_seq, KV_LORA),
            kpe_all[:, n_p:].reshape(depth, dec_batch, dec_seq, QK_ROPE),
            jnp.stack(st_s_out))
```

```python
import functools

import jax
import jax.numpy as jnp
import numpy as np
from jax import lax
from jax.experimental import pallas as pl
from jax.experimental.pallas import tpu as pltpu

F32 = jnp.float32
BF16 = jnp.bfloat16

CHUNK = 64
MLA_HEADS = 16
QK_NOPE = 128
QK_ROPE = 64
QK_HEAD = QK_NOPE + QK_ROPE
V_HEAD = 128
Q_LORA = 512
KV_LORA = 512
ROPE_THETA = 10000.0
ATTN_SCALE = QK_HEAD ** -0.5
HG_HEADS = 16
HG_DK = 128
HG_DV = 128
EPS = 1e-6

LANES = 128
QK_PAD = 2 * LANES
NEG_BIG = -1e30
LOG2_E = 1.4426950408889634
Q_SCALE = ATTN_SCALE * LOG2_E
VMEM_LIMIT = 58 * 1024 * 1024


def _params(*sem):
    return pltpu.CompilerParams(dimension_semantics=sem, vmem_limit_bytes=VMEM_LIMIT)


def _rms(x, g):
    return x * lax.rsqrt(jnp.mean(x * x, axis=-1, keepdims=True) + EPS) * g


def _dot(a, b):
    return jnp.dot(a, b, preferred_element_type=F32)


def _dot_nt(a, b):
    return lax.dot_general(a, b, (((1,), (1,)), ((), ())), preferred_element_type=F32)


def _dot_tn(a, b):
    return lax.dot_general(a, b, (((0,), (0,)), ((), ())), preferred_element_type=F32)


def _part_specs(parts, tm, index_map, tail_index_map):
    specs = [pl.BlockSpec((tm, parts[0].shape[1]), index_map)]
    if len(parts) == 2:
        specs.append(pl.BlockSpec(parts[1].shape, tail_index_map))
    return specs


def _stacked_rows(parts):
    return sum(p.shape[0] for p in parts)


def _on_row_tiles(operands, row_axis, tm, fn):
    if all(len(o) == 1 for o in operands):
        fn([o[0][...] for o in operands])
        return
    is_last = pl.program_id(row_axis) == pl.num_programs(row_axis) - 1

    @pl.when(jnp.logical_not(is_last))
    def _():
        fn([o[0][...] for o in operands])

    @pl.when(is_last)
    def _():
        fn([o[0][...] if len(o) == 1 else
            jnp.concatenate([o[0][:tm - o[1].shape[0]], o[1][...]], axis=0) for o in operands])


def _rmsnorm_body(*refs, tm):
    x_parts, (g_ref, o_ref) = refs[:-2], refs[-2:]

    def emit(tiles):
        o_ref[...] = _rms(tiles[0], g_ref[...]).astype(o_ref.dtype)

    _on_row_tiles([x_parts], 0, tm, emit)


def _rmsnorm(x_parts, g, tm):
    rows, d = _stacked_rows(x_parts), x_parts[0].shape[1]
    return pl.pallas_call(
        functools.partial(_rmsnorm_body, tm=tm),
        grid=(rows // tm,),
        in_specs=_part_specs(x_parts, tm, lambda i: (i, 0), lambda i: (0, 0))
        + [pl.BlockSpec((1, d), lambda i: (0, 0))],
        out_specs=pl.BlockSpec((tm, d), lambda i: (i, 0)),
        out_shape=jax.ShapeDtypeStruct((rows, d), BF16),
        compiler_params=_params("arbitrary"),
        name="rmsnorm",
    )(*x_parts, g.reshape(1, d))


def _mm_body(x_ref, w_ref, *rest, epis, tiles_per_group, n_extra):
    extra = rest[:n_extra]
    outs = rest[n_extra:-1]
    w_scr = rest[-1]

    @pl.when(pl.program_id(1) == 0)
    def _():
        w_scr[...] = w_ref[0].astype(w_scr.dtype)

    acc = _dot_nt(x_ref[...], w_scr[...])

    def emit(epi):
        res = epi(acc, *[e[...] for e in extra])
        for o_ref, r in zip(outs, res):
            o_ref[...] = r.astype(o_ref.dtype)

    if len(epis) == 1:
        emit(epis[0])
    else:
        group = pl.program_id(0) // tiles_per_group
        for g, epi in enumerate(epis):
            pl.when(group == g)(functools.partial(emit, epi))


def _mm(x, w_t, epis, outs, *, layer, row0s, n, tm, tn, col_args=(), tile_args=(), name):
    rows, k = x.shape
    assert all(r % 8 == 0 for r in row0s) and tn % 8 == 0 and len(epis) == len(row0s)
    tpg = n // tn
    n_tiles = tpg * len(row0s)

    def slab_row(j):
        row0 = row0s[0]
        for g in range(1, len(row0s)):
            row0 = jnp.where(j // tpg >= g, row0s[g], row0)
        return pl.multiple_of(row0 + (j % tpg) * tn, 8)

    grid = (n_tiles, rows // tm)
    in_specs = [pl.BlockSpec((tm, k), lambda j, i: (i, 0)),
                pl.BlockSpec((pl.Element(1), pl.Element(tn), pl.Element(k)),
                             lambda j, i: (layer, slab_row(j), 0))]
    args = [x, w_t]
    for arr, width in col_args:
        in_specs.append(pl.BlockSpec((arr.shape[0], width), lambda j, i: (0, j)))
        args.append(arr)
    for arr, width, follows_n in tile_args:
        if follows_n:
            in_specs.append(pl.BlockSpec((tm, width), lambda j, i: (i, j)))
        else:
            in_specs.append(pl.BlockSpec((tm, width), lambda j, i: (i, 0)))
        args.append(arr)
    out_specs = [pl.BlockSpec((tm, width), lambda j, i: (i, j)) for width, _ in outs]
    out_shape = [jax.ShapeDtypeStruct((rows, width * n_tiles), dt) for width, dt in outs]
    res = pl.pallas_call(
        functools.partial(_mm_body, epis=epis, tiles_per_group=tpg, n_extra=len(col_args) + len(tile_args)),
        grid=grid,
        in_specs=in_specs,
        out_specs=out_specs,
        out_shape=out_shape,
        scratch_shapes=[pltpu.VMEM((tn, k), BF16)],
        compiler_params=_params("parallel", "arbitrary"),
        name=name,
    )(*args)
    return res


def _mm_rows_body(x_ref, w_ref, *rest, epi, n_extra, tn, widths):
    extra = [e[...] for e in rest[:n_extra]]
    outs = rest[n_extra:]
    x = x_ref[...]
    for p in range(w_ref.shape[1] // tn):
        res = epi(_dot(x, w_ref[:, p * tn:(p + 1) * tn]), *extra)
        for o_ref, r, width in zip(outs, res, widths):
            o_ref[:, p * width:(p + 1) * width] = r.astype(o_ref.dtype)


def _mm_rows(x, w, epi, outs, *, layer, tm, tn, rows=None, tile_args=(), name):
    k = x.shape[1]
    rows = x.shape[0] if rows is None else rows
    n = w.shape[2]
    in_specs = [pl.BlockSpec((tm, k), lambda i: (i, 0)),
                pl.BlockSpec((None, k, n), lambda i: (layer, 0, 0))]
    in_specs += [pl.BlockSpec((tm, width), lambda i: (i, 0)) for _, width in tile_args]
    return pl.pallas_call(
        functools.partial(_mm_rows_body, epi=epi, n_extra=len(tile_args), tn=tn,
                          widths=[width for width, _ in outs]),
        grid=(rows // tm,),
        in_specs=in_specs,
        out_specs=[pl.BlockSpec((tm, width * (n // tn)), lambda i: (i, 0)) for width, _ in outs],
        out_shape=[jax.ShapeDtypeStruct((rows, width * (n // tn)), dt) for width, dt in outs],
        compiler_params=_params("parallel"),
        name=name,
    )(x, w, *[arr for arr, _ in tile_args])


def _silu(z):
    return z * jax.nn.sigmoid(z)


def _epi_silu(acc):
    return (_silu(acc),)


def _epi_sigmoid(acc):
    return (jax.nn.sigmoid(acc),)


def _epi_id(acc):
    return (acc,)


def _epi_relu2(acc):
    r = jnp.maximum(acc, 0.0)
    return (r * r,)


def _epi_forget(acc, lb_logits, *, layer):
    mx = jnp.max(lb_logits, axis=0, keepdims=True)
    e = jnp.exp(lb_logits - mx)
    sm = e / jnp.sum(e, axis=0, keepdims=True)
    lb = jnp.zeros_like(mx)
    for i in range(1, layer + 1):
        lb = lb + sm[i:i + 1]
    z = acc
    e = jnp.exp(-jnp.abs(z))
    r = 1.0 / (1.0 + e)
    sig_neg = jnp.where(z >= 0.0, e * r, r)
    log_sig = jnp.minimum(z, 0.0) - jnp.log(1.0 + e)
    a = jnp.log(lb)
    c = jnp.log1p(-lb) + log_sig
    hi = jnp.maximum(a, c)
    lo = jnp.minimum(a, c)
    logf = hi + jnp.log(1.0 + jnp.exp(lo - hi))
    kk = (1.0 - lb) * sig_neg
    return kk, logf


def _epi_q(acc, cos_t, sin_t):
    n0 = acc[:, 0 * LANES:1 * LANES] * Q_SCALE
    p0 = acc[:, 1 * LANES:2 * LANES] * cos_t + acc[:, 4 * LANES:5 * LANES] * sin_t
    n1 = acc[:, 2 * LANES:3 * LANES] * Q_SCALE
    p1 = acc[:, 3 * LANES:4 * LANES] * cos_t + acc[:, 5 * LANES:6 * LANES] * sin_t
    return (jnp.concatenate([n0, p0, n1, p1], axis=1),)


def _epi_kv(acc, krot):
    krot = krot.astype(F32)
    keys = jnp.concatenate([acc[:, 0:LANES], krot, acc[:, LANES:2 * LANES], krot], axis=1)
    return keys, acc[:, 2 * LANES:4 * LANES]


def _latent_body(h_ref, w_ref, gq_ref, gkv_ref, cos_ref, sin_ref,
                 qn_ref, ckv_ref, ckvb_ref, krot_ref, krotb_ref, w_scr):
    base = Q_LORA + KV_LORA
    half = QK_ROPE // 2

    @pl.when(pl.program_id(0) == 0)
    def _():
        w_scr[...] = jnp.zeros_like(w_scr)
        w_scr[0:base + QK_ROPE, :] = w_ref[...].astype(w_scr.dtype)
        w_scr[base + LANES:base + LANES + half, :] = w_ref[base + half:base + QK_ROPE, :].astype(w_scr.dtype)
        w_scr[base + LANES + half:base + LANES + QK_ROPE, :] = w_ref[base:base + half, :].astype(w_scr.dtype)

    acc = _dot_nt(h_ref[...], w_scr[...])
    qn_ref[...] = _rms(acc[:, :Q_LORA], gq_ref[...]).astype(qn_ref.dtype)
    ckv = _rms(acc[:, Q_LORA:base], gkv_ref[...])
    ckv_ref[...] = ckv
    ckvb_ref[...] = ckv.astype(ckvb_ref.dtype)
    kr = acc[:, base:base + LANES] * cos_ref[...] + acc[:, base + LANES:base + 2 * LANES] * sin_ref[...]
    krot_ref[...] = kr[:, :QK_ROPE]
    krotb_ref[...] = kr.astype(krotb_ref.dtype)


def _latent(h, w_t, gq, gkv, cos_t, sin_t, layer, tm):
    rows, d = h.shape
    n_in = Q_LORA + KV_LORA + QK_ROPE
    row = lambda width: pl.BlockSpec((tm, width), lambda i: (i, 0))
    const = lambda r, width: pl.BlockSpec((r, width), lambda i: (0, 0))
    return pl.pallas_call(
        _latent_body,
        grid=(rows // tm,),
        in_specs=[row(d), pl.BlockSpec((None, n_in, d), lambda i: (layer, 0, 0)),
                  const(1, Q_LORA), const(1, KV_LORA), row(LANES), row(LANES)],
        out_specs=[row(Q_LORA), row(KV_LORA), row(KV_LORA), row(QK_ROPE), row(LANES)],
        out_shape=[jax.ShapeDtypeStruct((rows, Q_LORA), BF16),
                   jax.ShapeDtypeStruct((rows, KV_LORA), F32),
                   jax.ShapeDtypeStruct((rows, KV_LORA), BF16),
                   jax.ShapeDtypeStruct((rows, QK_ROPE), F32),
                   jax.ShapeDtypeStruct((rows, LANES), BF16)],
        scratch_shapes=[pltpu.VMEM((Q_LORA + KV_LORA + 2 * LANES, d), BF16)],
        compiler_params=_params("arbitrary"),
        name="latent_proj",
    )(h, w_t, gq.reshape(1, -1), gkv.reshape(1, -1), cos_t, sin_t)


def _attn_body(q_ref, k_ref, v_ref, o_ref, *, tq, nq):
    k_chunk = lax.broadcasted_iota(jnp.int32, (tq, tq), 0) // CHUNK
    q_chunk = lax.broadcasted_iota(jnp.int32, (tq, tq), 1) // CHUNK
    visible = k_chunk <= q_chunk
    v_t = v_ref[...].T

    scores = []
    for n_past in range(nq):
        lo = n_past * tq
        q = q_ref[lo:lo + tq, :]
        s_d = jnp.where(visible, _dot_nt(k_ref[lo:lo + tq, :], q), NEG_BIG)
        s_p = _dot_nt(k_ref[0:lo, :], q) if n_past else None
        scores.append((s_d, s_p))
    probs = []
    for s_d, s_p in scores:
        m = jnp.max(s_d, axis=0, keepdims=True)
        if s_p is not None:
            m = jnp.maximum(m, jnp.max(s_p, axis=0, keepdims=True))
        p_d = jnp.exp2(s_d - m)
        l = jnp.sum(p_d, axis=0, keepdims=True)
        p_p = None
        if s_p is not None:
            p_p = jnp.exp2(s_p - m)
            l = l + jnp.sum(p_p, axis=0, keepdims=True)
            p_p = p_p.astype(BF16)
        probs.append((p_d.astype(BF16), p_p, l))
    for n_past, (p_d, p_p, l) in enumerate(probs):
        lo = n_past * tq
        acc = _dot(v_t[:, lo:lo + tq], p_d)
        if p_p is not None:
            acc = acc + _dot(v_t[:, 0:lo], p_p)
        o_ref[lo:lo + tq, :] = (acc / l).T.astype(o_ref.dtype)


def _prompt_attention(q, k, v, batch, seq, tq):
    return pl.pallas_call(
        functools.partial(_attn_body, tq=tq, nq=seq // tq),
        grid=(batch, MLA_HEADS),
        in_specs=[pl.BlockSpec((seq, QK_PAD), lambda b, h: (b, h)),
                  pl.BlockSpec((seq, QK_PAD), lambda b, h: (b, h)),
                  pl.BlockSpec((seq, V_HEAD), lambda b, h: (b, h))],
        out_specs=pl.BlockSpec((seq, V_HEAD), lambda b, h: (b, h)),
        out_shape=jax.ShapeDtypeStruct((batch * seq, MLA_HEADS * V_HEAD), BF16),
        compiler_params=_params("parallel", "parallel"),
        name="prompt_attention",
    )(q, k, v)


def _sattn_body(q_ref, cn_ref, kn_ref, cc_ref, ck_ref, wuk_ref, wuv_ref, o_ref, qa_scr, qp_scr,
                *, dec_seq, key_tile):
    for h in range(MLA_HEADS):
        qn = q_ref[:, h * QK_PAD:h * QK_PAD + QK_NOPE]
        qa_scr[h * dec_seq:(h + 1) * dec_seq, :] = _dot(qn, wuk_ref[h]).astype(BF16)
        qp_scr[h * dec_seq:(h + 1) * dec_seq, :] = q_ref[:, h * QK_PAD + QK_NOPE:(h + 1) * QK_PAD]
    qa = qa_scr[...]
    qp = qp_scr[...]
    rows = MLA_HEADS * dec_seq

    def update(carry, s, c):
        m, l, acc = carry
        m_new = jnp.maximum(m, jnp.max(s, axis=-1, keepdims=True))
        alpha = jnp.exp2(m - m_new)
        p = jnp.exp2(s - m_new)
        l = alpha * l + jnp.sum(p, axis=-1, keepdims=True)
        acc = alpha * acc + _dot(p.astype(BF16), c)
        return m_new, l, acc

    carry = (jnp.full((rows, 1), NEG_BIG, F32), jnp.zeros((rows, 1), F32), jnp.zeros((rows, KV_LORA), F32))
    past = cc_ref.shape[0]
    for t in range(past // key_tile):
        c = cc_ref[t * key_tile:(t + 1) * key_tile, :].astype(BF16)
        kp_t = ck_ref[:, t * key_tile:(t + 1) * key_tile].astype(BF16)
        s = _dot_nt(qa, c) + _dot(qp[:, :QK_ROPE], kp_t)
        carry = update(carry, s, c)
    cn = cn_ref[...]
    s = _dot_nt(qa, cn) + _dot_nt(qp, kn_ref[...])
    _, l, acc = update(carry, s, cn)
    o_lat = (acc / l).astype(BF16)
    for h in range(MLA_HEADS):
        o_ref[:, h * V_HEAD:(h + 1) * V_HEAD] = _dot(
            o_lat[h * dec_seq:(h + 1) * dec_seq, :], wuv_ref[h]).astype(o_ref.dtype)


def _sample_attention(q, ckv_b, krot_b, cache_c, cache_k, wuk_t, wuv_h, layer, n_prompt_rows, dec_batch, dec_seq):
    rb = n_prompt_rows // dec_seq
    past = cache_c.shape[2]
    key_tile = min(past, 1024)
    per_layer = lambda s: pl.BlockSpec((None,) + s, lambda b: (layer, 0, 0, 0))
    return pl.pallas_call(
        functools.partial(_sattn_body, dec_seq=dec_seq, key_tile=key_tile),
        grid=(dec_batch,),
        in_specs=[pl.BlockSpec((dec_seq, MLA_HEADS * QK_PAD), lambda b: (rb + b, 0)),
                  pl.BlockSpec((dec_seq, KV_LORA), lambda b: (rb + b, 0)),
                  pl.BlockSpec((dec_seq, LANES), lambda b: (rb + b, 0)),
                  pl.BlockSpec((None, None, past, KV_LORA), lambda b: (layer, b, 0, 0)),
                  pl.BlockSpec((None, None, QK_ROPE, past), lambda b: (layer, b, 0, 0)),
                  per_layer((MLA_HEADS, QK_NOPE, KV_LORA)),
                  per_layer((MLA_HEADS, KV_LORA, V_HEAD))],
        out_specs=pl.BlockSpec((dec_seq, MLA_HEADS * V_HEAD), lambda b: (b, 0)),
        out_shape=jax.ShapeDtypeStruct((dec_batch * dec_seq, MLA_HEADS * V_HEAD), BF16),
        scratch_shapes=[pltpu.VMEM((MLA_HEADS * dec_seq, KV_LORA), BF16),
                        pltpu.VMEM((MLA_HEADS * dec_seq, LANES), BF16)],
        compiler_params=_params("parallel"),
        name="sample_attention",
    )(q, ckv_b, krot_b, cache_c, cache_k, wuk_t, wuv_h)


def _hgrn_tables(length):
    t = np.arange(length)[:, None]
    r = np.arange(length)[None, :]
    groups = [(r <= t), (r > t)]
    masks = [(r == t)]
    m = length // 2
    while m >= 1:
        blk = t // (2 * m)
        start2 = blk * 2 * m + m
        second = (t % (2 * m)) >= m
        incl = second & (r >= start2) & (r <= t)
        excl = (~second) & (r > t) & (r < start2)
        groups.append(incl | excl)
        r_blk = r // (2 * m)
        r_first = (r % (2 * m)) < m
        masks.append(second & r_first & (r_blk == blk))
        m //= 2
    return (np.concatenate(groups, axis=0).astype(np.float32),
            np.stack(masks, axis=0).astype(np.float32))


def _hgrn_body(*refs, length, n_chunks, group, has_state):
    if has_state:
        q_ref, k_ref, lf_ref, v_ref, g_ref, gn_ref, sum_ref, mask_ref, s0_ref, o_ref, sout_ref, st_scr = refs
    else:
        q_ref, k_ref, lf_ref, v_ref, g_ref, gn_ref, sum_ref, mask_ref, o_ref, sout_ref, st_scr = refs
    tb = pl.program_id(2)

    @pl.when(tb == 0)
    def _():
        for g in range(group):
            st_scr[g] = s0_ref[g].T if has_state else jnp.zeros((HG_DV, HG_DK), F32)

    n_lev = mask_ref.shape[0] - 1
    summat = sum_ref[...]
    gn = gn_ref[...]

    pack_levels = length % LANES == 0 and (n_lev + 1) % 2 == 0
    zeros_k = jnp.zeros((length, HG_DK), BF16)

    def mixed_rows(q, k, half):
        parts = []
        for r in range(0, length, half):
            parts.append((q if (r // half) % 2 else k)[r:r + half])
        return jnp.concatenate(parts, axis=0)

    def level_operands(rows, g, e_all):
        cols = slice(g * HG_DK, (g + 1) * HG_DK)
        qs = [q_ref[rows, cols]]
        ks = [k_ref[rows, cols]]
        q = qs[0].astype(F32)
        k = ks[0].astype(F32)
        half = length // 2
        for lev in range(n_lev):
            e = e_all[(2 + lev) * length:(3 + lev) * length]
            if half % 8 == 0:
                u = (mixed_rows(q, k, half) * e).astype(BF16)
                qs.append(u)
                ks.append(u)
            else:
                qs.append((q * e).astype(BF16))
                ks.append((k * e).astype(BF16))
            half //= 2
        q_in = (q * e_all[0:length]).astype(BF16)
        k_out = (k * e_all[length:2 * length]).astype(BF16)
        return qs, ks, q_in, k_out

    def level_scores(qs, ks):
        if pack_levels:
            out = []
            for p in range(0, n_lev + 1, 2):
                lhs = jnp.concatenate([qs[p], qs[p + 1]], axis=1)
                rhs = jnp.concatenate([jnp.concatenate([ks[p], zeros_k], axis=1),
                                       jnp.concatenate([zeros_k, ks[p + 1]], axis=1)], axis=0)
                a = _dot_nt(lhs, rhs)
                out += [a[:, :length], a[:, length:]]
            return out
        return [_dot_nt(qs[p], ks[p]) for p in range(n_lev + 1)]

    def chunk(c, _):
        rows = pl.ds(pl.multiple_of(c * length, length), length)
        e_heads = []
        for pair in range(group // 2):
            cols2 = slice(2 * pair * HG_DK, (2 * pair + 2) * HG_DK)
            lf2 = lf_ref[rows, cols2] * LOG2_E
            hi = lf2.astype(BF16)
            mid = (lf2 - hi.astype(F32)).astype(BF16)
            e_pair = jnp.exp2(_dot(summat, jnp.concatenate([hi, mid], axis=0)))
            e_heads += [e_pair[:, :HG_DK], e_pair[:, HG_DK:]]
        work = []
        for g in range(group):
            qs, ks, q_in, k_out = level_operands(rows, g, e_heads[g])
            st = st_scr[g]
            work.append((level_scores(qs, ks), _dot_nt(q_in, st.astype(BF16)), k_out, st))
        for g in range(group):
            cols = slice(g * HG_DK, (g + 1) * HG_DK)
            scores, o_in, k_out, st = work[g]
            v = v_ref[rows, cols]
            att = mask_ref[0] * scores[0]
            for p in range(1, n_lev + 1):
                att = att + mask_ref[p] * scores[p]
            o = o_in + _dot(att.astype(BF16), v)
            e_last = e_heads[g][length - 1:length, :]
            st_scr[g] = st * e_last + _dot_tn(v, k_out)
            o_ref[rows, cols] = (_rms(o, gn) * g_ref[rows, cols]).astype(o_ref.dtype)
        return 0

    lax.fori_loop(0, n_chunks, chunk, 0)

    @pl.when(tb == pl.num_programs(2) - 1)
    def _():
        for g in range(group):
            sout_ref[g] = st_scr[g].T


def _hgrn(q, k, logf, v, gate, gn, state0, *, layer, row_block0, n_streams, stream_len, length, block_len, group):
    n_tb = stream_len // block_len
    n_chunks = block_len // length
    width = group * HG_DK
    summat, masks = _hgrn_tables(length)
    summat = jnp.asarray(np.concatenate([summat, summat], axis=1), BF16)
    masks = jnp.asarray(masks, F32)
    tok = lambda col0: pl.BlockSpec((block_len, width),
                                    lambda n, h, t: (row_block0 + n * n_tb + t, col0 // width + h))
    tokens = (q, k, logf, v, gate)
    st_spec = pl.BlockSpec((None, group, HG_DK, HG_DV), lambda n, h, t: (n, h, 0, 0))
    in_specs = [tok(col0) for _, col0 in tokens] + [pl.BlockSpec((1, HG_DV), lambda n, h, t: (0, 0)),
                pl.BlockSpec(summat.shape, lambda n, h, t: (0, 0)),
                pl.BlockSpec(masks.shape, lambda n, h, t: (0, 0, 0))]
    args = [arr for arr, _ in tokens] + [gn.reshape(1, HG_DV), summat, masks]
    if state0 is not None:
        in_specs.append(pl.BlockSpec((None, None, group, HG_DK, HG_DV), lambda n, h, t: (layer, n, h, 0, 0)))
        args.append(state0)
    return pl.pallas_call(
        functools.partial(_hgrn_body, length=length, n_chunks=n_chunks, group=group,
                          has_state=state0 is not None),
        grid=(n_streams, HG_HEADS // group, n_tb),
        in_specs=in_specs,
        out_specs=[pl.BlockSpec((block_len, width), lambda n, h, t: (n * n_tb + t, h)), st_spec],
        out_shape=[jax.ShapeDtypeStruct((n_streams * stream_len, HG_HEADS * HG_DV), BF16),
                   jax.ShapeDtypeStruct((n_streams, HG_HEADS, HG_DK, HG_DV), F32)],
        scratch_shapes=[pltpu.VMEM((group, HG_DV, HG_DK), F32)],
        compiler_params=_params("parallel", "parallel", "arbitrary"),
        name="hgrn_state" if state0 is not None else "hgrn_prompt",
    )(*args)


def _merge_body(*refs, tm, n_a):
    oa_parts, oh_parts = refs[:n_a], refs[n_a:-5]
    woa_ref, wob_ref, ga_ref, gb_ref, o_ref = refs[-5:]

    def emit(tiles):
        a = _dot(tiles[0], woa_ref[...])
        b = _dot(tiles[1], wob_ref[...])
        o_ref[...] = (ga_ref[...] * a + gb_ref[...] * b).astype(o_ref.dtype)

    _on_row_tiles([oa_parts, oh_parts], 1, tm, emit)


def _merge(oa_parts, oh_parts, woa, wob, ga, gb, layer, tm, tn):
    rows, k = _stacked_rows(oa_parts), oa_parts[0].shape[1]
    n = woa.shape[2]
    x_specs = lambda parts: _part_specs(parts, tm, lambda j, i: (i, 0), lambda j, i: (0, 0))
    w_spec = pl.BlockSpec((None, k, tn), lambda j, i: (layer, 0, j))
    t_spec = pl.BlockSpec((tm, tn), lambda j, i: (i, j))
    return pl.pallas_call(
        functools.partial(_merge_body, tm=tm, n_a=len(oa_parts)),
        grid=(n // tn, rows // tm),
        in_specs=x_specs(oa_parts) + x_specs(oh_parts) + [w_spec, w_spec]
        + [pl.BlockSpec((tm, tn), lambda j, i, c=col0 // tn: (i, c + j)) for _, col0 in (ga, gb)],
        out_specs=t_spec,
        out_shape=jax.ShapeDtypeStruct((rows, n), BF16),
        compiler_params=_params("parallel", "arbitrary"),
        name="gated_merge",
    )(*oa_parts, *oh_parts, woa, wob, ga[0], gb[0])


def _outproj_body(m_ref, w_ref, *refs, tm):
    x_parts, (g1_ref, g2_ref, x1_ref, h2_ref) = refs[:-4], refs[-4:]

    def emit(tiles):
        y = _dot(m_ref[...], w_ref[...])
        x1 = tiles[0] + _rms(y, g1_ref[...])
        x1_ref[...] = x1
        h2_ref[...] = _rms(x1, g2_ref[...]).astype(h2_ref.dtype)

    _on_row_tiles([x_parts], 0, tm, emit)


def _outproj(mix, w, x_parts, g_post, g_pre2, layer, tm):
    rows, d = mix.shape
    row = pl.BlockSpec((tm, d), lambda i: (i, 0))
    vec = pl.BlockSpec((1, d), lambda i: (0, 0))
    return pl.pallas_call(
        functools.partial(_outproj_body, tm=tm),
        grid=(rows // tm,),
        in_specs=[row, pl.BlockSpec((None, d, d), lambda i: (layer, 0, 0))]
        + _part_specs(x_parts, tm, lambda i: (i, 0), lambda i: (0, 0)) + [vec, vec],
        out_specs=[row, row],
        out_shape=[jax.ShapeDtypeStruct((rows, d), F32), jax.ShapeDtypeStruct((rows, d), BF16)],
        compiler_params=_params("arbitrary"),
        name="out_proj",
    )(mix, w, *x_parts, g_post.reshape(1, d), g_pre2.reshape(1, d))


def _mlp_body(h_ref, wu_ref, wd_ref, x_ref, g_ref, o_ref, *tail_refs, tail_start):
    f = pl.program_id(1)
    u = jnp.maximum(_dot(h_ref[...], wu_ref[...]), 0.0)
    part = _dot((u * u).astype(BF16), wd_ref[...])

    @pl.when(f == 0)
    def _():
        o_ref[...] = part

    @pl.when(f > 0)
    def _():
        o_ref[...] += part

    @pl.when(f == pl.num_programs(1) - 1)
    def _():
        y = x_ref[...] + _rms(o_ref[...], g_ref[...])
        o_ref[...] = y
        if tail_start is not None:
            @pl.when(pl.program_id(0) == pl.num_programs(0) - 1)
            def _():
                tail_refs[0][...] = y[tail_start:, :]


def _mlp(h2, w_up, w_down, x1, g, layer, tm, tf, split_rows=None):
    rows, d = x1.shape
    ff = w_up.shape[2]
    row = pl.BlockSpec((tm, d), lambda i, f: (i, 0))
    if split_rows is None:
        out_specs, out_shape, tail_start = row, jax.ShapeDtypeStruct((rows, d), F32), None
    else:
        tail = rows - split_rows
        tail_start = tm - tail
        assert 0 <= tail_start and tail % 8 == 0
        out_specs = [row, pl.BlockSpec((tail, d), lambda i, f: (0, 0))]
        out_shape = [jax.ShapeDtypeStruct((split_rows, d), F32), jax.ShapeDtypeStruct((tail, d), F32)]
    return pl.pallas_call(
        functools.partial(_mlp_body, tail_start=tail_start),
        grid=(rows // tm, ff // tf),
        in_specs=[row, pl.BlockSpec((None, d, tf), lambda i, f: (layer, 0, f)),
                  pl.BlockSpec((None, tf, d), lambda i, f: (layer, f, 0)), row,
                  pl.BlockSpec((1, d), lambda i, f: (0, 0))],
        out_specs=out_specs,
        out_shape=out_shape,
        compiler_params=_params("arbitrary", "arbitrary"),
        name="mlp",
    )(h2, w_up, w_down, x1, g.reshape(1, d))


def _pad_lanes(w):
    return jnp.concatenate([w, jnp.zeros(w.shape[:-1] + (LANES - w.shape[-1],), w.dtype)], axis=-1)


def _swap_halves(w):
    half = w.shape[-1] // 2
    return jnp.concatenate([w[..., half:], w[..., :half]], axis=-1)


def _prep_weights(w_uq, w_uk, w_uv):
    depth = w_uq.shape[0]
    uq = w_uq.reshape(depth, Q_LORA, MLA_HEADS // 2, 2, QK_HEAD)
    nope = uq[..., :QK_NOPE]
    pe = _pad_lanes(uq[..., QK_NOPE:])
    pe_sw = _pad_lanes(_swap_halves(uq[..., QK_NOPE:]))
    w_q = jnp.concatenate([nope[:, :, :, 0], pe[:, :, :, 0], nope[:, :, :, 1], pe[:, :, :, 1],
                           pe_sw[:, :, :, 0], pe_sw[:, :, :, 1]], axis=-1)
    w_q = w_q.reshape(depth, Q_LORA, -1).astype(BF16)
    uk = w_uk.reshape(depth, KV_LORA, MLA_HEADS // 2, 2 * QK_NOPE)
    uv = w_uv.reshape(depth, KV_LORA, MLA_HEADS // 2, 2 * V_HEAD)
    w_kv = jnp.concatenate([uk, uv], axis=-1).reshape(depth, KV_LORA, -1).astype(BF16)
    wuk_t = jnp.transpose(w_uk, (0, 2, 3, 1)).astype(BF16)
    wuv_h = jnp.transpose(w_uv, (0, 2, 1, 3)).astype(BF16)
    return w_q, w_kv, wuk_t, wuv_h


def _rope_tables(positions):
    half = QK_ROPE // 2
    inv = ROPE_THETA ** (-jnp.arange(half, dtype=F32) / half)
    ang = positions.astype(F32)[:, None] * inv[None, :]
    cos, sin = jnp.cos(ang), jnp.sin(ang)
    zeros = jnp.zeros((positions.shape[0], LANES - QK_ROPE), F32)
    return (jnp.concatenate([cos, cos, zeros], axis=1), jnp.concatenate([-sin, sin, zeros], axis=1))


def kernel(x_prompt, x_sample, cache_ckv, cache_kpe, state_hgrn, pre_mix_g, w_in, q_norm_g, w_uq, kv_norm_g,
           w_uk, w_uv, w_oa, hg_lb, hg_norm_g, w_ob, w_out, post_mix_g, pre_mlp_g, w_up, w_down, post_mlp_g):
    batch, seq, d = x_prompt.shape
    dec_batch, dec_seq, _ = x_sample.shape
    depth = w_in.shape[0]
    past = cache_ckv.shape[2]
    n_p = batch * seq
    n_s = dec_batch * dec_seq
    rows = n_p + n_s
    tm = 768
    tm_wide = 1408
    tm_mlp = 704
    assert rows % tm == 0 and rows % tm_wide == 0 and rows % tm_mlp == 0 and n_p % 1024 == 0

    x_parts = (x_prompt.reshape(n_p, d), x_sample.reshape(n_s, d))
    pos = jnp.concatenate([jnp.tile(jnp.arange(seq, dtype=jnp.int32), batch),
                           jnp.tile(past + jnp.arange(dec_seq, dtype=jnp.int32), dec_batch)])
    cos_t, sin_t = _rope_tables(pos)
    cos_q, sin_q = cos_t * Q_SCALE, sin_t * Q_SCALE

    w_in_t = jnp.transpose(w_in, (0, 2, 1))
    cache_kpe_t = jnp.transpose(cache_kpe, (0, 1, 3, 2))
    wide0 = Q_LORA + KV_LORA + QK_ROPE
    w_q, w_kv, wuk_t, wuv_h = _prep_weights(w_uq, w_uk, w_uv)
    w_oa_b, w_ob_b, w_out_b = w_oa.astype(BF16), w_ob.astype(BF16), w_out.astype(BF16)
    w_up_b, w_down_b = w_up.astype(BF16), w_down.astype(BF16)

    ckv_out, kpe_out, st_p_out, st_s_out = [], [], [], []
    for l in range(depth):
        h = _rmsnorm(x_parts, pre_mix_g[l], tm)
        qn, ckv, ckv_b, krot, krot_b = _latent(h, w_in_t, q_norm_g[l], kv_norm_g[l], cos_t, sin_t, l, tm)
        hk, logf = _mm(h, w_in_t, [functools.partial(_epi_forget, layer=l)], [(1024, BF16), (1024, F32)],
                       layer=l, row0s=[wide0 + d], n=d, tm=tm, tn=1024, col_args=[(hg_lb, 1024)], name="proj_hf")
        (wide,) = _mm(h, w_in_t, [_epi_silu, _epi_id, _epi_silu, _epi_sigmoid, _epi_sigmoid], [(1024, BF16)],
                      layer=l, row0s=[wide0 + g * d for g in (0, 2, 3, 4, 5)], n=d, tm=tm_wide, tn=1024,
                      name="proj_wide")
        hq, hv, hgate, ga, gb = [(wide, g * d) for g in range(5)]
        hk, logf = (hk, 0), (logf, 0)
        (q,) = _mm_rows(qn, w_q, _epi_q, [(2 * QK_PAD, BF16)], layer=l, tm=tm, tn=6 * LANES,
                        tile_args=[(cos_q, LANES), (sin_q, LANES)], name="proj_q")
        keys, vals = _mm_rows(ckv_b, w_kv, _epi_kv, [(2 * QK_PAD, BF16), (2 * V_HEAD, BF16)], layer=l, tm=1024,
                              tn=4 * LANES, rows=n_p, tile_args=[(krot_b, LANES)], name="proj_kv")
        oa_p = _prompt_attention(q, keys, vals, batch, seq, 256)
        oa_s = _sample_attention(q, ckv_b, krot_b, cache_ckv, cache_kpe_t, wuk_t, wuv_h, l, n_p, dec_batch, dec_seq)
        oh_p, st_p = _hgrn(hq, hk, logf, hv, hgate, hg_norm_g[l], None, layer=l,
                           row_block0=0, n_streams=batch, stream_len=seq, length=2 * CHUNK,
                           block_len=512, group=8)
        oh_s, st_s = _hgrn(hq, hk, logf, hv, hgate, hg_norm_g[l], state_hgrn, layer=l,
                           row_block0=n_p // dec_seq, n_streams=dec_batch, stream_len=dec_seq, length=dec_seq,
                           block_len=dec_seq, group=4)
        mix = _merge((oa_p, oa_s), (oh_p, oh_s), w_oa_b, w_ob_b, ga, gb, l, tm, 1024)
        x1, h2 = _outproj(mix, w_out_b, x_parts, post_mix_g[l], pre_mlp_g[l], l, 384)
        if l + 1 < depth:
            x_parts = (_mlp(h2, w_up_b, w_down_b, x1, post_mlp_g[l], l, tm_mlp, 1024),)
        else:
            y_p, y_s = _mlp(h2, w_up_b, w_down_b, x1, post_mlp_g[l], l, tm_mlp, 1024, split_rows=n_p)

        ckv_out.append(ckv)
        kpe_out.append(krot)
        st_p_out.append(st_p)
        st_s_out.append(st_s)

    ckv_all = jnp.stack(ckv_out)
    kpe_all = jnp.stack(kpe_out)
    return (y_p.reshape(batch, seq, d),
            y_s.reshape(dec_batch, dec_seq, d),
            ckv_all[:, :n_p].reshape(depth, batch, seq, KV_LORA),
            kpe_all[:, :n_p].reshape(depth, batch, seq, QK_ROPE),
            jnp.stack(st_p_out),
            ckv_all[:, n_p:].reshape(depth, dec_batch, dec_seq, KV_LORA),
            kpe_all[:, n_p:].reshape(depth, dec_batch, dec_seq, QK_ROPE),
            jnp.stack(st_s_out))
```

```python
import functools

import jax
import jax.numpy as jnp
import numpy as np
from jax import lax
from jax.experimental import pallas as pl
from jax.experimental.pallas import tpu as pltpu

F32 = jnp.float32
BF16 = jnp.bfloat16

CHUNK = 64
MLA_HEADS = 16
QK_NOPE = 128
QK_ROPE = 64
QK_HEAD = QK_NOPE + QK_ROPE
V_HEAD = 128
Q_LORA = 512
KV_LORA = 512
ROPE_THETA = 10000.0
ATTN_SCALE = QK_HEAD ** -0.5
HG_HEADS = 16
HG_DK = 128
HG_DV = 128
EPS = 1e-6

LANES = 128
QK_PAD = 2 * LANES
NEG_BIG = -1e30
LOG2_E = 1.4426950408889634
Q_SCALE = ATTN_SCALE * LOG2_E
VMEM_LIMIT = 58 * 1024 * 1024


def _params(*sem):
    return pltpu.CompilerParams(dimension_semantics=sem, vmem_limit_bytes=VMEM_LIMIT)


def _rms(x, g):
    return x * lax.rsqrt(jnp.mean(x * x, axis=-1, keepdims=True) + EPS) * g


def _dot(a, b):
    return jnp.dot(a, b, preferred_element_type=F32)


def _dot_nt(a, b):
    return lax.dot_general(a, b, (((1,), (1,)), ((), ())), preferred_element_type=F32)


def _dot_tn(a, b):
    return lax.dot_general(a, b, (((0,), (0,)), ((), ())), preferred_element_type=F32)


def _part_specs(parts, tm, index_map, tail_index_map):
    specs = [pl.BlockSpec((tm, parts[0].shape[1]), index_map)]
    if len(parts) == 2:
        specs.append(pl.BlockSpec(parts[1].shape, tail_index_map))
    return specs


def _stacked_rows(parts):
    return sum(p.shape[0] for p in parts)


def _on_row_tiles(operands, row_axis, tm, fn):
    if all(len(o) == 1 for o in operands):
        fn([o[0][...] for o in operands])
        return
    is_last = pl.program_id(row_axis) == pl.num_programs(row_axis) - 1

    @pl.when(jnp.logical_not(is_last))
    def _():
        fn([o[0][...] for o in operands])

    @pl.when(is_last)
    def _():
        fn([o[0][...] if len(o) == 1 else
            jnp.concatenate([o[0][:tm - o[1].shape[0]], o[1][...]], axis=0) for o in operands])


def _rmsnorm_body(*refs, tm):
    x_parts, (g_ref, o_ref) = refs[:-2], refs[-2:]

    def emit(tiles):
        o_ref[...] = _rms(tiles[0], g_ref[...]).astype(o_ref.dtype)

    _on_row_tiles([x_parts], 0, tm, emit)


def _rmsnorm(x_parts, g, tm):
    rows, d = _stacked_rows(x_parts), x_parts[0].shape[1]
    return pl.pallas_call(
        functools.partial(_rmsnorm_body, tm=tm),
        grid=(rows // tm,),
        in_specs=_part_specs(x_parts, tm, lambda i: (i, 0), lambda i: (0, 0))
        + [pl.BlockSpec((1, d), lambda i: (0, 0))],
        out_specs=pl.BlockSpec((tm, d), lambda i: (i, 0)),
        out_shape=jax.ShapeDtypeStruct((rows, d), BF16),
        compiler_params=_params("arbitrary"),
        name="rmsnorm",
    )(*x_parts, g.reshape(1, d))


def _mm_body(x_ref, w_ref, *rest, epis, tiles_per_group, n_extra):
    extra = rest[:n_extra]
    outs = rest[n_extra:-1]
    w_scr = rest[-1]

    @pl.when(pl.program_id(1) == 0)
    def _():
        w_scr[...] = w_ref[0].astype(w_scr.dtype)

    acc = _dot_nt(x_ref[...], w_scr[...])

    def emit(epi):
        res = epi(acc, *[e[...] for e in extra])
        for o_ref, r in zip(outs, res):
            o_ref[...] = r.astype(o_ref.dtype)

    if len(epis) == 1:
        emit(epis[0])
    else:
        group = pl.program_id(0) // tiles_per_group
        for g, epi in enumerate(epis):
            pl.when(group == g)(functools.partial(emit, epi))


def _mm(x, w_t, epis, outs, *, layer, row0s, n, tm, tn, col_args=(), tile_args=(), name):
    rows, k = x.shape
    assert all(r % 8 == 0 for r in row0s) and tn % 8 == 0 and len(epis) == len(row0s)
    tpg = n // tn
    n_tiles = tpg * len(row0s)

    def slab_row(j):
        row0 = row0s[0]
        for g in range(1, len(row0s)):
            row0 = jnp.where(j // tpg >= g, row0s[g], row0)
        return pl.multiple_of(row0 + (j % tpg) * tn, 8)

    grid = (n_tiles, rows // tm)
    in_specs = [pl.BlockSpec((tm, k), lambda j, i: (i, 0)),
                pl.BlockSpec((pl.Element(1), pl.Element(tn), pl.Element(k)),
                             lambda j, i: (layer, slab_row(j), 0))]
    args = [x, w_t]
    for arr, width in col_args:
        in_specs.append(pl.BlockSpec((arr.shape[0], width), lambda j, i: (0, j)))
        args.append(arr)
    for arr, width, follows_n in tile_args:
        if follows_n:
            in_specs.append(pl.BlockSpec((tm, width), lambda j, i: (i, j)))
        else:
            in_specs.append(pl.BlockSpec((tm, width), lambda j, i: (i, 0)))
        args.append(arr)
    out_specs = [pl.BlockSpec((tm, width), lambda j, i: (i, j)) for width, _ in outs]
    out_shape = [jax.ShapeDtypeStruct((rows, width * n_tiles), dt) for width, dt in outs]
    res = pl.pallas_call(
        functools.partial(_mm_body, epis=epis, tiles_per_group=tpg, n_extra=len(col_args) + len(tile_args)),
        grid=grid,
        in_specs=in_specs,
        out_specs=out_specs,
        out_shape=out_shape,
        scratch_shapes=[pltpu.VMEM((tn, k), BF16)],
        compiler_params=_params("parallel", "arbitrary"),
        name=name,
    )(*args)
    return res


def _mm_rows_body(x_ref, w_ref, *rest, epi, n_extra, tn, widths):
    extra = [e[...] for e in rest[:n_extra]]
    outs = rest[n_extra:]
    x = x_ref[...]
    for p in range(w_ref.shape[1] // tn):
        res = epi(_dot(x, w_ref[:, p * tn:(p + 1) * tn]), *extra)
        for o_ref, r, width in zip(outs, res, widths):
            o_ref[:, p * width:(p + 1) * width] = r.astype(o_ref.dtype)


def _mm_rows(x, w, epi, outs, *, layer, tm, tn, rows=None, tile_args=(), name):
    k = x.shape[1]
    rows = x.shape[0] if rows is None else rows
    n = w.shape[2]
    in_specs = [pl.BlockSpec((tm, k), lambda i: (i, 0)),
                pl.BlockSpec((None, k, n), lambda i: (layer, 0, 0))]
    in_specs += [pl.BlockSpec((tm, width), lambda i: (i, 0)) for _, width in tile_args]
    return pl.pallas_call(
        functools.partial(_mm_rows_body, epi=epi, n_extra=len(tile_args), tn=tn,
                          widths=[width for width, _ in outs]),
        grid=(rows // tm,),
        in_specs=in_specs,
        out_specs=[pl.BlockSpec((tm, width * (n // tn)), lambda i: (i, 0)) for width, _ in outs],
        out_shape=[jax.ShapeDtypeStruct((rows, width * (n // tn)), dt) for width, dt in outs],
        compiler_params=_params("parallel"),
        name=name,
    )(x, w, *[arr for arr, _ in tile_args])


def _silu(z):
    return z * jax.nn.sigmoid(z)


def _epi_silu(acc):
    return (_silu(acc),)


def _epi_sigmoid(acc):
    return (jax.nn.sigmoid(acc),)


def _epi_id(acc):
    return (acc,)


def _epi_relu2(acc):
    r = jnp.maximum(acc, 0.0)
    return (r * r,)


def _epi_forget(acc, lb_logits, *, layer):
    mx = jnp.max(lb_logits, axis=0, keepdims=True)
    e = jnp.exp(lb_logits - mx)
    sm = e / jnp.sum(e, axis=0, keepdims=True)
    lb = jnp.zeros_like(mx)
    for i in range(1, layer + 1):
        lb = lb + sm[i:i + 1]
    z = acc
    e = jnp.exp(-jnp.abs(z))
    r = 1.0 / (1.0 + e)
    sig_neg = jnp.where(z >= 0.0, e * r, r)
    log_sig = jnp.minimum(z, 0.0) - jnp.log(1.0 + e)
    a = jnp.log(lb)
    c = jnp.log1p(-lb) + log_sig
    hi = jnp.maximum(a, c)
    lo = jnp.minimum(a, c)
    logf = hi + jnp.log(1.0 + jnp.exp(lo - hi))
    kk = (1.0 - lb) * sig_neg
    return kk, logf


def _epi_q(acc, cos_t, sin_t):
    n0 = acc[:, 0 * LANES:1 * LANES] * Q_SCALE
    p0 = acc[:, 1 * LANES:2 * LANES] * cos_t + acc[:, 4 * LANES:5 * LANES] * sin_t
    n1 = acc[:, 2 * LANES:3 * LANES] * Q_SCALE
    p1 = acc[:, 3 * LANES:4 * LANES] * cos_t + acc[:, 5 * LANES:6 * LANES] * sin_t
    return (jnp.concatenate([n0, p0, n1, p1], axis=1),)


def _epi_kv(acc, krot):
    krot = krot.astype(F32)
    keys = jnp.concatenate([acc[:, 0:LANES], krot, acc[:, LANES:2 * LANES], krot], axis=1)
    return keys, acc[:, 2 * LANES:4 * LANES]


def _latent_body(h_ref, w_ref, gq_ref, gkv_ref, cos_ref, sin_ref,
                 qn_ref, ckv_ref, ckvb_ref, krot_ref, krotb_ref, w_scr):
    base = Q_LORA + KV_LORA
    half = QK_ROPE // 2

    @pl.when(pl.program_id(0) == 0)
    def _():
        w_scr[...] = jnp.zeros_like(w_scr)
        w_scr[0:base + QK_ROPE, :] = w_ref[...].astype(w_scr.dtype)
        w_scr[base + LANES:base + LANES + half, :] = w_ref[base + half:base + QK_ROPE, :].astype(w_scr.dtype)
        w_scr[base + LANES + half:base + LANES + QK_ROPE, :] = w_ref[base:base + half, :].astype(w_scr.dtype)

    acc = _dot_nt(h_ref[...], w_scr[...])
    qn_ref[...] = _rms(acc[:, :Q_LORA], gq_ref[...]).astype(qn_ref.dtype)
    ckv = _rms(acc[:, Q_LORA:base], gkv_ref[...])
    ckv_ref[...] = ckv
    ckvb_ref[...] = ckv.astype(ckvb_ref.dtype)
    kr = acc[:, base:base + LANES] * cos_ref[...] + acc[:, base + LANES:base + 2 * LANES] * sin_ref[...]
    krot_ref[...] = kr[:, :QK_ROPE]
    krotb_ref[...] = kr.astype(krotb_ref.dtype)


def _latent(h, w_t, gq, gkv, cos_t, sin_t, layer, tm):
    rows, d = h.shape
    n_in = Q_LORA + KV_LORA + QK_ROPE
    row = lambda width: pl.BlockSpec((tm, width), lambda i: (i, 0))
    const = lambda r, width: pl.BlockSpec((r, width), lambda i: (0, 0))
    return pl.pallas_call(
        _latent_body,
        grid=(rows // tm,),
        in_specs=[row(d), pl.BlockSpec((None, n_in, d), lambda i: (layer, 0, 0)),
                  const(1, Q_LORA), const(1, KV_LORA), row(LANES), row(LANES)],
        out_specs=[row(Q_LORA), row(KV_LORA), row(KV_LORA), row(QK_ROPE), row(LANES)],
        out_shape=[jax.ShapeDtypeStruct((rows, Q_LORA), BF16),
                   jax.ShapeDtypeStruct((rows, KV_LORA), F32),
                   jax.ShapeDtypeStruct((rows, KV_LORA), BF16),
                   jax.ShapeDtypeStruct((rows, QK_ROPE), F32),
                   jax.ShapeDtypeStruct((rows, LANES), BF16)],
        scratch_shapes=[pltpu.VMEM((Q_LORA + KV_LORA + 2 * LANES, d), BF16)],
        compiler_params=_params("arbitrary"),
        name="latent_proj",
    )(h, w_t, gq.reshape(1, -1), gkv.reshape(1, -1), cos_t, sin_t)


def _attn_body(q_ref, k_ref, v_ref, o_ref, *, tq, nq):
    k_chunk = lax.broadcasted_iota(jnp.int32, (tq, tq), 0) // CHUNK
    q_chunk = lax.broadcasted_iota(jnp.int32, (tq, tq), 1) // CHUNK
    visible = k_chunk <= q_chunk
    v_t = v_ref[...].T

    scores = []
    for n_past in range(nq):
        lo = n_past * tq
        q = q_ref[lo:lo + tq, :]
        s_d = jnp.where(visible, _dot_nt(k_ref[lo:lo + tq, :], q), NEG_BIG)
        s_p = _dot_nt(k_ref[0:lo, :], q) if n_past else None
        scores.append((s_d, s_p))
    probs = []
    for s_d, s_p in scores:
        m = jnp.max(s_d, axis=0, keepdims=True)
        if s_p is not None:
            m = jnp.maximum(m, jnp.max(s_p, axis=0, keepdims=True))
        p_d = jnp.exp2(s_d - m)
        l = jnp.sum(p_d, axis=0, keepdims=True)
        p_p = None
        if s_p is not None:
            p_p = jnp.exp2(s_p - m)
            l = l + jnp.sum(p_p, axis=0, keepdims=True)
            p_p = p_p.astype(BF16)
        probs.append((p_d.astype(BF16), p_p, l))
    for n_past, (p_d, p_p, l) in enumerate(probs):
        lo = n_past * tq
        acc = _dot(v_t[:, lo:lo + tq], p_d)
        if p_p is not None:
            acc = acc + _dot(v_t[:, 0:lo], p_p)
        o_ref[lo:lo + tq, :] = (acc / l).T.astype(o_ref.dtype)


def _prompt_attention(q, k, v, batch, seq, tq):
    return pl.pallas_call(
        functools.partial(_attn_body, tq=tq, nq=seq // tq),
        grid=(batch, MLA_HEADS),
        in_specs=[pl.BlockSpec((seq, QK_PAD), lambda b, h: (b, h)),
                  pl.BlockSpec((seq, QK_PAD), lambda b, h: (b, h)),
                  pl.BlockSpec((seq, V_HEAD), lambda b, h: (b, h))],
        out_specs=pl.BlockSpec((seq, V_HEAD), lambda b, h: (b, h)),
        out_shape=jax.ShapeDtypeStruct((batch * seq, MLA_HEADS * V_HEAD), BF16),
        compiler_params=_params("parallel", "parallel"),
        name="prompt_attention",
    )(q, k, v)


def _sattn_body(q_ref, cn_ref, kn_ref, cc_ref, ck_ref, wuk_ref, wuv_ref, o_ref, qa_scr, qp_scr,
                *, dec_seq, key_tile):
    for h in range(MLA_HEADS):
        qn = q_ref[:, h * QK_PAD:h * QK_PAD + QK_NOPE]
        qa_scr[h * dec_seq:(h + 1) * dec_seq, :] = _dot(qn, wuk_ref[h]).astype(BF16)
        qp_scr[h * dec_seq:(h + 1) * dec_seq, :] = q_ref[:, h * QK_PAD + QK_NOPE:(h + 1) * QK_PAD]
    qa = qa_scr[...]
    qp = qp_scr[...]
    rows = MLA_HEADS * dec_seq

    def update(carry, s, c):
        m, l, acc = carry
        m_new = jnp.maximum(m, jnp.max(s, axis=-1, keepdims=True))
        alpha = jnp.exp2(m - m_new)
        p = jnp.exp2(s - m_new)
        l = alpha * l + jnp.sum(p, axis=-1, keepdims=True)
        acc = alpha * acc + _dot(p.astype(BF16), c)
        return m_new, l, acc

    carry = (jnp.full((rows, 1), NEG_BIG, F32), jnp.zeros((rows, 1), F32), jnp.zeros((rows, KV_LORA), F32))
    past = cc_ref.shape[0]
    for t in range(past // key_tile):
        c = cc_ref[t * key_tile:(t + 1) * key_tile, :].astype(BF16)
        kp_t = ck_ref[:, t * key_tile:(t + 1) * key_tile].astype(BF16)
        s = _dot_nt(qa, c) + _dot(qp[:, :QK_ROPE], kp_t)
        carry = update(carry, s, c)
    cn = cn_ref[...]
    s = _dot_nt(qa, cn) + _dot_nt(qp, kn_ref[...])
    _, l, acc = update(carry, s, cn)
    o_lat = (acc / l).astype(BF16)
    for h in range(MLA_HEADS):
        o_ref[:, h * V_HEAD:(h + 1) * V_HEAD] = _dot(
            o_lat[h * dec_seq:(h + 1) * dec_seq, :], wuv_ref[h]).astype(o_ref.dtype)


def _sample_attention(q, ckv_b, krot_b, cache_c, cache_k, wuk_t, wuv_h, layer, n_prompt_rows, dec_batch, dec_seq):
    rb = n_prompt_rows // dec_seq
    past = cache_c.shape[2]
    key_tile = min(past, 1024)
    per_layer = lambda s: pl.BlockSpec((None,) + s, lambda b: (layer, 0, 0, 0))
    return pl.pallas_call(
        functools.partial(_sattn_body, dec_seq=dec_seq, key_tile=key_tile),
        grid=(dec_batch,),
        in_specs=[pl.BlockSpec((dec_seq, MLA_HEADS * QK_PAD), lambda b: (rb + b, 0)),
                  pl.BlockSpec((dec_seq, KV_LORA), lambda b: (rb + b, 0)),
                  pl.BlockSpec((dec_seq, LANES), lambda b: (rb + b, 0)),
                  pl.BlockSpec((None, None, past, KV_LORA), lambda b: (layer, b, 0, 0)),
                  pl.BlockSpec((None, None, QK_ROPE, past), lambda b: (layer, b, 0, 0)),
                  per_layer((MLA_HEADS, QK_NOPE, KV_LORA)),
                  per_layer((MLA_HEADS, KV_LORA, V_HEAD))],
        out_specs=pl.BlockSpec((dec_seq, MLA_HEADS * V_HEAD), lambda b: (b, 0)),
        out_shape=jax.ShapeDtypeStruct((dec_batch * dec_seq, MLA_HEADS * V_HEAD), BF16),
        scratch_shapes=[pltpu.VMEM((MLA_HEADS * dec_seq, KV_LORA), BF16),
                        pltpu.VMEM((MLA_HEADS * dec_seq, LANES), BF16)],
        compiler_params=_params("parallel"),
        name="sample_attention",
    )(q, ckv_b, krot_b, cache_c, cache_k, wuk_t, wuv_h)


def _hgrn_tables(length):
    t = np.arange(length)[:, None]
    r = np.arange(length)[None, :]
    groups = [(r <= t)]
    masks = [(r == t)]
    m = length // 2
    while m >= 1:
        blk = t // (2 * m)
        start2 = blk * 2 * m + m
        second = (t % (2 * m)) >= m
        incl = second & (r >= start2) & (r <= t)
        excl = (~second) & (r > t) & (r < start2)
        if m % 8:
            groups.append(incl | excl)
        r_blk = r // (2 * m)
        r_first = (r % (2 * m)) < m
        masks.append(second & r_first & (r_blk == blk))
        m //= 2
    return (np.concatenate(groups, axis=0).astype(np.float32),
            np.stack(masks, axis=0).astype(np.float32))


def _hgrn_body(*refs, length, n_chunks, group, has_state):
    if has_state:
        q_ref, k_ref, lf_ref, v_ref, g_ref, gn_ref, sum_ref, mask_ref, s0_ref, o_ref, sout_ref, st_scr = refs
    else:
        q_ref, k_ref, lf_ref, v_ref, g_ref, gn_ref, sum_ref, mask_ref, o_ref, sout_ref, st_scr = refs
    tb = pl.program_id(2)

    @pl.when(tb == 0)
    def _():
        for g in range(group):
            st_scr[g] = s0_ref[g].T if has_state else jnp.zeros((HG_DV, HG_DK), F32)

    n_lev = mask_ref.shape[0] - 1
    summat = sum_ref[...]
    gn = gn_ref[...]

    pack_levels = length % LANES == 0 and (n_lev + 1) % 2 == 0
    zeros_k = jnp.zeros((length, HG_DK), BF16)

    def mixed_rows(q, k, half):
        parts = []
        for r in range(0, length, half):
            parts.append((q if (r // half) % 2 else k)[r:r + half])
        return jnp.concatenate(parts, axis=0)

    halves = [length >> (lev + 1) for lev in range(n_lev)]
    coarse = [half for half in halves if half % 8 == 0]

    def exponentials(sums):
        b = sums[0:length]
        levels = []
        for half in coarse:
            parts = []
            for r in range(0, length, half):
                ref = (r // (2 * half)) * 2 * half + half - 1
                ref_row = b[ref:ref + 1]
                parts.append(b[r:r + half] - ref_row if (r // half) % 2 else ref_row - b[r:r + half])
            levels.append(jnp.concatenate(parts, axis=0))
        for i in range(n_lev - len(coarse)):
            levels.append(sums[(1 + i) * length:(2 + i) * length])
        e_in = jnp.exp2(b)
        e_out = jnp.exp2(b[length - 1:length] - b)
        return e_in, e_out, [jnp.exp2(x) for x in levels]

    def level_operands(rows, g, e_in, e_out, e_levels):
        cols = slice(g * HG_DK, (g + 1) * HG_DK)
        qs = [q_ref[rows, cols]]
        ks = [k_ref[rows, cols]]
        q = qs[0].astype(F32)
        k = ks[0].astype(F32)
        for half, e in zip(halves, e_levels):
            if half % 8 == 0:
                u = (mixed_rows(q, k, half) * e).astype(BF16)
                qs.append(u)
                ks.append(u)
            else:
                qs.append((q * e).astype(BF16))
                ks.append((k * e).astype(BF16))
        return qs, ks, (q * e_in).astype(BF16), (k * e_out).astype(BF16)

    def level_scores(qs, ks):
        if pack_levels:
            out = []
            for p in range(0, n_lev + 1, 2):
                lhs = jnp.concatenate([qs[p], qs[p + 1]], axis=1)
                rhs = jnp.concatenate([jnp.concatenate([ks[p], zeros_k], axis=1),
                                       jnp.concatenate([zeros_k, ks[p + 1]], axis=1)], axis=0)
                a = _dot_nt(lhs, rhs)
                out += [a[:, :length], a[:, length:]]
            return out
        return [_dot_nt(qs[p], ks[p]) for p in range(n_lev + 1)]

    def chunk(c, _):
        rows = pl.ds(pl.multiple_of(c * length, length), length)
        e_heads = []
        for pair in range(group // 2):
            cols2 = slice(2 * pair * HG_DK, (2 * pair + 2) * HG_DK)
            lf2 = lf_ref[rows, cols2] * LOG2_E
            hi = lf2.astype(BF16)
            mid = (lf2 - hi.astype(F32)).astype(BF16)
            e_in, e_out, e_levels = exponentials(_dot(summat, jnp.concatenate([hi, mid], axis=0)))
            for lanes in (slice(0, HG_DK), slice(HG_DK, 2 * HG_DK)):
                e_heads.append((e_in[:, lanes], e_out[:, lanes], [e[:, lanes] for e in e_levels]))
        work = []
        for g in range(group):
            qs, ks, q_in, k_out = level_operands(rows, g, *e_heads[g])
            st = st_scr[g]
            work.append((level_scores(qs, ks), _dot_nt(q_in, st.astype(BF16)), k_out, st))
        for g in range(group):
            cols = slice(g * HG_DK, (g + 1) * HG_DK)
            scores, o_in, k_out, st = work[g]
            v = v_ref[rows, cols]
            att = mask_ref[0] * scores[0]
            for p in range(1, n_lev + 1):
                att = att + mask_ref[p] * scores[p]
            o = o_in + _dot(att.astype(BF16), v)
            e_last = e_heads[g][0][length - 1:length, :]
            st_scr[g] = st * e_last + _dot_tn(v, k_out)
            o_ref[rows, cols] = (_rms(o, gn) * g_ref[rows, cols]).astype(o_ref.dtype)
        return 0

    lax.fori_loop(0, n_chunks, chunk, 0)

    @pl.when(tb == pl.num_programs(2) - 1)
    def _():
        for g in range(group):
            sout_ref[g] = st_scr[g].T


def _hgrn(q, k, logf, v, gate, gn, state0, *, layer, row_block0, n_streams, stream_len, length, block_len, group):
    n_tb = stream_len // block_len
    n_chunks = block_len // length
    width = group * HG_DK
    summat, masks = _hgrn_tables(length)
    summat = jnp.asarray(np.concatenate([summat, summat], axis=1), BF16)
    masks = jnp.asarray(masks, F32)
    tok = lambda col0: pl.BlockSpec((block_len, width),
                                    lambda n, h, t: (row_block0 + n * n_tb + t, col0 // width + h))
    tokens = (q, k, logf, v, gate)
    st_spec = pl.BlockSpec((None, group, HG_DK, HG_DV), lambda n, h, t: (n, h, 0, 0))
    in_specs = [tok(col0) for _, col0 in tokens] + [pl.BlockSpec((1, HG_DV), lambda n, h, t: (0, 0)),
                pl.BlockSpec(summat.shape, lambda n, h, t: (0, 0)),
                pl.BlockSpec(masks.shape, lambda n, h, t: (0, 0, 0))]
    args = [arr for arr, _ in tokens] + [gn.reshape(1, HG_DV), summat, masks]
    if state0 is not None:
        in_specs.append(pl.BlockSpec((None, None, group, HG_DK, HG_DV), lambda n, h, t: (layer, n, h, 0, 0)))
        args.append(state0)
    return pl.pallas_call(
        functools.partial(_hgrn_body, length=length, n_chunks=n_chunks, group=group,
                          has_state=state0 is not None),
        grid=(n_streams, HG_HEADS // group, n_tb),
        in_specs=in_specs,
        out_specs=[pl.BlockSpec((block_len, width), lambda n, h, t: (n * n_tb + t, h)), st_spec],
        out_shape=[jax.ShapeDtypeStruct((n_streams * stream_len, HG_HEADS * HG_DV), BF16),
                   jax.ShapeDtypeStruct((n_streams, HG_HEADS, HG_DK, HG_DV), F32)],
        scratch_shapes=[pltpu.VMEM((group, HG_DV, HG_DK), F32)],
        compiler_params=_params("parallel", "parallel", "arbitrary"),
        name="hgrn_state" if state0 is not None else "hgrn_prompt",
    )(*args)


def _merge_body(*refs, tm, n_a):
    oa_parts, oh_parts = refs[:n_a], refs[n_a:-5]
    woa_ref, wob_ref, ga_ref, gb_ref, o_ref = refs[-5:]

    def emit(tiles):
        a = _dot(tiles[0], woa_ref[...])
        b = _dot(tiles[1], wob_ref[...])
        o_ref[...] = (ga_ref[...] * a + gb_ref[...] * b).astype(o_ref.dtype)

    _on_row_tiles([oa_parts, oh_parts], 1, tm, emit)


def _merge(oa_parts, oh_parts, woa, wob, ga, gb, layer, tm, tn):
    rows, k = _stacked_rows(oa_parts), oa_parts[0].shape[1]
    n = woa.shape[2]
    x_specs = lambda parts: _part_specs(parts, tm, lambda j, i: (i, 0), lambda j, i: (0, 0))
    w_spec = pl.BlockSpec((None, k, tn), lambda j, i: (layer, 0, j))
    t_spec = pl.BlockSpec((tm, tn), lambda j, i: (i, j))
    return pl.pallas_call(
        functools.partial(_merge_body, tm=tm, n_a=len(oa_parts)),
        grid=(n // tn, rows // tm),
        in_specs=x_specs(oa_parts) + x_specs(oh_parts) + [w_spec, w_spec]
        + [pl.BlockSpec((tm, tn), lambda j, i, c=col0 // tn: (i, c + j)) for _, col0 in (ga, gb)],
        out_specs=t_spec,
        out_shape=jax.ShapeDtypeStruct((rows, n), BF16),
        compiler_params=_params("parallel", "arbitrary"),
        name="gated_merge",
    )(*oa_parts, *oh_parts, woa, wob, ga[0], gb[0])


def _outproj_body(m_ref, w_ref, *refs, tm):
    x_parts, (g1_ref, g2_ref, x1_ref, h2_ref) = refs[:-4], refs[-4:]

    def emit(tiles):
        y = _dot(m_ref[...], w_ref[...])
        x1 = tiles[0] + _rms(y, g1_ref[...])
        x1_ref[...] = x1
        h2_ref[...] = _rms(x1, g2_ref[...]).astype(h2_ref.dtype)

    _on_row_tiles([x_parts], 0, tm, emit)


def _outproj(mix, w, x_parts, g_post, g_pre2, layer, tm):
    rows, d = mix.shape
    row = pl.BlockSpec((tm, d), lambda i: (i, 0))
    vec = pl.BlockSpec((1, d), lambda i: (0, 0))
    return pl.pallas_call(
        functools.partial(_outproj_body, tm=tm),
        grid=(rows // tm,),
        in_specs=[row, pl.BlockSpec((None, d, d), lambda i: (layer, 0, 0))]
        + _part_specs(x_parts, tm, lambda i: (i, 0), lambda i: (0, 0)) + [vec, vec],
        out_specs=[row, row],
        out_shape=[jax.ShapeDtypeStruct((rows, d), F32), jax.ShapeDtypeStruct((rows, d), BF16)],
        compiler_params=_params("arbitrary"),
        name="out_proj",
    )(mix, w, *x_parts, g_post.reshape(1, d), g_pre2.reshape(1, d))


def _mlp_body(h_ref, wu_ref, wd_ref, x_ref, g_ref, *rest, tail_start, norm_next):
    rest = list(rest)
    gn_ref = rest.pop(0) if norm_next else None
    o_ref = rest.pop(0)
    tail_ref = rest.pop(0) if tail_start is not None else None
    hn_ref = rest.pop(0) if norm_next else None
    f = pl.program_id(1)
    u = jnp.maximum(_dot(h_ref[...], wu_ref[...]), 0.0)
    part = _dot((u * u).astype(BF16), wd_ref[...])

    @pl.when(f == 0)
    def _():
        o_ref[...] = part

    @pl.when(f > 0)
    def _():
        o_ref[...] += part

    @pl.when(f == pl.num_programs(1) - 1)
    def _():
        y = x_ref[...] + _rms(o_ref[...], g_ref[...])
        o_ref[...] = y
        if norm_next:
            hn_ref[...] = _rms(y, gn_ref[...]).astype(hn_ref.dtype)
        if tail_start is not None:
            @pl.when(pl.program_id(0) == pl.num_programs(0) - 1)
            def _():
                tail_ref[...] = y[tail_start:, :]


def _mlp(h2, w_up, w_down, x1, g, layer, tm, tf, split_rows=None, next_g=None):
    rows, d = x1.shape
    ff = w_up.shape[2]
    row = pl.BlockSpec((tm, d), lambda i, f: (i, 0))
    vec = pl.BlockSpec((1, d), lambda i, f: (0, 0))
    out_specs, out_shape, tail_start = [row], [jax.ShapeDtypeStruct((rows, d), F32)], None
    if split_rows is not None:
        tail = rows - split_rows
        tail_start = tm - tail
        assert 0 <= tail_start and tail % 8 == 0
        out_specs.append(pl.BlockSpec((tail, d), lambda i, f: (0, 0)))
        out_shape = [jax.ShapeDtypeStruct((split_rows, d), F32), jax.ShapeDtypeStruct((tail, d), F32)]
    in_specs = [row, pl.BlockSpec((None, d, tf), lambda i, f: (layer, 0, f)),
                pl.BlockSpec((None, tf, d), lambda i, f: (layer, f, 0)), row, vec]
    args = [h2, w_up, w_down, x1, g.reshape(1, d)]
    if next_g is not None:
        in_specs.append(vec)
        args.append(next_g.reshape(1, d))
        out_specs.append(row)
        out_shape.append(jax.ShapeDtypeStruct((rows, d), BF16))
    return pl.pallas_call(
        functools.partial(_mlp_body, tail_start=tail_start, norm_next=next_g is not None),
        grid=(rows // tm, ff // tf),
        in_specs=in_specs,
        out_specs=out_specs,
        out_shape=out_shape,
        compiler_params=_params("arbitrary", "arbitrary"),
        name="mlp",
    )(*args)


def _pad_lanes(w):
    return jnp.concatenate([w, jnp.zeros(w.shape[:-1] + (LANES - w.shape[-1],), w.dtype)], axis=-1)


def _swap_halves(w):
    half = w.shape[-1] // 2
    return jnp.concatenate([w[..., half:], w[..., :half]], axis=-1)


def _prep_weights(w_uq, w_uk, w_uv):
    depth = w_uq.shape[0]
    uq = w_uq.reshape(depth, Q_LORA, MLA_HEADS // 2, 2, QK_HEAD)
    nope = uq[..., :QK_NOPE]
    pe = _pad_lanes(uq[..., QK_NOPE:])
    pe_sw = _pad_lanes(_swap_halves(uq[..., QK_NOPE:]))
    w_q = jnp.concatenate([nope[:, :, :, 0], pe[:, :, :, 0], nope[:, :, :, 1], pe[:, :, :, 1],
                           pe_sw[:, :, :, 0], pe_sw[:, :, :, 1]], axis=-1)
    w_q = w_q.reshape(depth, Q_LORA, -1).astype(BF16)
    uk = w_uk.reshape(depth, KV_LORA, MLA_HEADS // 2, 2 * QK_NOPE)
    uv = w_uv.reshape(depth, KV_LORA, MLA_HEADS // 2, 2 * V_HEAD)
    w_kv = jnp.concatenate([uk, uv], axis=-1).reshape(depth, KV_LORA, -1).astype(BF16)
    wuk_t = jnp.transpose(w_uk, (0, 2, 3, 1)).astype(BF16)
    wuv_h = jnp.transpose(w_uv, (0, 2, 1, 3)).astype(BF16)
    return w_q, w_kv, wuk_t, wuv_h


def _rope_tables(positions):
    half = QK_ROPE // 2
    inv = ROPE_THETA ** (-jnp.arange(half, dtype=F32) / half)
    ang = positions.astype(F32)[:, None] * inv[None, :]
    cos, sin = jnp.cos(ang), jnp.sin(ang)
    zeros = jnp.zeros((positions.shape[0], LANES - QK_ROPE), F32)
    return (jnp.concatenate([cos, cos, zeros], axis=1), jnp.concatenate([-sin, sin, zeros], axis=1))


def kernel(x_prompt, x_sample, cache_ckv, cache_kpe, state_hgrn, pre_mix_g, w_in, q_norm_g, w_uq, kv_norm_g,
           w_uk, w_uv, w_oa, hg_lb, hg_norm_g, w_ob, w_out, post_mix_g, pre_mlp_g, w_up, w_down, post_mlp_g):
    batch, seq, d = x_prompt.shape
    dec_batch, dec_seq, _ = x_sample.shape
    depth = w_in.shape[0]
    past = cache_ckv.shape[2]
    n_p = batch * seq
    n_s = dec_batch * dec_seq
    rows = n_p + n_s
    tm = 768
    tm_wide = 1408
    assert rows % tm == 0 and rows % tm_wide == 0 and n_p % 1024 == 0

    x_parts = (x_prompt.reshape(n_p, d), x_sample.reshape(n_s, d))
    pos = jnp.concatenate([jnp.tile(jnp.arange(seq, dtype=jnp.int32), batch),
                           jnp.tile(past + jnp.arange(dec_seq, dtype=jnp.int32), dec_batch)])
    cos_t, sin_t = _rope_tables(pos)
    cos_q, sin_q = cos_t * Q_SCALE, sin_t * Q_SCALE

    w_in_t = jnp.transpose(w_in, (0, 2, 1))
    cache_kpe_t = jnp.transpose(cache_kpe, (0, 1, 3, 2))
    wide0 = Q_LORA + KV_LORA + QK_ROPE
    w_q, w_kv, wuk_t, wuv_h = _prep_weights(w_uq, w_uk, w_uv)
    w_oa_b, w_ob_b, w_out_b = w_oa.astype(BF16), w_ob.astype(BF16), w_out.astype(BF16)
    w_up_b, w_down_b = w_up.astype(BF16), w_down.astype(BF16)

    ckv_out, kpe_out, st_p_out, st_s_out = [], [], [], []
    for l in range(depth):
        if l == 0:
            h = _rmsnorm(x_parts, pre_mix_g[l], tm)
        qn, ckv, ckv_b, krot, krot_b = _latent(h, w_in_t, q_norm_g[l], kv_norm_g[l], cos_t, sin_t, l, tm)
        hk, logf = _mm(h, w_in_t, [functools.partial(_epi_forget, layer=l)], [(1024, BF16), (1024, F32)],
                       layer=l, row0s=[wide0 + d], n=d, tm=tm, tn=1024, col_args=[(hg_lb, 1024)], name="proj_hf")
        hq, hv, hgate, ga, gb = [
            (*_mm(h, w_in_t, [epi], [(1024, BF16)], layer=l, row0s=[wide0 + g * d], n=d, tm=tm_wide, tn=1024,
                  name=name), 0)
            for g, epi, name in ((0, _epi_silu, "proj_hq"), (2, _epi_id, "proj_hi"), (3, _epi_silu, "proj_hg"),
                                 (4, _epi_sigmoid, "proj_ga"), (5, _epi_sigmoid, "proj_gb"))]
        hk, logf = [hk, 0], [logf, 0]
        (q,) = _mm_rows(qn, w_q, _epi_q, [(2 * QK_PAD, BF16)], layer=l, tm=tm, tn=6 * LANES,
                        tile_args=[(cos_q, LANES), (sin_q, LANES)], name="proj_q")
        keys, vals = _mm_rows(ckv_b, w_kv, _epi_kv, [(2 * QK_PAD, BF16), (2 * V_HEAD, BF16)], layer=l, tm=1024,
                              tn=4 * LANES, rows=n_p, tile_args=[(krot_b, LANES)], name="proj_kv")
        oa_p = _prompt_attention(q, keys, vals, batch, seq, 256)
        oa_s = _sample_attention(q, ckv_b, krot_b, cache_ckv, cache_kpe_t, wuk_t, wuv_h, l, n_p, dec_batch, dec_seq)
        oh_p, st_p = _hgrn(hq, hk, logf, hv, hgate, hg_norm_g[l], None, layer=l,
                           row_block0=0, n_streams=batch, stream_len=seq, length=2 * CHUNK,
                           block_len=512, group=8)
        oh_s, st_s = _hgrn(hq, hk, logf, hv, hgate, hg_norm_g[l], state_hgrn, layer=l,
                           row_block0=n_p // dec_seq, n_streams=dec_batch, stream_len=dec_seq, length=dec_seq,
                           block_len=dec_seq, group=4)
        mix = _merge((oa_p, oa_s), (oh_p, oh_s), w_oa_b, w_ob_b, ga, gb, l, tm, 1024)
        x1, h2 = _outproj(mix, w_out_b, x_parts, post_mix_g[l], pre_mlp_g[l], l, 384)
        if l + 1 < depth:
            x, h = _mlp(h2, w_up_b, w_down_b, x1, post_mlp_g[l], l, tm, 512, next_g=pre_mix_g[l + 1])
            x_parts = (x,)
        else:
            y_p, y_s = _mlp(h2, w_up_b, w_down_b, x1, post_mlp_g[l], l, tm, 512, split_rows=n_p)

        ckv_out.append(ckv)
        kpe_out.append(krot)
        st_p_out.append(st_p)
        st_s_out.append(st_s)

    ckv_all = jnp.stack(ckv_out)
    kpe_all = jnp.stack(kpe_out)
    return (y_p.reshape(batch, seq, d),
            y_s.reshape(dec_batch, dec_seq, d),
            ckv_all[:, :n_p].reshape(depth, batch, seq, KV_LORA),
            kpe_all[:, :n_p].reshape(depth, batch, seq, QK_ROPE),
            jnp.stack(st_p_out),
            ckv_all[:, n_p:].reshape(depth, dec_batch, dec_seq, KV_LORA),
            kpe_all[:, n_p:].reshape(depth, dec_batch, dec_seq, QK_ROPE),
            jnp.stack(st_s_out))
```

```python
import functools

import jax
import jax.numpy as jnp
import numpy as np
from jax import lax
from jax.experimental import pallas as pl
from jax.experimental.pallas import tpu as pltpu

F32 = jnp.float32
BF16 = jnp.bfloat16

CHUNK = 64
MLA_HEADS = 16
QK_NOPE = 128
QK_ROPE = 64
QK_HEAD = QK_NOPE + QK_ROPE
V_HEAD = 128
Q_LORA = 512
KV_LORA = 512
ROPE_THETA = 10000.0
ATTN_SCALE = QK_HEAD ** -0.5
HG_HEADS = 16
HG_DK = 128
HG_DV = 128
EPS = 1e-6

LANES = 128
QK_PAD = 2 * LANES
NEG_BIG = -1e30
LOG2_E = 1.4426950408889634
Q_SCALE = ATTN_SCALE * LOG2_E
VMEM_LIMIT = 58 * 1024 * 1024


def _params(*sem):
    return pltpu.CompilerParams(dimension_semantics=sem, vmem_limit_bytes=VMEM_LIMIT)


def _rms(x, g):
    return x * lax.rsqrt(jnp.mean(x * x, axis=-1, keepdims=True) + EPS) * g


def _dot(a, b):
    return jnp.dot(a, b, preferred_element_type=F32)


def _dot_nt(a, b):
    return lax.dot_general(a, b, (((1,), (1,)), ((), ())), preferred_element_type=F32)


def _dot_tn(a, b):
    return lax.dot_general(a, b, (((0,), (0,)), ((), ())), preferred_element_type=F32)


def _part_specs(parts, tm, index_map, tail_index_map):
    specs = [pl.BlockSpec((tm, parts[0].shape[1]), index_map)]
    if len(parts) == 2:
        specs.append(pl.BlockSpec(parts[1].shape, tail_index_map))
    return specs


def _stacked_rows(parts):
    return sum(p.shape[0] for p in parts)


def _on_row_tiles(operands, row_axis, tm, fn):
    if all(len(o) == 1 for o in operands):
        fn([o[0][...] for o in operands])
        return
    is_last = pl.program_id(row_axis) == pl.num_programs(row_axis) - 1

    @pl.when(jnp.logical_not(is_last))
    def _():
        fn([o[0][...] for o in operands])

    @pl.when(is_last)
    def _():
        fn([o[0][...] if len(o) == 1 else
            jnp.concatenate([o[0][:tm - o[1].shape[0]], o[1][...]], axis=0) for o in operands])


def _rmsnorm_body(*refs, tm):
    x_parts, (g_ref, o_ref) = refs[:-2], refs[-2:]

    def emit(tiles):
        o_ref[...] = _rms(tiles[0], g_ref[...]).astype(o_ref.dtype)

    _on_row_tiles([x_parts], 0, tm, emit)


def _rmsnorm(x_parts, g, tm):
    rows, d = _stacked_rows(x_parts), x_parts[0].shape[1]
    return pl.pallas_call(
        functools.partial(_rmsnorm_body, tm=tm),
        grid=(rows // tm,),
        in_specs=_part_specs(x_parts, tm, lambda i: (i, 0), lambda i: (0, 0))
        + [pl.BlockSpec((1, d), lambda i: (0, 0))],
        out_specs=pl.BlockSpec((tm, d), lambda i: (i, 0)),
        out_shape=jax.ShapeDtypeStruct((rows, d), BF16),
        compiler_params=_params("arbitrary"),
        name="rmsnorm",
    )(*x_parts, g.reshape(1, d))


def _mm_body(x_ref, w_ref, *rest, epis, tiles_per_group, n_extra):
    extra = rest[:n_extra]
    outs = rest[n_extra:-1]
    w_scr = rest[-1]

    @pl.when(pl.program_id(1) == 0)
    def _():
        w_scr[...] = w_ref[0].astype(w_scr.dtype)

    acc = _dot_nt(x_ref[...], w_scr[...])

    def emit(epi):
        res = epi(acc, *[e[...] for e in extra])
        for o_ref, r in zip(outs, res):
            o_ref[...] = r.astype(o_ref.dtype)

    if len(epis) == 1:
        emit(epis[0])
    else:
        group = pl.program_id(0) // tiles_per_group
        for g, epi in enumerate(epis):
            pl.when(group == g)(functools.partial(emit, epi))


def _mm(x, w_t, epis, outs, *, layer, row0s, n, tm, tn, col_args=(), tile_args=(), name):
    rows, k = x.shape
    assert all(r % 8 == 0 for r in row0s) and tn % 8 == 0 and len(epis) == len(row0s)
    tpg = n // tn
    n_tiles = tpg * len(row0s)

    def slab_row(j):
        row0 = row0s[0]
        for g in range(1, len(row0s)):
            row0 = jnp.where(j // tpg >= g, row0s[g], row0)
        return pl.multiple_of(row0 + (j % tpg) * tn, 8)

    grid = (n_tiles, rows // tm)
    in_specs = [pl.BlockSpec((tm, k), lambda j, i: (i, 0)),
                pl.BlockSpec((pl.Element(1), pl.Element(tn), pl.Element(k)),
                             lambda j, i: (layer, slab_row(j), 0))]
    args = [x, w_t]
    for arr, width in col_args:
        in_specs.append(pl.BlockSpec((arr.shape[0], width), lambda j, i: (0, j)))
        args.append(arr)
    for arr, width, follows_n in tile_args:
        if follows_n:
            in_specs.append(pl.BlockSpec((tm, width), lambda j, i: (i, j)))
        else:
            in_specs.append(pl.BlockSpec((tm, width), lambda j, i: (i, 0)))
        args.append(arr)
    out_specs = [pl.BlockSpec((tm, width), lambda j, i: (i, j)) for width, _ in outs]
    out_shape = [jax.ShapeDtypeStruct((rows, width * n_tiles), dt) for width, dt in outs]
    res = pl.pallas_call(
        functools.partial(_mm_body, epis=epis, tiles_per_group=tpg, n_extra=len(col_args) + len(tile_args)),
        grid=grid,
        in_specs=in_specs,
        out_specs=out_specs,
        out_shape=out_shape,
        scratch_shapes=[pltpu.VMEM((tn, k), BF16)],
        compiler_params=_params("parallel", "arbitrary"),
        name=name,
    )(*args)
    return res


def _mm_rows_body(x_ref, w_ref, *rest, epi, n_extra, tn, widths):
    extra = [e[...] for e in rest[:n_extra]]
    outs = rest[n_extra:]
    x = x_ref[...]
    for p in range(w_ref.shape[1] // tn):
        res = epi(_dot(x, w_ref[:, p * tn:(p + 1) * tn]), *extra)
        for o_ref, r, width in zip(outs, res, widths):
            o_ref[:, p * width:(p + 1) * width] = r.astype(o_ref.dtype)


def _mm_rows(x, w, epi, outs, *, layer, tm, tn, rows=None, tile_args=(), name):
    k = x.shape[1]
    rows = x.shape[0] if rows is None else rows
    n = w.shape[2]
    in_specs = [pl.BlockSpec((tm, k), lambda i: (i, 0)),
                pl.BlockSpec((None, k, n), lambda i: (layer, 0, 0))]
    in_specs += [pl.BlockSpec((tm, width), lambda i: (i, 0)) for _, width in tile_args]
    return pl.pallas_call(
        functools.partial(_mm_rows_body, epi=epi, n_extra=len(tile_args), tn=tn,
                          widths=[width for width, _ in outs]),
        grid=(rows // tm,),
        in_specs=in_specs,
        out_specs=[pl.BlockSpec((tm, width * (n // tn)), lambda i: (i, 0)) for width, _ in outs],
        out_shape=[jax.ShapeDtypeStruct((rows, width * (n // tn)), dt) for width, dt in outs],
        compiler_params=_params("parallel"),
        name=name,
    )(x, w, *[arr for arr, _ in tile_args])


def _silu(z):
    return z * jax.nn.sigmoid(z)


def _epi_silu(acc):
    return (_silu(acc),)


def _epi_sigmoid(acc):
    return (jax.nn.sigmoid(acc),)


def _epi_id(acc):
    return (acc,)


def _epi_relu2(acc):
    r = jnp.maximum(acc, 0.0)
    return (r * r,)


def _epi_forget(acc, lb_logits, *, layer):
    mx = jnp.max(lb_logits, axis=0, keepdims=True)
    e = jnp.exp(lb_logits - mx)
    sm = e / jnp.sum(e, axis=0, keepdims=True)
    lb = jnp.zeros_like(mx)
    for i in range(1, layer + 1):
        lb = lb + sm[i:i + 1]
    z = acc
    e = jnp.exp(-jnp.abs(z))
    r = 1.0 / (1.0 + e)
    sig_neg = jnp.where(z >= 0.0, e * r, r)
    log_sig = jnp.minimum(z, 0.0) - jnp.log(1.0 + e)
    a = jnp.log(lb)
    c = jnp.log1p(-lb) + log_sig
    hi = jnp.maximum(a, c)
    lo = jnp.minimum(a, c)
    logf = hi + jnp.log(1.0 + jnp.exp(lo - hi))
    kk = (1.0 - lb) * sig_neg
    return kk, logf


def _epi_q(acc, cos_t, sin_t):
    n0 = acc[:, 0 * LANES:1 * LANES] * Q_SCALE
    p0 = acc[:, 1 * LANES:2 * LANES] * cos_t + acc[:, 4 * LANES:5 * LANES] * sin_t
    n1 = acc[:, 2 * LANES:3 * LANES] * Q_SCALE
    p1 = acc[:, 3 * LANES:4 * LANES] * cos_t + acc[:, 5 * LANES:6 * LANES] * sin_t
    return (jnp.concatenate([n0, p0, n1, p1], axis=1),)


def _epi_kv(acc, krot):
    krot = krot.astype(F32)
    keys = jnp.concatenate([acc[:, 0:LANES], krot, acc[:, LANES:2 * LANES], krot], axis=1)
    return keys, acc[:, 2 * LANES:4 * LANES]


def _latent_body(h_ref, w_ref, gq_ref, gkv_ref, cos_ref, sin_ref,
                 qn_ref, ckv_ref, ckvb_ref, krot_ref, krotb_ref, w_scr):
    base = Q_LORA + KV_LORA
    half = QK_ROPE // 2

    @pl.when(pl.program_id(0) == 0)
    def _():
        w_scr[...] = jnp.zeros_like(w_scr)
        w_scr[0:base + QK_ROPE, :] = w_ref[...].astype(w_scr.dtype)
        w_scr[base + LANES:base + LANES + half, :] = w_ref[base + half:base + QK_ROPE, :].astype(w_scr.dtype)
        w_scr[base + LANES + half:base + LANES + QK_ROPE, :] = w_ref[base:base + half, :].astype(w_scr.dtype)

    acc = _dot_nt(h_ref[...], w_scr[...])
    qn_ref[...] = _rms(acc[:, :Q_LORA], gq_ref[...]).astype(qn_ref.dtype)
    ckv = _rms(acc[:, Q_LORA:base], gkv_ref[...])
    ckv_ref[...] = ckv
    ckvb_ref[...] = ckv.astype(ckvb_ref.dtype)
    kr = acc[:, base:base + LANES] * cos_ref[...] + acc[:, base + LANES:base + 2 * LANES] * sin_ref[...]
    krot_ref[...] = kr[:, :QK_ROPE]
    krotb_ref[...] = kr.astype(krotb_ref.dtype)


def _latent(h, w_t, gq, gkv, cos_t, sin_t, layer, tm):
    rows, d = h.shape
    n_in = Q_LORA + KV_LORA + QK_ROPE
    row = lambda width: pl.BlockSpec((tm, width), lambda i: (i, 0))
    const = lambda r, width: pl.BlockSpec((r, width), lambda i: (0, 0))
    return pl.pallas_call(
        _latent_body,
        grid=(rows // tm,),
        in_specs=[row(d), pl.BlockSpec((None, n_in, d), lambda i: (layer, 0, 0)),
                  const(1, Q_LORA), const(1, KV_LORA), row(LANES), row(LANES)],
        out_specs=[row(Q_LORA), row(KV_LORA), row(KV_LORA), row(QK_ROPE), row(LANES)],
        out_shape=[jax.ShapeDtypeStruct((rows, Q_LORA), BF16),
                   jax.ShapeDtypeStruct((rows, KV_LORA), F32),
                   jax.ShapeDtypeStruct((rows, KV_LORA), BF16),
                   jax.ShapeDtypeStruct((rows, QK_ROPE), F32),
                   jax.ShapeDtypeStruct((rows, LANES), BF16)],
        scratch_shapes=[pltpu.VMEM((Q_LORA + KV_LORA + 2 * LANES, d), BF16)],
        compiler_params=_params("arbitrary"),
        name="latent_proj",
    )(h, w_t, gq.reshape(1, -1), gkv.reshape(1, -1), cos_t, sin_t)


def _attn_body(q_ref, k_ref, v_ref, o_ref, *, tq, nq):
    k_chunk = lax.broadcasted_iota(jnp.int32, (tq, tq), 0) // CHUNK
    q_chunk = lax.broadcasted_iota(jnp.int32, (tq, tq), 1) // CHUNK
    visible = k_chunk <= q_chunk
    v_t = v_ref[...].T

    scores = []
    for n_past in range(nq):
        lo = n_past * tq
        q = q_ref[lo:lo + tq, :]
        s_d = jnp.where(visible, _dot_nt(k_ref[lo:lo + tq, :], q), NEG_BIG)
        s_p = _dot_nt(k_ref[0:lo, :], q) if n_past else None
        scores.append((s_d, s_p))
    probs = []
    for s_d, s_p in scores:
        m = jnp.max(s_d, axis=0, keepdims=True)
        if s_p is not None:
            m = jnp.maximum(m, jnp.max(s_p, axis=0, keepdims=True))
        p_d = jnp.exp2(s_d - m)
        l = jnp.sum(p_d, axis=0, keepdims=True)
        p_p = None
        if s_p is not None:
            p_p = jnp.exp2(s_p - m)
            l = l + jnp.sum(p_p, axis=0, keepdims=True)
            p_p = p_p.astype(BF16)
        probs.append((p_d.astype(BF16), p_p, l))
    for n_past, (p_d, p_p, l) in enumerate(probs):
        lo = n_past * tq
        acc = _dot(v_t[:, lo:lo + tq], p_d)
        if p_p is not None:
            acc = acc + _dot(v_t[:, 0:lo], p_p)
        o_ref[lo:lo + tq, :] = (acc / l).T.astype(o_ref.dtype)


def _prompt_attention(q, k, v, batch, seq, tq):
    return pl.pallas_call(
        functools.partial(_attn_body, tq=tq, nq=seq // tq),
        grid=(batch, MLA_HEADS),
        in_specs=[pl.BlockSpec((seq, QK_PAD), lambda b, h: (b, h)),
                  pl.BlockSpec((seq, QK_PAD), lambda b, h: (b, h)),
                  pl.BlockSpec((seq, V_HEAD), lambda b, h: (b, h))],
        out_specs=pl.BlockSpec((seq, V_HEAD), lambda b, h: (b, h)),
        out_shape=jax.ShapeDtypeStruct((batch * seq, MLA_HEADS * V_HEAD), BF16),
        compiler_params=_params("parallel", "parallel"),
        name="prompt_attention",
    )(q, k, v)


def _sattn_body(q_ref, cn_ref, kn_ref, cc_ref, ck_ref, wuk_ref, wuv_ref, o_ref, qa_scr, qp_scr,
                *, dec_seq, key_tile):
    for h in range(MLA_HEADS):
        qn = q_ref[:, h * QK_PAD:h * QK_PAD + QK_NOPE]
        qa_scr[h * dec_seq:(h + 1) * dec_seq, :] = _dot(qn, wuk_ref[h]).astype(BF16)
        qp_scr[h * dec_seq:(h + 1) * dec_seq, :] = q_ref[:, h * QK_PAD + QK_NOPE:(h + 1) * QK_PAD]
    qa = qa_scr[...]
    qp = qp_scr[...]
    rows = MLA_HEADS * dec_seq

    def update(carry, s, c):
        m, l, acc = carry
        m_new = jnp.maximum(m, jnp.max(s, axis=-1, keepdims=True))
        alpha = jnp.exp2(m - m_new)
        p = jnp.exp2(s - m_new)
        l = alpha * l + jnp.sum(p, axis=-1, keepdims=True)
        acc = alpha * acc + _dot(p.astype(BF16), c)
        return m_new, l, acc

    carry = (jnp.full((rows, 1), NEG_BIG, F32), jnp.zeros((rows, 1), F32), jnp.zeros((rows, KV_LORA), F32))
    past = cc_ref.shape[0]
    for t in range(past // key_tile):
        c = cc_ref[t * key_tile:(t + 1) * key_tile, :].astype(BF16)
        kp_t = ck_ref[:, t * key_tile:(t + 1) * key_tile].astype(BF16)
        s = _dot_nt(qa, c) + _dot(qp[:, :QK_ROPE], kp_t)
        carry = update(carry, s, c)
    cn = cn_ref[...]
    s = _dot_nt(qa, cn) + _dot_nt(qp, kn_ref[...])
    _, l, acc = update(carry, s, cn)
    o_lat = (acc / l).astype(BF16)
    for h in range(MLA_HEADS):
        o_ref[:, h * V_HEAD:(h + 1) * V_HEAD] = _dot(
            o_lat[h * dec_seq:(h + 1) * dec_seq, :], wuv_ref[h]).astype(o_ref.dtype)


def _sample_attention(q, ckv_b, krot_b, cache_c, cache_k, wuk_t, wuv_h, layer, n_prompt_rows, dec_batch, dec_seq):
    rb = n_prompt_rows // dec_seq
    past = cache_c.shape[2]
    key_tile = min(past, 1024)
    per_layer = lambda s: pl.BlockSpec((None,) + s, lambda b: (layer, 0, 0, 0))
    return pl.pallas_call(
        functools.partial(_sattn_body, dec_seq=dec_seq, key_tile=key_tile),
        grid=(dec_batch,),
        in_specs=[pl.BlockSpec((dec_seq, MLA_HEADS * QK_PAD), lambda b: (rb + b, 0)),
                  pl.BlockSpec((dec_seq, KV_LORA), lambda b: (rb + b, 0)),
                  pl.BlockSpec((dec_seq, LANES), lambda b: (rb + b, 0)),
                  pl.BlockSpec((None, None, past, KV_LORA), lambda b: (layer, b, 0, 0)),
                  pl.BlockSpec((None, None, QK_ROPE, past), lambda b: (layer, b, 0, 0)),
                  per_layer((MLA_HEADS, QK_NOPE, KV_LORA)),
                  per_layer((MLA_HEADS, KV_LORA, V_HEAD))],
        out_specs=pl.BlockSpec((dec_seq, MLA_HEADS * V_HEAD), lambda b: (b, 0)),
        out_shape=jax.ShapeDtypeStruct((dec_batch * dec_seq, MLA_HEADS * V_HEAD), BF16),
        scratch_shapes=[pltpu.VMEM((MLA_HEADS * dec_seq, KV_LORA), BF16),
                        pltpu.VMEM((MLA_HEADS * dec_seq, LANES), BF16)],
        compiler_params=_params("parallel"),
        name="sample_attention",
    )(q, ckv_b, krot_b, cache_c, cache_k, wuk_t, wuv_h)


def _hgrn_tables(length):
    t = np.arange(length)[:, None]
    r = np.arange(length)[None, :]
    groups = [(r <= t)]
    masks = [(r == t)]
    m = length // 2
    while m >= 1:
        blk = t // (2 * m)
        start2 = blk * 2 * m + m
        second = (t % (2 * m)) >= m
        incl = second & (r >= start2) & (r <= t)
        excl = (~second) & (r > t) & (r < start2)
        if m % 8:
            groups.append(incl | excl)
        r_blk = r // (2 * m)
        r_first = (r % (2 * m)) < m
        masks.append(second & r_first & (r_blk == blk))
        m //= 2
    return (np.concatenate(groups, axis=0).astype(np.float32),
            np.stack(masks, axis=0).astype(np.float32))


def _hgrn_body(*refs, length, n_chunks, group, has_state):
    if has_state:
        q_ref, k_ref, lf_ref, v_ref, g_ref, gn_ref, sum_ref, mask_ref, s0_ref, o_ref, sout_ref, st_scr = refs
    else:
        q_ref, k_ref, lf_ref, v_ref, g_ref, gn_ref, sum_ref, mask_ref, o_ref, sout_ref, st_scr = refs
    tb = pl.program_id(2)

    @pl.when(tb == 0)
    def _():
        for g in range(group):
            st_scr[g] = s0_ref[g].T if has_state else jnp.zeros((HG_DV, HG_DK), F32)

    n_lev = mask_ref.shape[0] - 1
    summat = sum_ref[...]
    gn = gn_ref[...]

    pack_levels = length % LANES == 0 and (n_lev + 1) % 2 == 0
    zeros_k = jnp.zeros((length, HG_DK), BF16)

    def mixed_rows(q, k, half):
        parts = []
        for r in range(0, length, half):
            parts.append((q if (r // half) % 2 else k)[r:r + half])
        return jnp.concatenate(parts, axis=0)

    halves = [length >> (lev + 1) for lev in range(n_lev)]
    coarse = [half for half in halves if half % 8 == 0]

    def exponentials(sums):
        b = sums[0:length]
        levels = []
        for half in coarse:
            parts = []
            for r in range(0, length, half):
                ref = (r // (2 * half)) * 2 * half + half - 1
                ref_row = b[ref:ref + 1]
                parts.append(b[r:r + half] - ref_row if (r // half) % 2 else ref_row - b[r:r + half])
            levels.append(jnp.concatenate(parts, axis=0))
        for i in range(n_lev - len(coarse)):
            levels.append(sums[(1 + i) * length:(2 + i) * length])
        e_in = jnp.exp2(b)
        e_out = jnp.exp2(b[length - 1:length] - b)
        return e_in, e_out, [jnp.exp2(x) for x in levels]

    def level_operands(rows, g, e_in, e_out, e_levels):
        cols = slice(g * HG_DK, (g + 1) * HG_DK)
        qs = [q_ref[rows, cols]]
        ks = [k_ref[rows, cols]]
        q = qs[0].astype(F32)
        k = ks[0].astype(F32)
        for half, e in zip(halves, e_levels):
            if half % 8 == 0:
                u = (mixed_rows(q, k, half) * e).astype(BF16)
                qs.append(u)
                ks.append(u)
            else:
                qs.append((q * e).astype(BF16))
                ks.append((k * e).astype(BF16))
        return qs, ks, (q * e_in).astype(BF16), (k * e_out).astype(BF16)

    def level_scores(qs, ks):
        if pack_levels:
            out = []
            for p in range(0, n_lev + 1, 2):
                lhs = jnp.concatenate([qs[p], qs[p + 1]], axis=1)
                rhs = jnp.concatenate([jnp.concatenate([ks[p], zeros_k], axis=1),
                                       jnp.concatenate([zeros_k, ks[p + 1]], axis=1)], axis=0)
                a = _dot_nt(lhs, rhs)
                out += [a[:, :length], a[:, length:]]
            return out
        return [_dot_nt(qs[p], ks[p]) for p in range(n_lev + 1)]

    def chunk(c, _):
        rows = pl.ds(pl.multiple_of(c * length, length), length)
        e_heads = []
        for pair in range(group // 2):
            cols2 = slice(2 * pair * HG_DK, (2 * pair + 2) * HG_DK)
            lf2 = lf_ref[rows, cols2] * LOG2_E
            hi = lf2.astype(BF16)
            mid = (lf2 - hi.astype(F32)).astype(BF16)
            e_in, e_out, e_levels = exponentials(_dot(summat, jnp.concatenate([hi, mid], axis=0)))
            for lanes in (slice(0, HG_DK), slice(HG_DK, 2 * HG_DK)):
                e_heads.append((e_in[:, lanes], e_out[:, lanes], [e[:, lanes] for e in e_levels]))
        work = []
        for g in range(group):
            qs, ks, q_in, k_out = level_operands(rows, g, *e_heads[g])
            st = st_scr[g]
            work.append((level_scores(qs, ks), _dot_nt(q_in, st.astype(BF16)), k_out, st))
        for g in range(group):
            cols = slice(g * HG_DK, (g + 1) * HG_DK)
            scores, o_in, k_out, st = work[g]
            v = v_ref[rows, cols]
            att = mask_ref[0] * scores[0]
            for p in range(1, n_lev + 1):
                att = att + mask_ref[p] * scores[p]
            o = o_in + _dot(att.astype(BF16), v)
            e_last = e_heads[g][0][length - 1:length, :]
            st_scr[g] = st * e_last + _dot_tn(v, k_out)
            o_ref[rows, cols] = (_rms(o, gn) * g_ref[rows, cols]).astype(o_ref.dtype)
        return 0

    lax.fori_loop(0, n_chunks, chunk, 0)

    @pl.when(tb == pl.num_programs(2) - 1)
    def _():
        for g in range(group):
            sout_ref[g] = st_scr[g].T


def _hgrn(q, k, logf, v, gate, gn, state0, *, layer, row_block0, n_streams, stream_len, length, block_len, group):
    n_tb = stream_len // block_len
    n_chunks = block_len // length
    width = group * HG_DK
    summat, masks = _hgrn_tables(length)
    summat = jnp.asarray(np.concatenate([summat, summat], axis=1), BF16)
    masks = jnp.asarray(masks, F32)
    tok = lambda col0: pl.BlockSpec((block_len, width),
                                    lambda n, h, t: (row_block0 + n * n_tb + t, col0 // width + h))
    tokens = (q, k, logf, v, gate)
    st_spec = pl.BlockSpec((None, group, HG_DK, HG_DV), lambda n, h, t: (n, h, 0, 0))
    in_specs = [tok(col0) for _, col0 in tokens] + [pl.BlockSpec((1, HG_DV), lambda n, h, t: (0, 0)),
                pl.BlockSpec(summat.shape, lambda n, h, t: (0, 0)),
                pl.BlockSpec(masks.shape, lambda n, h, t: (0, 0, 0))]
    args = [arr for arr, _ in tokens] + [gn.reshape(1, HG_DV), summat, masks]
    if state0 is not None:
        in_specs.append(pl.BlockSpec((None, None, group, HG_DK, HG_DV), lambda n, h, t: (layer, n, h, 0, 0)))
        args.append(state0)
    return pl.pallas_call(
        functools.partial(_hgrn_body, length=length, n_chunks=n_chunks, group=group,
                          has_state=state0 is not None),
        grid=(n_streams, HG_HEADS // group, n_tb),
        in_specs=in_specs,
        out_specs=[pl.BlockSpec((block_len, width), lambda n, h, t: (n * n_tb + t, h)), st_spec],
        out_shape=[jax.ShapeDtypeStruct((n_streams * stream_len, HG_HEADS * HG_DV), BF16),
                   jax.ShapeDtypeStruct((n_streams, HG_HEADS, HG_DK, HG_DV), F32)],
        scratch_shapes=[pltpu.VMEM((group, HG_DV, HG_DK), F32)],
        compiler_params=_params("parallel", "parallel", "arbitrary"),
        name="hgrn_state" if state0 is not None else "hgrn_prompt",
    )(*args)


def _merge_body(*refs, tm, n_a):
    oa_parts, oh_parts = refs[:n_a], refs[n_a:-5]
    woa_ref, wob_ref, ga_ref, gb_ref, o_ref = refs[-5:]

    def emit(tiles):
        a = _dot(tiles[0], woa_ref[...])
        b = _dot(tiles[1], wob_ref[...])
        o_ref[...] = (ga_ref[...] * a + gb_ref[...] * b).astype(o_ref.dtype)

    _on_row_tiles([oa_parts, oh_parts], 1, tm, emit)


def _merge(oa_parts, oh_parts, woa, wob, ga, gb, layer, tm, tn):
    rows, k = _stacked_rows(oa_parts), oa_parts[0].shape[1]
    n = woa.shape[2]
    x_specs = lambda parts: _part_specs(parts, tm, lambda j, i: (i, 0), lambda j, i: (0, 0))
    w_spec = pl.BlockSpec((None, k, tn), lambda j, i: (layer, 0, j))
    t_spec = pl.BlockSpec((tm, tn), lambda j, i: (i, j))
    return pl.pallas_call(
        functools.partial(_merge_body, tm=tm, n_a=len(oa_parts)),
        grid=(n // tn, rows // tm),
        in_specs=x_specs(oa_parts) + x_specs(oh_parts) + [w_spec, w_spec]
        + [pl.BlockSpec((tm, tn), lambda j, i, c=col0 // tn: (i, c + j)) for _, col0 in (ga, gb)],
        out_specs=t_spec,
        out_shape=jax.ShapeDtypeStruct((rows, n), BF16),
        compiler_params=_params("parallel", "arbitrary"),
        name="gated_merge",
    )(*oa_parts, *oh_parts, woa, wob, ga[0], gb[0])


def _outproj_body(m_ref, w_ref, *refs, tm):
    x_parts, (g1_ref, g2_ref, x1_ref, h2_ref) = refs[:-4], refs[-4:]

    def emit(tiles):
        y = _dot(m_ref[...], w_ref[...])
        x1 = tiles[0] + _rms(y, g1_ref[...])
        x1_ref[...] = x1
        h2_ref[...] = _rms(x1, g2_ref[...]).astype(h2_ref.dtype)

    _on_row_tiles([x_parts], 0, tm, emit)


def _outproj(mix, w, x_parts, g_post, g_pre2, layer, tm):
    rows, d = mix.shape
    row = pl.BlockSpec((tm, d), lambda i: (i, 0))
    vec = pl.BlockSpec((1, d), lambda i: (0, 0))
    return pl.pallas_call(
        functools.partial(_outproj_body, tm=tm),
        grid=(rows // tm,),
        in_specs=[row, pl.BlockSpec((None, d, d), lambda i: (layer, 0, 0))]
        + _part_specs(x_parts, tm, lambda i: (i, 0), lambda i: (0, 0)) + [vec, vec],
        out_specs=[row, row],
        out_shape=[jax.ShapeDtypeStruct((rows, d), F32), jax.ShapeDtypeStruct((rows, d), BF16)],
        compiler_params=_params("arbitrary"),
        name="out_proj",
    )(mix, w, *x_parts, g_post.reshape(1, d), g_pre2.reshape(1, d))


def _mlp_body(h_ref, wu_ref, wd_ref, x_ref, g_ref, *rest, tail_start, norm_next):
    rest = list(rest)
    gn_ref = rest.pop(0) if norm_next else None
    o_ref = rest.pop(0)
    tail_ref = rest.pop(0) if tail_start is not None else None
    hn_ref = rest.pop(0) if norm_next else None
    acc_ref = rest.pop(0)
    f = pl.program_id(1)

    @pl.when(f == 0)
    def _():
        acc_ref[...] = jnp.zeros_like(acc_ref)

    u = jnp.maximum(_dot(h_ref[...], wu_ref[...]), 0.0)
    acc_ref[...] += _dot((u * u).astype(BF16), wd_ref[...])

    @pl.when(f == pl.num_programs(1) - 1)
    def _():
        y = x_ref[...] + _rms(acc_ref[...], g_ref[...])
        o_ref[...] = y
        if norm_next:
            hn_ref[...] = _rms(y, gn_ref[...]).astype(hn_ref.dtype)
        if tail_start is not None:
            @pl.when(pl.program_id(0) == pl.num_programs(0) - 1)
            def _():
                tail_ref[...] = y[tail_start:, :]


def _mlp(h2, w_up, w_down, x1, g, layer, tm, tf, split_rows=None, next_g=None):
    rows, d = x1.shape
    ff = w_up.shape[2]
    row = pl.BlockSpec((tm, d), lambda i, f: (i, 0))
    vec = pl.BlockSpec((1, d), lambda i, f: (0, 0))
    out_specs, out_shape, tail_start = [row], [jax.ShapeDtypeStruct((rows, d), F32)], None
    if split_rows is not None:
        tail = rows - split_rows
        tail_start = tm - tail
        assert 0 <= tail_start and tail % 8 == 0
        out_specs.append(pl.BlockSpec((tail, d), lambda i, f: (0, 0)))
        out_shape = [jax.ShapeDtypeStruct((split_rows, d), F32), jax.ShapeDtypeStruct((tail, d), F32)]
    in_specs = [row, pl.BlockSpec((None, d, tf), lambda i, f: (layer, 0, f)),
                pl.BlockSpec((None, tf, d), lambda i, f: (layer, f, 0)), row, vec]
    args = [h2, w_up, w_down, x1, g.reshape(1, d)]
    if next_g is not None:
        in_specs.append(vec)
        args.append(next_g.reshape(1, d))
        out_specs.append(row)
        out_shape.append(jax.ShapeDtypeStruct((rows, d), BF16))
    return pl.pallas_call(
        functools.partial(_mlp_body, tail_start=tail_start, norm_next=next_g is not None),
        grid=(rows // tm, ff // tf),
        in_specs=in_specs,
        out_specs=out_specs,
        out_shape=out_shape,
        scratch_shapes=[pltpu.VMEM((tm, d), F32)],
        compiler_params=_params("arbitrary", "arbitrary"),
        name="mlp",
    )(*args)


def _pad_lanes(w):
    return jnp.concatenate([w, jnp.zeros(w.shape[:-1] + (LANES - w.shape[-1],), w.dtype)], axis=-1)


def _swap_halves(w):
    half = w.shape[-1] // 2
    return jnp.concatenate([w[..., half:], w[..., :half]], axis=-1)


def _prep_weights(w_uq, w_uk, w_uv):
    depth = w_uq.shape[0]
    uq = w_uq.reshape(depth, Q_LORA, MLA_HEADS // 2, 2, QK_HEAD)
    nope = uq[..., :QK_NOPE]
    pe = _pad_lanes(uq[..., QK_NOPE:])
    pe_sw = _pad_lanes(_swap_halves(uq[..., QK_NOPE:]))
    w_q = jnp.concatenate([nope[:, :, :, 0], pe[:, :, :, 0], nope[:, :, :, 1], pe[:, :, :, 1],
                           pe_sw[:, :, :, 0], pe_sw[:, :, :, 1]], axis=-1)
    w_q = w_q.reshape(depth, Q_LORA, -1).astype(BF16)
    uk = w_uk.reshape(depth, KV_LORA, MLA_HEADS // 2, 2 * QK_NOPE)
    uv = w_uv.reshape(depth, KV_LORA, MLA_HEADS // 2, 2 * V_HEAD)
    w_kv = jnp.concatenate([uk, uv], axis=-1).reshape(depth, KV_LORA, -1).astype(BF16)
    wuk_t = jnp.transpose(w_uk, (0, 2, 3, 1)).astype(BF16)
    wuv_h = jnp.transpose(w_uv, (0, 2, 1, 3)).astype(BF16)
    return w_q, w_kv, wuk_t, wuv_h


def _rope_tables(positions):
    half = QK_ROPE // 2
    inv = ROPE_THETA ** (-jnp.arange(half, dtype=F32) / half)
    ang = positions.astype(F32)[:, None] * inv[None, :]
    cos, sin = jnp.cos(ang), jnp.sin(ang)
    zeros = jnp.zeros((positions.shape[0], LANES - QK_ROPE), F32)
    return (jnp.concatenate([cos, cos, zeros], axis=1), jnp.concatenate([-sin, sin, zeros], axis=1))


def kernel(x_prompt, x_sample, cache_ckv, cache_kpe, state_hgrn, pre_mix_g, w_in, q_norm_g, w_uq, kv_norm_g,
           w_uk, w_uv, w_oa, hg_lb, hg_norm_g, w_ob, w_out, post_mix_g, pre_mlp_g, w_up, w_down, post_mlp_g):
    batch, seq, d = x_prompt.shape
    dec_batch, dec_seq, _ = x_sample.shape
    depth = w_in.shape[0]
    past = cache_ckv.shape[2]
    n_p = batch * seq
    n_s = dec_batch * dec_seq
    rows = n_p + n_s
    tm = 768
    tm_wide = 1408
    tm_mlp = 704
    assert rows % tm == 0 and rows % tm_wide == 0 and rows % tm_mlp == 0 and n_p % 1024 == 0

    x_parts = (x_prompt.reshape(n_p, d), x_sample.reshape(n_s, d))
    pos = jnp.concatenate([jnp.tile(jnp.arange(seq, dtype=jnp.int32), batch),
                           jnp.tile(past + jnp.arange(dec_seq, dtype=jnp.int32), dec_batch)])
    cos_t, sin_t = _rope_tables(pos)
    cos_q, sin_q = cos_t * Q_SCALE, sin_t * Q_SCALE

    w_in_t = jnp.transpose(w_in, (0, 2, 1))
    cache_kpe_t = jnp.transpose(cache_kpe, (0, 1, 3, 2))
    wide0 = Q_LORA + KV_LORA + QK_ROPE
    w_q, w_kv, wuk_t, wuv_h = _prep_weights(w_uq, w_uk, w_uv)
    w_oa_b, w_ob_b, w_out_b = w_oa.astype(BF16), w_ob.astype(BF16), w_out.astype(BF16)
    w_up_b, w_down_b = w_up.astype(BF16), w_down.astype(BF16)

    ckv_out, kpe_out, st_p_out, st_s_out = [], [], [], []
    for l in range(depth):
        h = _rmsnorm(x_parts, pre_mix_g[l], tm)
        qn, ckv, ckv_b, krot, krot_b = _latent(h, w_in_t, q_norm_g[l], kv_norm_g[l], cos_t, sin_t, l, tm)
        hk, logf = _mm(h, w_in_t, [functools.partial(_epi_forget, layer=l)], [(1024, BF16), (1024, F32)],
                       layer=l, row0s=[wide0 + d], n=d, tm=tm, tn=1024, col_args=[(hg_lb, 1024)], name="proj_hf")
        hq, hv, hgate, ga, gb = [
            (*_mm(h, w_in_t, [epi], [(1024, BF16)], layer=l, row0s=[wide0 + g * d], n=d, tm=tm_wide, tn=1024,
                  name=name), 0)
            for g, epi, name in ((0, _epi_silu, "proj_hq"), (2, _epi_id, "proj_hi"), (3, _epi_silu, "proj_hg"),
                                 (4, _epi_sigmoid, "proj_ga"), (5, _epi_sigmoid, "proj_gb"))]
        hk, logf = [hk, 0], [logf, 0]
        (q,) = _mm_rows(qn, w_q, _epi_q, [(2 * QK_PAD, BF16)], layer=l, tm=tm, tn=6 * LANES,
                        tile_args=[(cos_q, LANES), (sin_q, LANES)], name="proj_q")
        keys, vals = _mm_rows(ckv_b, w_kv, _epi_kv, [(2 * QK_PAD, BF16), (2 * V_HEAD, BF16)], layer=l, tm=1024,
                              tn=4 * LANES, rows=n_p, tile_args=[(krot_b, LANES)], name="proj_kv")
        oa_p = _prompt_attention(q, keys, vals, batch, seq, 256)
        oa_s = _sample_attention(q, ckv_b, krot_b, cache_ckv, cache_kpe_t, wuk_t, wuv_h, l, n_p, dec_batch, dec_seq)
        oh_p, st_p = _hgrn(hq, hk, logf, hv, hgate, hg_norm_g[l], None, layer=l,
                           row_block0=0, n_streams=batch, stream_len=seq, length=2 * CHUNK,
                           block_len=512, group=8)
        oh_s, st_s = _hgrn(hq, hk, logf, hv, hgate, hg_norm_g[l], state_hgrn, layer=l,
                           row_block0=n_p // dec_seq, n_streams=dec_batch, stream_len=dec_seq, length=dec_seq,
                           block_len=dec_seq, group=4)
        mix = _merge((oa_p, oa_s), (oh_p, oh_s), w_oa_b, w_ob_b, ga, gb, l, tm, 1024)
        x1, h2 = _outproj(mix, w_out_b, x_parts, post_mix_g[l], pre_mlp_g[l], l, 384)
        if l + 1 < depth:
            x_parts = tuple(_mlp(h2, w_up_b, w_down_b, x1, post_mlp_g[l], l, tm_mlp, 1024))
        else:
            y_p, y_s = _mlp(h2, w_up_b, w_down_b, x1, post_mlp_g[l], l, tm, 512, split_rows=n_p)

        ckv_out.append(ckv)
        kpe_out.append(krot)
        st_p_out.append(st_p)
        st_s_out.append(st_s)

    ckv_all = jnp.stack(ckv_out)
    kpe_all = jnp.stack(kpe_out)
    return (y_p.reshape(batch, seq, d),
            y_s.reshape(dec_batch, dec_seq, d),
            ckv_all[:, :n_p].reshape(depth, batch, seq, KV_LORA),
            kpe_all[:, :n_p].reshape(depth, batch, seq, QK_ROPE),
            jnp.stack(st_p_out),
            ckv_all[:, n_p:].reshape(depth, dec_batch, dec_seq, KV_LORA),
            kpe_all[:, n_p:].reshape(depth, dec_batch, dec_seq, QK_ROPE),
            jnp.stack(st_s_out))
```

```python
import functools

import jax
import jax.numpy as jnp
import numpy as np
from jax import lax
from jax.experimental import pallas as pl
from jax.experimental.pallas import tpu as pltpu

F32 = jnp.float32
BF16 = jnp.bfloat16

CHUNK = 64
MLA_HEADS = 16
QK_NOPE = 128
QK_ROPE = 64
QK_HEAD = QK_NOPE + QK_ROPE
V_HEAD = 128
Q_LORA = 512
KV_LORA = 512
ROPE_THETA = 10000.0
ATTN_SCALE = QK_HEAD ** -0.5
HG_HEADS = 16
HG_DK = 128
HG_DV = 128
EPS = 1e-6

LANES = 128
QK_PAD = 2 * LANES
NEG_BIG = -1e30
LOG2_E = 1.4426950408889634
Q_SCALE = ATTN_SCALE * LOG2_E
VMEM_LIMIT = 58 * 1024 * 1024


def _params(*sem):
    return pltpu.CompilerParams(dimension_semantics=sem, vmem_limit_bytes=VMEM_LIMIT)


def _rms(x, g):
    return x * lax.rsqrt(jnp.mean(x * x, axis=-1, keepdims=True) + EPS) * g


def _dot(a, b):
    return jnp.dot(a, b, preferred_element_type=F32)


def _dot_nt(a, b):
    return lax.dot_general(a, b, (((1,), (1,)), ((), ())), preferred_element_type=F32)


def _dot_tn(a, b):
    return lax.dot_general(a, b, (((0,), (0,)), ((), ())), preferred_element_type=F32)


def _part_specs(parts, tm, index_map, tail_index_map):
    specs = [pl.BlockSpec((tm, parts[0].shape[1]), index_map)]
    if len(parts) == 2:
        specs.append(pl.BlockSpec(parts[1].shape, tail_index_map))
    return specs


def _stacked_rows(parts):
    return sum(p.shape[0] for p in parts)


def _on_row_tiles(operands, row_axis, tm, fn):
    if all(len(o) == 1 for o in operands):
        fn([o[0][...] for o in operands])
        return
    is_last = pl.program_id(row_axis) == pl.num_programs(row_axis) - 1

    @pl.when(jnp.logical_not(is_last))
    def _():
        fn([o[0][...] for o in operands])

    @pl.when(is_last)
    def _():
        fn([o[0][...] if len(o) == 1 else
            jnp.concatenate([o[0][:tm - o[1].shape[0]], o[1][...]], axis=0) for o in operands])


def _rmsnorm_body(*refs, tm):
    x_parts, (g_ref, o_ref) = refs[:-2], refs[-2:]

    def emit(tiles):
        o_ref[...] = _rms(tiles[0], g_ref[...]).astype(o_ref.dtype)

    _on_row_tiles([x_parts], 0, tm, emit)


def _rmsnorm(x_parts, g, tm):
    rows, d = _stacked_rows(x_parts), x_parts[0].shape[1]
    return pl.pallas_call(
        functools.partial(_rmsnorm_body, tm=tm),
        grid=(rows // tm,),
        in_specs=_part_specs(x_parts, tm, lambda i: (i, 0), lambda i: (0, 0))
        + [pl.BlockSpec((1, d), lambda i: (0, 0))],
        out_specs=pl.BlockSpec((tm, d), lambda i: (i, 0)),
        out_shape=jax.ShapeDtypeStruct((rows, d), BF16),
        compiler_params=_params("arbitrary"),
        name="rmsnorm",
    )(*x_parts, g.reshape(1, d))


def _mm_body(x_ref, w_ref, *rest, epis, tiles_per_group, n_extra, row_chunks):
    extra = rest[:n_extra]
    outs = rest[n_extra:-1]
    w_scr = rest[-1]

    @pl.when(pl.program_id(1) == 0)
    def _():
        w_scr[...] = w_ref[0].astype(w_scr.dtype)

    tm = x_ref.shape[0]
    rc = tm // row_chunks

    def emit(epi):
        for c in range(row_chunks):
            rows = slice(c * rc, (c + 1) * rc)
            acc = _dot_nt(x_ref[rows, :], w_scr[...])
            res = epi(acc, *[e[rows, :] if e.shape[0] == tm else e[...] for e in extra])
            for o_ref, r in zip(outs, res):
                o_ref[rows, :] = r.astype(o_ref.dtype)

    if len(epis) == 1:
        emit(epis[0])
    else:
        group = pl.program_id(0) // tiles_per_group
        for g, epi in enumerate(epis):
            pl.when(group == g)(functools.partial(emit, epi))


def _mm(x, w_t, epis, outs, *, layer, row0s, n, tm, tn, row_chunks=1, col_args=(), tile_args=(), name):
    rows, k = x.shape
    assert all(r % 8 == 0 for r in row0s) and tn % 8 == 0 and len(epis) == len(row0s)
    assert tm % (16 * row_chunks) == 0
    tpg = n // tn
    n_tiles = tpg * len(row0s)

    def slab_row(j):
        row0 = row0s[0]
        for g in range(1, len(row0s)):
            row0 = jnp.where(j // tpg >= g, row0s[g], row0)
        return pl.multiple_of(row0 + (j % tpg) * tn, 8)

    grid = (n_tiles, rows // tm)
    in_specs = [pl.BlockSpec((tm, k), lambda j, i: (i, 0)),
                pl.BlockSpec((pl.Element(1), pl.Element(tn), pl.Element(k)),
                             lambda j, i: (layer, slab_row(j), 0))]
    args = [x, w_t]
    for arr, width in col_args:
        in_specs.append(pl.BlockSpec((arr.shape[0], width), lambda j, i: (0, j)))
        args.append(arr)
    for arr, width, follows_n in tile_args:
        if follows_n:
            in_specs.append(pl.BlockSpec((tm, width), lambda j, i: (i, j)))
        else:
            in_specs.append(pl.BlockSpec((tm, width), lambda j, i: (i, 0)))
        args.append(arr)
    out_specs = [pl.BlockSpec((tm, width), lambda j, i: (i, j)) for width, _ in outs]
    out_shape = [jax.ShapeDtypeStruct((rows, width * n_tiles), dt) for width, dt in outs]
    res = pl.pallas_call(
        functools.partial(_mm_body, epis=epis, tiles_per_group=tpg, n_extra=len(col_args) + len(tile_args),
                          row_chunks=row_chunks),
        grid=grid,
        in_specs=in_specs,
        out_specs=out_specs,
        out_shape=out_shape,
        scratch_shapes=[pltpu.VMEM((tn, k), BF16)],
        compiler_params=_params("parallel", "arbitrary"),
        name=name,
    )(*args)
    return res


def _mm_rows_body(x_ref, w_ref, *rest, epi, n_extra, tn, widths):
    extra = [e[...] for e in rest[:n_extra]]
    outs = rest[n_extra:]
    x = x_ref[...]
    for p in range(w_ref.shape[1] // tn):
        res = epi(_dot(x, w_ref[:, p * tn:(p + 1) * tn]), *extra)
        for o_ref, r, width in zip(outs, res, widths):
            o_ref[:, p * width:(p + 1) * width] = r.astype(o_ref.dtype)


def _mm_rows(x, w, epi, outs, *, layer, tm, tn, rows=None, tile_args=(), name):
    k = x.shape[1]
    rows = x.shape[0] if rows is None else rows
    n = w.shape[2]
    in_specs = [pl.BlockSpec((tm, k), lambda i: (i, 0)),
                pl.BlockSpec((None, k, n), lambda i: (layer, 0, 0))]
    in_specs += [pl.BlockSpec((tm, width), lambda i: (i, 0)) for _, width in tile_args]
    return pl.pallas_call(
        functools.partial(_mm_rows_body, epi=epi, n_extra=len(tile_args), tn=tn,
                          widths=[width for width, _ in outs]),
        grid=(rows // tm,),
        in_specs=in_specs,
        out_specs=[pl.BlockSpec((tm, width * (n // tn)), lambda i: (i, 0)) for width, _ in outs],
        out_shape=[jax.ShapeDtypeStruct((rows, width * (n // tn)), dt) for width, dt in outs],
        compiler_params=_params("parallel"),
        name=name,
    )(x, w, *[arr for arr, _ in tile_args])


def _silu(z):
    return z * jax.nn.sigmoid(z)


def _epi_silu(acc):
    return (_silu(acc),)


def _epi_sigmoid(acc):
    return (jax.nn.sigmoid(acc),)


def _epi_id(acc):
    return (acc,)


def _epi_relu2(acc):
    r = jnp.maximum(acc, 0.0)
    return (r * r,)


def _epi_forget(acc, lb_logits, *, layer):
    mx = jnp.max(lb_logits, axis=0, keepdims=True)
    e = jnp.exp(lb_logits - mx)
    sm = e / jnp.sum(e, axis=0, keepdims=True)
    lb = jnp.zeros_like(mx)
    for i in range(1, layer + 1):
        lb = lb + sm[i:i + 1]
    z = acc
    e = jnp.exp(-jnp.abs(z))
    r = 1.0 / (1.0 + e)
    sig_neg = jnp.where(z >= 0.0, e * r, r)
    log_sig = jnp.minimum(z, 0.0) - jnp.log(1.0 + e)
    a = jnp.log(lb)
    c = jnp.log1p(-lb) + log_sig
    hi = jnp.maximum(a, c)
    lo = jnp.minimum(a, c)
    logf = hi + jnp.log(1.0 + jnp.exp(lo - hi))
    kk = (1.0 - lb) * sig_neg
    return kk, logf


def _epi_q(acc, cos_t, sin_t):
    n0 = acc[:, 0 * LANES:1 * LANES] * Q_SCALE
    p0 = acc[:, 1 * LANES:2 * LANES] * cos_t + acc[:, 4 * LANES:5 * LANES] * sin_t
    n1 = acc[:, 2 * LANES:3 * LANES] * Q_SCALE
    p1 = acc[:, 3 * LANES:4 * LANES] * cos_t + acc[:, 5 * LANES:6 * LANES] * sin_t
    return (jnp.concatenate([n0, p0, n1, p1], axis=1),)


def _epi_kv(acc, krot):
    krot = krot.astype(F32)
    keys = jnp.concatenate([acc[:, 0:LANES], krot, acc[:, LANES:2 * LANES], krot], axis=1)
    return keys, acc[:, 2 * LANES:4 * LANES]


def _latent_body(h_ref, w_ref, gq_ref, gkv_ref, cos_ref, sin_ref,
                 qn_ref, ckv_ref, ckvb_ref, krot_ref, krotb_ref, w_scr):
    base = Q_LORA + KV_LORA
    half = QK_ROPE // 2

    @pl.when(pl.program_id(0) == 0)
    def _():
        w_scr[...] = jnp.zeros_like(w_scr)
        w_scr[0:base + QK_ROPE, :] = w_ref[...].astype(w_scr.dtype)
        w_scr[base + LANES:base + LANES + half, :] = w_ref[base + half:base + QK_ROPE, :].astype(w_scr.dtype)
        w_scr[base + LANES + half:base + LANES + QK_ROPE, :] = w_ref[base:base + half, :].astype(w_scr.dtype)

    acc = _dot_nt(h_ref[...], w_scr[...])
    qn_ref[...] = _rms(acc[:, :Q_LORA], gq_ref[...]).astype(qn_ref.dtype)
    ckv = _rms(acc[:, Q_LORA:base], gkv_ref[...])
    ckv_ref[...] = ckv
    ckvb_ref[...] = ckv.astype(ckvb_ref.dtype)
    kr = acc[:, base:base + LANES] * cos_ref[...] + acc[:, base + LANES:base + 2 * LANES] * sin_ref[...]
    krot_ref[...] = kr[:, :QK_ROPE]
    krotb_ref[...] = kr.astype(krotb_ref.dtype)


def _latent(h, w_t, gq, gkv, cos_t, sin_t, layer, tm):
    rows, d = h.shape
    n_in = Q_LORA + KV_LORA + QK_ROPE
    row = lambda width: pl.BlockSpec((tm, width), lambda i: (i, 0))
    const = lambda r, width: pl.BlockSpec((r, width), lambda i: (0, 0))
    return pl.pallas_call(
        _latent_body,
        grid=(rows // tm,),
        in_specs=[row(d), pl.BlockSpec((None, n_in, d), lambda i: (layer, 0, 0)),
                  const(1, Q_LORA), const(1, KV_LORA), row(LANES), row(LANES)],
        out_specs=[row(Q_LORA), row(KV_LORA), row(KV_LORA), row(QK_ROPE), row(LANES)],
        out_shape=[jax.ShapeDtypeStruct((rows, Q_LORA), BF16),
                   jax.ShapeDtypeStruct((rows, KV_LORA), F32),
                   jax.ShapeDtypeStruct((rows, KV_LORA), BF16),
                   jax.ShapeDtypeStruct((rows, QK_ROPE), F32),
                   jax.ShapeDtypeStruct((rows, LANES), BF16)],
        scratch_shapes=[pltpu.VMEM((Q_LORA + KV_LORA + 2 * LANES, d), BF16)],
        compiler_params=_params("arbitrary"),
        name="latent_proj",
    )(h, w_t, gq.reshape(1, -1), gkv.reshape(1, -1), cos_t, sin_t)


def _attn_body(q_ref, k_ref, v_ref, o_ref, *, tq, nq):
    k_chunk = lax.broadcasted_iota(jnp.int32, (tq, tq), 0) // CHUNK
    q_chunk = lax.broadcasted_iota(jnp.int32, (tq, tq), 1) // CHUNK
    visible = k_chunk <= q_chunk
    v_t = v_ref[...].T

    scores = []
    for n_past in range(nq):
        lo = n_past * tq
        q = q_ref[lo:lo + tq, :]
        s_d = jnp.where(visible, _dot_nt(k_ref[lo:lo + tq, :], q), NEG_BIG)
        s_p = _dot_nt(k_ref[0:lo, :], q) if n_past else None
        scores.append((s_d, s_p))
    probs = []
    for s_d, s_p in scores:
        m = jnp.max(s_d, axis=0, keepdims=True)
        if s_p is not None:
            m = jnp.maximum(m, jnp.max(s_p, axis=0, keepdims=True))
        p_d = jnp.exp2(s_d - m)
        l = jnp.sum(p_d, axis=0, keepdims=True)
        p_p = None
        if s_p is not None:
            p_p = jnp.exp2(s_p - m)
            l = l + jnp.sum(p_p, axis=0, keepdims=True)
            p_p = p_p.astype(BF16)
        probs.append((p_d.astype(BF16), p_p, l))
    for n_past, (p_d, p_p, l) in enumerate(probs):
        lo = n_past * tq
        acc = _dot(v_t[:, lo:lo + tq], p_d)
        if p_p is not None:
            acc = acc + _dot(v_t[:, 0:lo], p_p)
        o_ref[lo:lo + tq, :] = (acc / l).T.astype(o_ref.dtype)


def _prompt_attention(q, k, v, batch, seq, tq):
    return pl.pallas_call(
        functools.partial(_attn_body, tq=tq, nq=seq // tq),
        grid=(batch, MLA_HEADS),
        in_specs=[pl.BlockSpec((seq, QK_PAD), lambda b, h: (b, h)),
                  pl.BlockSpec((seq, QK_PAD), lambda b, h: (b, h)),
                  pl.BlockSpec((seq, V_HEAD), lambda b, h: (b, h))],
        out_specs=pl.BlockSpec((seq, V_HEAD), lambda b, h: (b, h)),
        out_shape=jax.ShapeDtypeStruct((batch * seq, MLA_HEADS * V_HEAD), BF16),
        compiler_params=_params("parallel", "parallel"),
        name="prompt_attention",
    )(q, k, v)


def _sattn_body(q_ref, cn_ref, kn_ref, cc_ref, ck_ref, wuk_ref, wuv_ref, o_ref, qa_scr, qp_scr,
                *, dec_seq, key_tile):
    for h in range(MLA_HEADS):
        qn = q_ref[:, h * QK_PAD:h * QK_PAD + QK_NOPE]
        qa_scr[h * dec_seq:(h + 1) * dec_seq, :] = _dot(qn, wuk_ref[h]).astype(BF16)
        qp_scr[h * dec_seq:(h + 1) * dec_seq, :] = q_ref[:, h * QK_PAD + QK_NOPE:(h + 1) * QK_PAD]
    qa = qa_scr[...]
    qp = qp_scr[...]
    rows = MLA_HEADS * dec_seq

    def update(carry, s, c):
        m, l, acc = carry
        m_new = jnp.maximum(m, jnp.max(s, axis=-1, keepdims=True))
        alpha = jnp.exp2(m - m_new)
        p = jnp.exp2(s - m_new)
        l = alpha * l + jnp.sum(p, axis=-1, keepdims=True)
        acc = alpha * acc + _dot(p.astype(BF16), c)
        return m_new, l, acc

    carry = (jnp.full((rows, 1), NEG_BIG, F32), jnp.zeros((rows, 1), F32), jnp.zeros((rows, KV_LORA), F32))
    past = cc_ref.shape[0]
    for t in range(past // key_tile):
        c = cc_ref[t * key_tile:(t + 1) * key_tile, :].astype(BF16)
        kp_t = ck_ref[:, t * key_tile:(t + 1) * key_tile].astype(BF16)
        s = _dot_nt(qa, c) + _dot(qp[:, :QK_ROPE], kp_t)
        carry = update(carry, s, c)
    cn = cn_ref[...]
    s = _dot_nt(qa, cn) + _dot_nt(qp, kn_ref[...])
    _, l, acc = update(carry, s, cn)
    o_lat = (acc / l).astype(BF16)
    for h in range(MLA_HEADS):
        o_ref[:, h * V_HEAD:(h + 1) * V_HEAD] = _dot(
            o_lat[h * dec_seq:(h + 1) * dec_seq, :], wuv_ref[h]).astype(o_ref.dtype)


def _sample_attention(q, ckv_b, krot_b, cache_c, cache_k, wuk_t, wuv_h, layer, n_prompt_rows, dec_batch, dec_seq):
    rb = n_prompt_rows // dec_seq
    past = cache_c.shape[2]
    key_tile = min(past, 1024)
    per_layer = lambda s: pl.BlockSpec((None,) + s, lambda b: (layer, 0, 0, 0))
    return pl.pallas_call(
        functools.partial(_sattn_body, dec_seq=dec_seq, key_tile=key_tile),
        grid=(dec_batch,),
        in_specs=[pl.BlockSpec((dec_seq, MLA_HEADS * QK_PAD), lambda b: (rb + b, 0)),
                  pl.BlockSpec((dec_seq, KV_LORA), lambda b: (rb + b, 0)),
                  pl.BlockSpec((dec_seq, LANES), lambda b: (rb + b, 0)),
                  pl.BlockSpec((None, None, past, KV_LORA), lambda b: (layer, b, 0, 0)),
                  pl.BlockSpec((None, None, QK_ROPE, past), lambda b: (layer, b, 0, 0)),
                  per_layer((MLA_HEADS, QK_NOPE, KV_LORA)),
                  per_layer((MLA_HEADS, KV_LORA, V_HEAD))],
        out_specs=pl.BlockSpec((dec_seq, MLA_HEADS * V_HEAD), lambda b: (b, 0)),
        out_shape=jax.ShapeDtypeStruct((dec_batch * dec_seq, MLA_HEADS * V_HEAD), BF16),
        scratch_shapes=[pltpu.VMEM((MLA_HEADS * dec_seq, KV_LORA), BF16),
                        pltpu.VMEM((MLA_HEADS * dec_seq, LANES), BF16)],
        compiler_params=_params("parallel"),
        name="sample_attention",
    )(q, ckv_b, krot_b, cache_c, cache_k, wuk_t, wuv_h)


def _hgrn_tables(length):
    t = np.arange(length)[:, None]
    r = np.arange(length)[None, :]
    groups = [(r <= t)]
    masks = [(r == t)]
    m = length // 2
    while m >= 1:
        blk = t // (2 * m)
        start2 = blk * 2 * m + m
        second = (t % (2 * m)) >= m
        incl = second & (r >= start2) & (r <= t)
        excl = (~second) & (r > t) & (r < start2)
        if m % 8:
            groups.append(incl | excl)
        r_blk = r // (2 * m)
        r_first = (r % (2 * m)) < m
        masks.append(second & r_first & (r_blk == blk))
        m //= 2
    return (np.concatenate(groups, axis=0).astype(np.float32),
            np.stack(masks, axis=0).astype(np.float32))


def _hgrn_body(*refs, length, n_chunks, group, has_state):
    if has_state:
        q_ref, k_ref, lf_ref, v_ref, g_ref, gn_ref, sum_ref, mask_ref, s0_ref, o_ref, sout_ref, st_scr = refs
    else:
        q_ref, k_ref, lf_ref, v_ref, g_ref, gn_ref, sum_ref, mask_ref, o_ref, sout_ref, st_scr = refs
    tb = pl.program_id(2)

    @pl.when(tb == 0)
    def _():
        for g in range(group):
            st_scr[g] = s0_ref[g].T if has_state else jnp.zeros((HG_DV, HG_DK), F32)

    n_lev = mask_ref.shape[0] - 1
    summat = sum_ref[...]
    gn = gn_ref[...]

    pack_levels = length % LANES == 0 and (n_lev + 1) % 2 == 0
    zeros_k = jnp.zeros((length, HG_DK), BF16)

    def mixed_rows(q, k, half):
        parts = []
        for r in range(0, length, half):
            parts.append((q if (r // half) % 2 else k)[r:r + half])
        return jnp.concatenate(parts, axis=0)

    halves = [length >> (lev + 1) for lev in range(n_lev)]
    coarse = [half for half in halves if half % 8 == 0]

    def exponentials(sums):
        b = sums[0:length]
        levels = []
        for half in coarse:
            parts = []
            for r in range(0, length, half):
                ref = (r // (2 * half)) * 2 * half + half - 1
                ref_row = b[ref:ref + 1]
                parts.append(b[r:r + half] - ref_row if (r // half) % 2 else ref_row - b[r:r + half])
            levels.append(jnp.concatenate(parts, axis=0))
        for i in range(n_lev - len(coarse)):
            levels.append(sums[(1 + i) * length:(2 + i) * length])
        e_in = jnp.exp2(b)
        e_out = jnp.exp2(b[length - 1:length] - b)
        return e_in, e_out, [jnp.exp2(x) for x in levels]

    def level_operands(rows, g, e_in, e_out, e_levels):
        cols = slice(g * HG_DK, (g + 1) * HG_DK)
        qs = [q_ref[rows, cols]]
        ks = [k_ref[rows, cols]]
        q = qs[0].astype(F32)
        k = ks[0].astype(F32)
        for half, e in zip(halves, e_levels):
            if half % 8 == 0:
                u = (mixed_rows(q, k, half) * e).astype(BF16)
                qs.append(u)
                ks.append(u)
            else:
                qs.append((q * e).astype(BF16))
                ks.append((k * e).astype(BF16))
        return qs, ks, (q * e_in).astype(BF16), (k * e_out).astype(BF16)

    def level_scores(qs, ks):
        if pack_levels:
            out = []
            for p in range(0, n_lev + 1, 2):
                lhs = jnp.concatenate([qs[p], qs[p + 1]], axis=1)
                rhs = jnp.concatenate([jnp.concatenate([ks[p], zeros_k], axis=1),
                                       jnp.concatenate([zeros_k, ks[p + 1]], axis=1)], axis=0)
                a = _dot_nt(lhs, rhs)
                out += [a[:, :length], a[:, length:]]
            return out
        return [_dot_nt(qs[p], ks[p]) for p in range(n_lev + 1)]

    def chunk(c, _):
        rows = pl.ds(pl.multiple_of(c * length, length), length)
        e_heads = []
        for pair in range(group // 2):
            cols2 = slice(2 * pair * HG_DK, (2 * pair + 2) * HG_DK)
            lf2 = lf_ref[rows, cols2] * LOG2_E
            hi = lf2.astype(BF16)
            mid = (lf2 - hi.astype(F32)).astype(BF16)
            e_in, e_out, e_levels = exponentials(_dot(summat, jnp.concatenate([hi, mid], axis=0)))
            for lanes in (slice(0, HG_DK), slice(HG_DK, 2 * HG_DK)):
                e_heads.append((e_in[:, lanes], e_out[:, lanes], [e[:, lanes] for e in e_levels]))
        work = []
        for g in range(group):
            qs, ks, q_in, k_out = level_operands(rows, g, *e_heads[g])
            st = st_scr[g]
            work.append((level_scores(qs, ks), _dot_nt(q_in, st.astype(BF16)), k_out, st))
        for g in range(group):
            cols = slice(g * HG_DK, (g + 1) * HG_DK)
            scores, o_in, k_out, st = work[g]
            v = v_ref[rows, cols]
            att = mask_ref[0] * scores[0]
            for p in range(1, n_lev + 1):
                att = att + mask_ref[p] * scores[p]
            o = o_in + _dot(att.astype(BF16), v)
            e_last = e_heads[g][0][length - 1:length, :]
            st_scr[g] = st * e_last + _dot_tn(v, k_out)
            o_ref[rows, cols] = (_rms(o, gn) * g_ref[rows, cols]).astype(o_ref.dtype)
        return 0

    lax.fori_loop(0, n_chunks, chunk, 0)

    @pl.when(tb == pl.num_programs(2) - 1)
    def _():
        for g in range(group):
            sout_ref[g] = st_scr[g].T


def _hgrn(q, k, logf, v, gate, gn, state0, *, layer, row_block0, n_streams, stream_len, length, block_len, group):
    n_tb = stream_len // block_len
    n_chunks = block_len // length
    width = group * HG_DK
    summat, masks = _hgrn_tables(length)
    summat = jnp.asarray(np.concatenate([summat, summat], axis=1), BF16)
    masks = jnp.asarray(masks, F32)
    tok = lambda col0: pl.BlockSpec((block_len, width),
                                    lambda n, h, t: (row_block0 + n * n_tb + t, col0 // width + h))
    tokens = (q, k, logf, v, gate)
    st_spec = pl.BlockSpec((None, group, HG_DK, HG_DV), lambda n, h, t: (n, h, 0, 0))
    in_specs = [tok(col0) for _, col0 in tokens] + [pl.BlockSpec((1, HG_DV), lambda n, h, t: (0, 0)),
                pl.BlockSpec(summat.shape, lambda n, h, t: (0, 0)),
                pl.BlockSpec(masks.shape, lambda n, h, t: (0, 0, 0))]
    args = [arr for arr, _ in tokens] + [gn.reshape(1, HG_DV), summat, masks]
    if state0 is not None:
        in_specs.append(pl.BlockSpec((None, None, group, HG_DK, HG_DV), lambda n, h, t: (layer, n, h, 0, 0)))
        args.append(state0)
    return pl.pallas_call(
        functools.partial(_hgrn_body, length=length, n_chunks=n_chunks, group=group,
                          has_state=state0 is not None),
        grid=(n_streams, HG_HEADS // group, n_tb),
        in_specs=in_specs,
        out_specs=[pl.BlockSpec((block_len, width), lambda n, h, t: (n * n_tb + t, h)), st_spec],
        out_shape=[jax.ShapeDtypeStruct((n_streams * stream_len, HG_HEADS * HG_DV), BF16),
                   jax.ShapeDtypeStruct((n_streams, HG_HEADS, HG_DK, HG_DV), F32)],
        scratch_shapes=[pltpu.VMEM((group, HG_DV, HG_DK), F32)],
        compiler_params=_params("parallel", "parallel", "arbitrary"),
        name="hgrn_state" if state0 is not None else "hgrn_prompt",
    )(*args)


def _merge_body(*refs, tm, n_a):
    oa_parts, oh_parts = refs[:n_a], refs[n_a:-5]
    woa_ref, wob_ref, ga_ref, gb_ref, o_ref = refs[-5:]

    def emit(tiles):
        a = _dot(tiles[0], woa_ref[...])
        b = _dot(tiles[1], wob_ref[...])
        o_ref[...] = (ga_ref[...] * a + gb_ref[...] * b).astype(o_ref.dtype)

    _on_row_tiles([oa_parts, oh_parts], 1, tm, emit)


def _merge(oa_parts, oh_parts, woa, wob, ga, gb, layer, tm, tn):
    rows, k = _stacked_rows(oa_parts), oa_parts[0].shape[1]
    n = woa.shape[2]
    x_specs = lambda parts: _part_specs(parts, tm, lambda j, i: (i, 0), lambda j, i: (0, 0))
    w_spec = pl.BlockSpec((None, k, tn), lambda j, i: (layer, 0, j))
    t_spec = pl.BlockSpec((tm, tn), lambda j, i: (i, j))
    return pl.pallas_call(
        functools.partial(_merge_body, tm=tm, n_a=len(oa_parts)),
        grid=(n // tn, rows // tm),
        in_specs=x_specs(oa_parts) + x_specs(oh_parts) + [w_spec, w_spec]
        + [pl.BlockSpec((tm, tn), lambda j, i, c=col0 // tn: (i, c + j)) for _, col0 in (ga, gb)],
        out_specs=t_spec,
        out_shape=jax.ShapeDtypeStruct((rows, n), BF16),
        compiler_params=_params("parallel", "arbitrary"),
        name="gated_merge",
    )(*oa_parts, *oh_parts, woa, wob, ga[0], gb[0])


def _outproj_body(m_ref, w_ref, *refs, tm):
    x_parts, (g1_ref, g2_ref, x1_ref, h2_ref) = refs[:-4], refs[-4:]

    def emit(tiles):
        rc = tm // 2
        for c in range(2):
            rows = slice(c * rc, (c + 1) * rc)
            y = _dot(m_ref[rows, :], w_ref[...])
            x1 = tiles[0][rows, :] + _rms(y, g1_ref[...])
            x1_ref[rows, :] = x1
            h2_ref[rows, :] = _rms(x1, g2_ref[...]).astype(h2_ref.dtype)

    _on_row_tiles([x_parts], 0, tm, emit)


def _outproj(mix, w, x_parts, g_post, g_pre2, layer, tm):
    rows, d = mix.shape
    row = pl.BlockSpec((tm, d), lambda i: (i, 0))
    vec = pl.BlockSpec((1, d), lambda i: (0, 0))
    return pl.pallas_call(
        functools.partial(_outproj_body, tm=tm),
        grid=(rows // tm,),
        in_specs=[row, pl.BlockSpec((None, d, d), lambda i: (layer, 0, 0))]
        + _part_specs(x_parts, tm, lambda i: (i, 0), lambda i: (0, 0)) + [vec, vec],
        out_specs=[row, row],
        out_shape=[jax.ShapeDtypeStruct((rows, d), F32), jax.ShapeDtypeStruct((rows, d), BF16)],
        compiler_params=_params("arbitrary"),
        name="out_proj",
    )(mix, w, *x_parts, g_post.reshape(1, d), g_pre2.reshape(1, d))


def _mlp_body(h_ref, wu_ref, wd_ref, x_ref, g_ref, *rest, tail_start, norm_next):
    rest = list(rest)
    gn_ref = rest.pop(0) if norm_next else None
    o_ref = rest.pop(0)
    tail_ref = rest.pop(0) if tail_start is not None else None
    hn_ref = rest.pop(0) if norm_next else None
    acc_ref = rest.pop(0)
    f = pl.program_id(1)

    @pl.when(f == 0)
    def _():
        acc_ref[...] = jnp.zeros_like(acc_ref)

    u = jnp.maximum(_dot(h_ref[...], wu_ref[...]), 0.0)
    acc_ref[...] += _dot((u * u).astype(BF16), wd_ref[...])

    @pl.when(f == pl.num_programs(1) - 1)
    def _():
        y = x_ref[...] + _rms(acc_ref[...], g_ref[...])
        o_ref[...] = y
        if norm_next:
            hn_ref[...] = _rms(y, gn_ref[...]).astype(hn_ref.dtype)
        if tail_start is not None:
            @pl.when(pl.program_id(0) == pl.num_programs(0) - 1)
            def _():
                tail_ref[...] = y[tail_start:, :]


def _mlp(h2, w_up, w_down, x1, g, layer, tm, tf, split_rows=None, next_g=None):
    rows, d = x1.shape
    ff = w_up.shape[2]
    row = pl.BlockSpec((tm, d), lambda i, f: (i, 0))
    vec = pl.BlockSpec((1, d), lambda i, f: (0, 0))
    out_specs, out_shape, tail_start = [row], [jax.ShapeDtypeStruct((rows, d), F32)], None
    if split_rows is not None:
        tail = rows - split_rows
        tail_start = tm - tail
        assert 0 <= tail_start and tail % 8 == 0
        out_specs.append(pl.BlockSpec((tail, d), lambda i, f: (0, 0)))
        out_shape = [jax.ShapeDtypeStruct((split_rows, d), F32), jax.ShapeDtypeStruct((tail, d), F32)]
    in_specs = [row, pl.BlockSpec((None, d, tf), lambda i, f: (layer, 0, f)),
                pl.BlockSpec((None, tf, d), lambda i, f: (layer, f, 0)), row, vec]
    args = [h2, w_up, w_down, x1, g.reshape(1, d)]
    if next_g is not None:
        in_specs.append(vec)
        args.append(next_g.reshape(1, d))
        out_specs.append(row)
        out_shape.append(jax.ShapeDtypeStruct((rows, d), BF16))
    return pl.pallas_call(
        functools.partial(_mlp_body, tail_start=tail_start, norm_next=next_g is not None),
        grid=(rows // tm, ff // tf),
        in_specs=in_specs,
        out_specs=out_specs,
        out_shape=out_shape,
        scratch_shapes=[pltpu.VMEM((tm, d), F32)],
        compiler_params=_params("arbitrary", "arbitrary"),
        name="mlp",
    )(*args)


def _pad_lanes(w):
    return jnp.concatenate([w, jnp.zeros(w.shape[:-1] + (LANES - w.shape[-1],), w.dtype)], axis=-1)


def _swap_halves(w):
    half = w.shape[-1] // 2
    return jnp.concatenate([w[..., half:], w[..., :half]], axis=-1)


def _prep_weights(w_uq, w_uk, w_uv):
    depth = w_uq.shape[0]
    uq = w_uq.reshape(depth, Q_LORA, MLA_HEADS // 2, 2, QK_HEAD)
    nope = uq[..., :QK_NOPE]
    pe = _pad_lanes(uq[..., QK_NOPE:])
    pe_sw = _pad_lanes(_swap_halves(uq[..., QK_NOPE:]))
    w_q = jnp.concatenate([nope[:, :, :, 0], pe[:, :, :, 0], nope[:, :, :, 1], pe[:, :, :, 1],
                           pe_sw[:, :, :, 0], pe_sw[:, :, :, 1]], axis=-1)
    w_q = w_q.reshape(depth, Q_LORA, -1).astype(BF16)
    uk = w_uk.reshape(depth, KV_LORA, MLA_HEADS // 2, 2 * QK_NOPE)
    uv = w_uv.reshape(depth, KV_LORA, MLA_HEADS // 2, 2 * V_HEAD)
    w_kv = jnp.concatenate([uk, uv], axis=-1).reshape(depth, KV_LORA, -1).astype(BF16)
    wuk_t = jnp.transpose(w_uk, (0, 2, 3, 1)).astype(BF16)
    wuv_h = jnp.transpose(w_uv, (0, 2, 1, 3)).astype(BF16)
    return w_q, w_kv, wuk_t, wuv_h


def _rope_tables(positions):
    half = QK_ROPE // 2
    inv = ROPE_THETA ** (-jnp.arange(half, dtype=F32) / half)
    ang = positions.astype(F32)[:, None] * inv[None, :]
    cos, sin = jnp.cos(ang), jnp.sin(ang)
    zeros = jnp.zeros((positions.shape[0], LANES - QK_ROPE), F32)
    return (jnp.concatenate([cos, cos, zeros], axis=1), jnp.concatenate([-sin, sin, zeros], axis=1))


def kernel(x_prompt, x_sample, cache_ckv, cache_kpe, state_hgrn, pre_mix_g, w_in, q_norm_g, w_uq, kv_norm_g,
           w_uk, w_uv, w_oa, hg_lb, hg_norm_g, w_ob, w_out, post_mix_g, pre_mlp_g, w_up, w_down, post_mlp_g):
    batch, seq, d = x_prompt.shape
    dec_batch, dec_seq, _ = x_sample.shape
    depth = w_in.shape[0]
    past = cache_ckv.shape[2]
    n_p = batch * seq
    n_s = dec_batch * dec_seq
    rows = n_p + n_s
    tm = 768
    tm_wide = 1408
    tm_mlp = 704
    assert rows % tm == 0 and rows % tm_wide == 0 and rows % tm_mlp == 0 and n_p % 1024 == 0

    x_parts = (x_prompt.reshape(n_p, d), x_sample.reshape(n_s, d))
    pos = jnp.concatenate([jnp.tile(jnp.arange(seq, dtype=jnp.int32), batch),
                           jnp.tile(past + jnp.arange(dec_seq, dtype=jnp.int32), dec_batch)])
    cos_t, sin_t = _rope_tables(pos)
    cos_q, sin_q = cos_t * Q_SCALE, sin_t * Q_SCALE

    w_in_t = jnp.transpose(w_in, (0, 2, 1))
    cache_kpe_t = jnp.transpose(cache_kpe, (0, 1, 3, 2))
    wide0 = Q_LORA + KV_LORA + QK_ROPE
    w_q, w_kv, wuk_t, wuv_h = _prep_weights(w_uq, w_uk, w_uv)
    w_oa_b, w_ob_b, w_out_b = w_oa.astype(BF16), w_ob.astype(BF16), w_out.astype(BF16)
    w_up_b, w_down_b = w_up.astype(BF16), w_down.astype(BF16)

    ckv_out, kpe_out, st_p_out, st_s_out = [], [], [], []
    for l in range(depth):
        h = _rmsnorm(x_parts, pre_mix_g[l], tm)
        qn, ckv, ckv_b, krot, krot_b = _latent(h, w_in_t, q_norm_g[l], kv_norm_g[l], cos_t, sin_t, l, tm)
        hk, logf = _mm(h, w_in_t, [functools.partial(_epi_forget, layer=l)], [(1024, BF16), (1024, F32)],
                       layer=l, row0s=[wide0 + d], n=d, tm=tm, tn=1024, col_args=[(hg_lb, 1024)], name="proj_hf")
        hq, hv, hgate, ga, gb = [
            (*_mm(h, w_in_t, [epi], [(1024, BF16)], layer=l, row0s=[wide0 + g * d], n=d, tm=tm_wide, tn=1024,
                  row_chunks=4, name=name), 0)
            for g, epi, name in ((0, _epi_silu, "proj_hq"), (2, _epi_id, "proj_hi"), (3, _epi_silu, "proj_hg"),
                                 (4, _epi_sigmoid, "proj_ga"), (5, _epi_sigmoid, "proj_gb"))]
        hk, logf = [hk, 0], [logf, 0]
        (q,) = _mm_rows(qn, w_q, _epi_q, [(2 * QK_PAD, BF16)], layer=l, tm=tm, tn=6 * LANES,
                        tile_args=[(cos_q, LANES), (sin_q, LANES)], name="proj_q")
        keys, vals = _mm_rows(ckv_b, w_kv, _epi_kv, [(2 * QK_PAD, BF16), (2 * V_HEAD, BF16)], layer=l, tm=1024,
                              tn=4 * LANES, rows=n_p, tile_args=[(krot_b, LANES)], name="proj_kv")
        oa_p = _prompt_attention(q, keys, vals, batch, seq, 256)
        oa_s = _sample_attention(q, ckv_b, krot_b, cache_ckv, cache_kpe_t, wuk_t, wuv_h, l, n_p, dec_batch, dec_seq)
        oh_p, st_p = _hgrn(hq, hk, logf, hv, hgate, hg_norm_g[l], None, layer=l,
                           row_block0=0, n_streams=batch, stream_len=seq, length=2 * CHUNK,
                           block_len=1024, group=8)
        oh_s, st_s = _hgrn(hq, hk, logf, hv, hgate, hg_norm_g[l], state_hgrn, layer=l,
                           row_block0=n_p // dec_seq, n_streams=dec_batch, stream_len=dec_seq, length=dec_seq,
                           block_len=dec_seq, group=4)
        mix = _merge((oa_p, oa_s), (oh_p, oh_s), w_oa_b, w_ob_b, ga, gb, l, tm, 1024)
        x1, h2 = _outproj(mix, w_out_b, x_parts, post_mix_g[l], pre_mlp_g[l], l, 384)
        if l + 1 < depth:
            x_parts = tuple(_mlp(h2, w_up_b, w_down_b, x1, post_mlp_g[l], l, tm_mlp, 1024))
        else:
            y_p, y_s = _mlp(h2, w_up_b, w_down_b, x1, post_mlp_g[l], l, tm, 512, split_rows=n_p)

        ckv_out.append(ckv)
        kpe_out.append(krot)
        st_p_out.append(st_p)
        st_s_out.append(st_s)

    ckv_all = jnp.stack(ckv_out)
    kpe_all = jnp.stack(kpe_out)
    return (y_p.reshape(batch, seq, d),
            y_s.reshape(dec_batch, dec_seq, d),
            ckv_all[:, :n_p].reshape(depth, batch, seq, KV_LORA),
            kpe_all[:, :n_p].reshape(depth, batch, seq, QK_ROPE),
            jnp.stack(st_p_out),
            ckv_all[:, n_p:].reshape(depth, dec_batch, dec_seq, KV_LORA),
            kpe_all[:, n_p:].reshape(depth, dec_batch, dec_seq, QK_ROPE),
            jnp.stack(st_s_out))
```

```python
import functools

import jax
import jax.numpy as jnp
import numpy as np
from jax import lax
from jax.experimental import pallas as pl
from jax.experimental.pallas import tpu as pltpu

F32 = jnp.float32
BF16 = jnp.bfloat16

CHUNK = 64
MLA_HEADS = 16
QK_NOPE = 128
QK_ROPE = 64
QK_HEAD = QK_NOPE + QK_ROPE
V_HEAD = 128
Q_LORA = 512
KV_LORA = 512
ROPE_THETA = 10000.0
ATTN_SCALE = QK_HEAD ** -0.5
HG_HEADS = 16
HG_DK = 128
HG_DV = 128
EPS = 1e-6

LANES = 128
QK_PAD = 2 * LANES
NEG_BIG = -1e30
LOG2_E = 1.4426950408889634
Q_SCALE = ATTN_SCALE * LOG2_E
VMEM_LIMIT = 58 * 1024 * 1024


def _params(*sem):
    return pltpu.CompilerParams(dimension_semantics=sem, vmem_limit_bytes=VMEM_LIMIT)


def _rms(x, g):
    return x * lax.rsqrt(jnp.mean(x * x, axis=-1, keepdims=True) + EPS) * g


def _dot(a, b):
    return jnp.dot(a, b, preferred_element_type=F32)


def _dot_nt(a, b):
    return lax.dot_general(a, b, (((1,), (1,)), ((), ())), preferred_element_type=F32)


def _dot_tn(a, b):
    return lax.dot_general(a, b, (((0,), (0,)), ((), ())), preferred_element_type=F32)


def _part_specs(parts, tm, index_map, tail_index_map):
    specs = [pl.BlockSpec((tm, parts[0].shape[1]), index_map)]
    if len(parts) == 2:
        specs.append(pl.BlockSpec(parts[1].shape, tail_index_map))
    return specs


def _stacked_rows(parts):
    return sum(p.shape[0] for p in parts)


def _on_row_tiles(operands, row_axis, tm, fn):
    if all(len(o) == 1 for o in operands):
        fn([o[0][...] for o in operands])
        return
    is_last = pl.program_id(row_axis) == pl.num_programs(row_axis) - 1

    @pl.when(jnp.logical_not(is_last))
    def _():
        fn([o[0][...] for o in operands])

    @pl.when(is_last)
    def _():
        fn([o[0][...] if len(o) == 1 else
            jnp.concatenate([o[0][:tm - o[1].shape[0]], o[1][...]], axis=0) for o in operands])


def _rmsnorm_body(*refs, tm):
    x_parts, (g_ref, o_ref) = refs[:-2], refs[-2:]

    def emit(tiles):
        o_ref[...] = _rms(tiles[0], g_ref[...]).astype(o_ref.dtype)

    _on_row_tiles([x_parts], 0, tm, emit)


def _rmsnorm(x_parts, g, tm):
    rows, d = _stacked_rows(x_parts), x_parts[0].shape[1]
    return pl.pallas_call(
        functools.partial(_rmsnorm_body, tm=tm),
        grid=(rows // tm,),
        in_specs=_part_specs(x_parts, tm, lambda i: (i, 0), lambda i: (0, 0))
        + [pl.BlockSpec((1, d), lambda i: (0, 0))],
        out_specs=pl.BlockSpec((tm, d), lambda i: (i, 0)),
        out_shape=jax.ShapeDtypeStruct((rows, d), BF16),
        compiler_params=_params("arbitrary"),
        name="rmsnorm",
    )(*x_parts, g.reshape(1, d))


def _mm_body(x_ref, w_ref, *rest, epis, tiles_per_group, n_extra, row_chunks):
    extra = rest[:n_extra]
    outs = rest[n_extra:-1]
    w_scr = rest[-1]

    @pl.when(pl.program_id(1) == 0)
    def _():
        w_scr[...] = w_ref[0].astype(w_scr.dtype)

    tm = x_ref.shape[0]
    rc = tm // row_chunks

    def emit(epi):
        for c in range(row_chunks):
            rows = slice(c * rc, (c + 1) * rc)
            acc = _dot_nt(x_ref[rows, :], w_scr[...])
            res = epi(acc, *[e[rows, :] if e.shape[0] == tm else e[...] for e in extra])
            for o_ref, r in zip(outs, res):
                o_ref[rows, :] = r.astype(o_ref.dtype)

    if len(epis) == 1:
        emit(epis[0])
    else:
        group = pl.program_id(0) // tiles_per_group
        for g, epi in enumerate(epis):
            pl.when(group == g)(functools.partial(emit, epi))


def _mm(x, w_t, epis, outs, *, layer, row0s, n, tm, tn, row_chunks=1, col_args=(), tile_args=(), name):
    rows, k = x.shape
    assert all(r % 8 == 0 for r in row0s) and tn % 8 == 0 and len(epis) == len(row0s)
    assert tm % (16 * row_chunks) == 0
    tpg = n // tn
    n_tiles = tpg * len(row0s)

    def slab_row(j):
        row0 = row0s[0]
        for g in range(1, len(row0s)):
            row0 = jnp.where(j // tpg >= g, row0s[g], row0)
        return pl.multiple_of(row0 + (j % tpg) * tn, 8)

    grid = (n_tiles, rows // tm)
    in_specs = [pl.BlockSpec((tm, k), lambda j, i: (i, 0)),
                pl.BlockSpec((pl.Element(1), pl.Element(tn), pl.Element(k)),
                             lambda j, i: (layer, slab_row(j), 0))]
    args = [x, w_t]
    for arr, width in col_args:
        in_specs.append(pl.BlockSpec((arr.shape[0], width), lambda j, i: (0, j)))
        args.append(arr)
    for arr, width, follows_n in tile_args:
        if follows_n:
            in_specs.append(pl.BlockSpec((tm, width), lambda j, i: (i, j)))
        else:
            in_specs.append(pl.BlockSpec((tm, width), lambda j, i: (i, 0)))
        args.append(arr)
    out_specs = [pl.BlockSpec((tm, width), lambda j, i: (i, j)) for width, _ in outs]
    out_shape = [jax.ShapeDtypeStruct((rows, width * n_tiles), dt) for width, dt in outs]
    res = pl.pallas_call(
        functools.partial(_mm_body, epis=epis, tiles_per_group=tpg, n_extra=len(col_args) + len(tile_args),
                          row_chunks=row_chunks),
        grid=grid,
        in_specs=in_specs,
        out_specs=out_specs,
        out_shape=out_shape,
        scratch_shapes=[pltpu.VMEM((tn, k), BF16)],
        compiler_params=_params("parallel", "arbitrary"),
        name=name,
    )(*args)
    return res


def _mm_rows_body(x_ref, w_ref, *rest, epi, n_extra, tn, widths):
    extra = [e[...] for e in rest[:n_extra]]
    outs = rest[n_extra:]
    x = x_ref[...]
    for p in range(w_ref.shape[1] // tn):
        res = epi(_dot(x, w_ref[:, p * tn:(p + 1) * tn]), *extra)
        for o_ref, r, width in zip(outs, res, widths):
            o_ref[:, p * width:(p + 1) * width] = r.astype(o_ref.dtype)


def _mm_rows(x, w, epi, outs, *, layer, tm, tn, rows=None, tile_args=(), name):
    k = x.shape[1]
    rows = x.shape[0] if rows is None else rows
    n = w.shape[2]
    in_specs = [pl.BlockSpec((tm, k), lambda i: (i, 0)),
                pl.BlockSpec((None, k, n), lambda i: (layer, 0, 0))]
    in_specs += [pl.BlockSpec((tm, width), lambda i: (i, 0)) for _, width in tile_args]
    return pl.pallas_call(
        functools.partial(_mm_rows_body, epi=epi, n_extra=len(tile_args), tn=tn,
                          widths=[width for width, _ in outs]),
        grid=(rows // tm,),
        in_specs=in_specs,
        out_specs=[pl.BlockSpec((tm, width * (n // tn)), lambda i: (i, 0)) for width, _ in outs],
        out_shape=[jax.ShapeDtypeStruct((rows, width * (n // tn)), dt) for width, dt in outs],
        compiler_params=_params("parallel"),
        name=name,
    )(x, w, *[arr for arr, _ in tile_args])


def _silu(z):
    return z * jax.nn.sigmoid(z)


def _epi_silu(acc):
    return (_silu(acc),)


def _epi_sigmoid(acc):
    return (jax.nn.sigmoid(acc),)


def _epi_id(acc):
    return (acc,)


def _epi_relu2(acc):
    r = jnp.maximum(acc, 0.0)
    return (r * r,)


def _epi_forget(acc, lb_logits, *, layer):
    mx = jnp.max(lb_logits, axis=0, keepdims=True)
    e = jnp.exp(lb_logits - mx)
    sm = e / jnp.sum(e, axis=0, keepdims=True)
    lb = jnp.zeros_like(mx)
    for i in range(1, layer + 1):
        lb = lb + sm[i:i + 1]
    z = acc
    e = jnp.exp(-jnp.abs(z))
    r = 1.0 / (1.0 + e)
    sig_neg = jnp.where(z >= 0.0, e * r, r)
    log_sig = jnp.minimum(z, 0.0) - jnp.log(1.0 + e)
    a = jnp.log(lb)
    c = jnp.log1p(-lb) + log_sig
    hi = jnp.maximum(a, c)
    lo = jnp.minimum(a, c)
    logf = hi + jnp.log(1.0 + jnp.exp(lo - hi))
    kk = (1.0 - lb) * sig_neg
    return kk, logf


def _epi_q(acc, cos_t, sin_t):
    n0 = acc[:, 0 * LANES:1 * LANES] * Q_SCALE
    p0 = acc[:, 1 * LANES:2 * LANES] * cos_t + acc[:, 4 * LANES:5 * LANES] * sin_t
    n1 = acc[:, 2 * LANES:3 * LANES] * Q_SCALE
    p1 = acc[:, 3 * LANES:4 * LANES] * cos_t + acc[:, 5 * LANES:6 * LANES] * sin_t
    return (jnp.concatenate([n0, p0, n1, p1], axis=1),)


def _epi_kv(acc, krot):
    krot = krot.astype(F32)
    keys = jnp.concatenate([acc[:, 0:LANES], krot, acc[:, LANES:2 * LANES], krot], axis=1)
    return keys, acc[:, 2 * LANES:4 * LANES]


def _latent_body(h_ref, w_ref, gq_ref, gkv_ref, cos_ref, sin_ref,
                 qn_ref, ckv_ref, ckvb_ref, krot_ref, krotb_ref, w_scr):
    base = Q_LORA + KV_LORA
    half = QK_ROPE // 2

    @pl.when(pl.program_id(0) == 0)
    def _():
        w_scr[...] = jnp.zeros_like(w_scr)
        w_scr[0:base + QK_ROPE, :] = w_ref[...].astype(w_scr.dtype)
        w_scr[base + LANES:base + LANES + half, :] = w_ref[base + half:base + QK_ROPE, :].astype(w_scr.dtype)
        w_scr[base + LANES + half:base + LANES + QK_ROPE, :] = w_ref[base:base + half, :].astype(w_scr.dtype)

    acc = _dot_nt(h_ref[...], w_scr[...])
    qn_ref[...] = _rms(acc[:, :Q_LORA], gq_ref[...]).astype(qn_ref.dtype)
    ckv = _rms(acc[:, Q_LORA:base], gkv_ref[...])
    ckv_ref[...] = ckv
    ckvb_ref[...] = ckv.astype(ckvb_ref.dtype)
    kr = acc[:, base:base + LANES] * cos_ref[...] + acc[:, base + LANES:base + 2 * LANES] * sin_ref[...]
    krot_ref[...] = kr[:, :QK_ROPE]
    krotb_ref[...] = kr.astype(krotb_ref.dtype)


def _latent(h, w_t, gq, gkv, cos_t, sin_t, layer, tm):
    rows, d = h.shape
    n_in = Q_LORA + KV_LORA + QK_ROPE
    row = lambda width: pl.BlockSpec((tm, width), lambda i: (i, 0))
    const = lambda r, width: pl.BlockSpec((r, width), lambda i: (0, 0))
    return pl.pallas_call(
        _latent_body,
        grid=(rows // tm,),
        in_specs=[row(d), pl.BlockSpec((None, n_in, d), lambda i: (layer, 0, 0)),
                  const(1, Q_LORA), const(1, KV_LORA), row(LANES), row(LANES)],
        out_specs=[row(Q_LORA), row(KV_LORA), row(KV_LORA), row(QK_ROPE), row(LANES)],
        out_shape=[jax.ShapeDtypeStruct((rows, Q_LORA), BF16),
                   jax.ShapeDtypeStruct((rows, KV_LORA), F32),
                   jax.ShapeDtypeStruct((rows, KV_LORA), BF16),
                   jax.ShapeDtypeStruct((rows, QK_ROPE), F32),
                   jax.ShapeDtypeStruct((rows, LANES), BF16)],
        scratch_shapes=[pltpu.VMEM((Q_LORA + KV_LORA + 2 * LANES, d), BF16)],
        compiler_params=_params("arbitrary"),
        name="latent_proj",
    )(h, w_t, gq.reshape(1, -1), gkv.reshape(1, -1), cos_t, sin_t)


def _attn_body(q_ref, k_ref, v_ref, o_ref, *, tq, nq):
    k_chunk = lax.broadcasted_iota(jnp.int32, (tq, tq), 0) // CHUNK
    q_chunk = lax.broadcasted_iota(jnp.int32, (tq, tq), 1) // CHUNK
    visible = k_chunk <= q_chunk
    v_t = v_ref[...].T

    scores = []
    for n_past in range(nq):
        lo = n_past * tq
        q = q_ref[lo:lo + tq, :]
        s_d = jnp.where(visible, _dot_nt(k_ref[lo:lo + tq, :], q), NEG_BIG)
        s_p = _dot_nt(k_ref[0:lo, :], q) if n_past else None
        scores.append((s_d, s_p))
    probs = []
    for s_d, s_p in scores:
        m = jnp.max(s_d, axis=0, keepdims=True)
        if s_p is not None:
            m = jnp.maximum(m, jnp.max(s_p, axis=0, keepdims=True))
        p_d = jnp.exp2(s_d - m)
        l = jnp.sum(p_d, axis=0, keepdims=True)
        p_p = None
        if s_p is not None:
            p_p = jnp.exp2(s_p - m)
            l = l + jnp.sum(p_p, axis=0, keepdims=True)
            p_p = p_p.astype(BF16)
        probs.append((p_d.astype(BF16), p_p, l))
    for n_past, (p_d, p_p, l) in enumerate(probs):
        lo = n_past * tq
        acc = _dot(v_t[:, lo:lo + tq], p_d)
        if p_p is not None:
            acc = acc + _dot(v_t[:, 0:lo], p_p)
        o_ref[lo:lo + tq, :] = (acc / l).T.astype(o_ref.dtype)


def _prompt_attention(q, k, v, batch, seq, tq):
    return pl.pallas_call(
        functools.partial(_attn_body, tq=tq, nq=seq // tq),
        grid=(batch, MLA_HEADS),
        in_specs=[pl.BlockSpec((seq, QK_PAD), lambda b, h: (b, h)),
                  pl.BlockSpec((seq, QK_PAD), lambda b, h: (b, h)),
                  pl.BlockSpec((seq, V_HEAD), lambda b, h: (b, h))],
        out_specs=pl.BlockSpec((seq, V_HEAD), lambda b, h: (b, h)),
        out_shape=jax.ShapeDtypeStruct((batch * seq, MLA_HEADS * V_HEAD), BF16),
        compiler_params=_params("parallel", "parallel"),
        name="prompt_attention",
    )(q, k, v)


def _sattn_body(q_ref, cn_ref, kn_ref, cc_ref, ck_ref, wuk_ref, wuv_ref, o_ref, qa_scr, qp_scr,
                *, dec_seq, key_tile):
    for h in range(MLA_HEADS):
        qn = q_ref[:, h * QK_PAD:h * QK_PAD + QK_NOPE]
        qa_scr[h * dec_seq:(h + 1) * dec_seq, :] = _dot(qn, wuk_ref[h]).astype(BF16)
        qp_scr[h * dec_seq:(h + 1) * dec_seq, :] = q_ref[:, h * QK_PAD + QK_NOPE:(h + 1) * QK_PAD]
    qa = qa_scr[...]
    qp = qp_scr[...]
    rows = MLA_HEADS * dec_seq

    def update(carry, s, c):
        m, l, acc = carry
        m_new = jnp.maximum(m, jnp.max(s, axis=-1, keepdims=True))
        alpha = jnp.exp2(m - m_new)
        p = jnp.exp2(s - m_new)
        l = alpha * l + jnp.sum(p, axis=-1, keepdims=True)
        acc = alpha * acc + _dot(p.astype(BF16), c)
        return m_new, l, acc

    carry = (jnp.full((rows, 1), NEG_BIG, F32), jnp.zeros((rows, 1), F32), jnp.zeros((rows, KV_LORA), F32))
    past = cc_ref.shape[0]
    for t in range(past // key_tile):
        c = cc_ref[t * key_tile:(t + 1) * key_tile, :].astype(BF16)
        kp_t = ck_ref[:, t * key_tile:(t + 1) * key_tile].astype(BF16)
        s = _dot_nt(qa, c) + _dot(qp[:, :QK_ROPE], kp_t)
        carry = update(carry, s, c)
    cn = cn_ref[...]
    s = _dot_nt(qa, cn) + _dot_nt(qp, kn_ref[...])
    _, l, acc = update(carry, s, cn)
    o_lat = (acc / l).astype(BF16)
    for h in range(MLA_HEADS):
        o_ref[:, h * V_HEAD:(h + 1) * V_HEAD] = _dot(
            o_lat[h * dec_seq:(h + 1) * dec_seq, :], wuv_ref[h]).astype(o_ref.dtype)


def _sample_attention(q, ckv_b, krot_b, cache_c, cache_k, wuk_t, wuv_h, layer, n_prompt_rows, dec_batch, dec_seq):
    rb = n_prompt_rows // dec_seq
    past = cache_c.shape[2]
    key_tile = min(past, 1024)
    per_layer = lambda s: pl.BlockSpec((None,) + s, lambda b: (layer, 0, 0, 0))
    return pl.pallas_call(
        functools.partial(_sattn_body, dec_seq=dec_seq, key_tile=key_tile),
        grid=(dec_batch,),
        in_specs=[pl.BlockSpec((dec_seq, MLA_HEADS * QK_PAD), lambda b: (rb + b, 0)),
                  pl.BlockSpec((dec_seq, KV_LORA), lambda b: (rb + b, 0)),
                  pl.BlockSpec((dec_seq, LANES), lambda b: (rb + b, 0)),
                  pl.BlockSpec((None, None, past, KV_LORA), lambda b: (layer, b, 0, 0)),
                  pl.BlockSpec((None, None, QK_ROPE, past), lambda b: (layer, b, 0, 0)),
                  per_layer((MLA_HEADS, QK_NOPE, KV_LORA)),
                  per_layer((MLA_HEADS, KV_LORA, V_HEAD))],
        out_specs=pl.BlockSpec((dec_seq, MLA_HEADS * V_HEAD), lambda b: (b, 0)),
        out_shape=jax.ShapeDtypeStruct((dec_batch * dec_seq, MLA_HEADS * V_HEAD), BF16),
        scratch_shapes=[pltpu.VMEM((MLA_HEADS * dec_seq, KV_LORA), BF16),
                        pltpu.VMEM((MLA_HEADS * dec_seq, LANES), BF16)],
        compiler_params=_params("parallel"),
        name="sample_attention",
    )(q, ckv_b, krot_b, cache_c, cache_k, wuk_t, wuv_h)


def _hgrn_tables(length):
    t = np.arange(length)[:, None]
    r = np.arange(length)[None, :]
    groups = [(r <= t)]
    masks = [(r == t)]
    m = length // 2
    while m >= 1:
        blk = t // (2 * m)
        start2 = blk * 2 * m + m
        second = (t % (2 * m)) >= m
        incl = second & (r >= start2) & (r <= t)
        excl = (~second) & (r > t) & (r < start2)
        if m % 8:
            groups.append(incl | excl)
        r_blk = r // (2 * m)
        r_first = (r % (2 * m)) < m
        masks.append(second & r_first & (r_blk == blk))
        m //= 2
    return (np.concatenate(groups, axis=0).astype(np.float32),
            np.stack(masks, axis=0).astype(np.float32))


def _hgrn_body(*refs, length, n_chunks, group, has_state):
    if has_state:
        q_ref, k_ref, lf_ref, v_ref, g_ref, gn_ref, sum_ref, mask_ref, s0_ref, o_ref, sout_ref, st_scr = refs
    else:
        q_ref, k_ref, lf_ref, v_ref, g_ref, gn_ref, sum_ref, mask_ref, o_ref, sout_ref, st_scr = refs
    tb = pl.program_id(2)

    @pl.when(tb == 0)
    def _():
        for g in range(group):
            st_scr[g] = s0_ref[g].T if has_state else jnp.zeros((HG_DV, HG_DK), F32)

    n_lev = mask_ref.shape[0] - 1
    summat = sum_ref[...]
    gn = gn_ref[...]

    pack_levels = length % LANES == 0 and (n_lev + 1) % 2 == 0
    zeros_k = jnp.zeros((length, HG_DK), BF16)

    def mixed_rows(q, k, half):
        parts = []
        for r in range(0, length, half):
            parts.append((q if (r // half) % 2 else k)[r:r + half])
        return jnp.concatenate(parts, axis=0)

    halves = [length >> (lev + 1) for lev in range(n_lev)]
    coarse = [half for half in halves if half % 8 == 0]

    def exponentials(sums):
        b = sums[0:length]
        levels = []
        for half in coarse:
            parts = []
            for r in range(0, length, half):
                ref = (r // (2 * half)) * 2 * half + half - 1
                ref_row = b[ref:ref + 1]
                parts.append(b[r:r + half] - ref_row if (r // half) % 2 else ref_row - b[r:r + half])
            levels.append(jnp.concatenate(parts, axis=0))
        for i in range(n_lev - len(coarse)):
            levels.append(sums[(1 + i) * length:(2 + i) * length])
        e_in = jnp.exp2(b)
        e_out = jnp.exp2(b[length - 1:length] - b)
        return e_in, e_out, [jnp.exp2(x) for x in levels]

    def level_operands(rows, g, e_in, e_out, e_levels):
        cols = slice(g * HG_DK, (g + 1) * HG_DK)
        qs = [q_ref[rows, cols]]
        ks = [k_ref[rows, cols]]
        q = qs[0].astype(F32)
        k = ks[0].astype(F32)
        for half, e in zip(halves, e_levels):
            if half % 8 == 0:
                u = (mixed_rows(q, k, half) * e).astype(BF16)
                qs.append(u)
                ks.append(u)
            else:
                qs.append((q * e).astype(BF16))
                ks.append((k * e).astype(BF16))
        return qs, ks, (q * e_in).astype(BF16), (k * e_out).astype(BF16)

    def level_scores(qs, ks):
        if pack_levels:
            out = []
            for p in range(0, n_lev + 1, 2):
                lhs = jnp.concatenate([qs[p], qs[p + 1]], axis=1)
                rhs = jnp.concatenate([jnp.concatenate([ks[p], zeros_k], axis=1),
                                       jnp.concatenate([zeros_k, ks[p + 1]], axis=1)], axis=0)
                a = _dot_nt(lhs, rhs)
                out += [a[:, :length], a[:, length:]]
            return out
        return [_dot_nt(qs[p], ks[p]) for p in range(n_lev + 1)]

    def chunk(c, _):
        rows = pl.ds(pl.multiple_of(c * length, length), length)
        e_heads = []
        for pair in range(group // 2):
            cols2 = slice(2 * pair * HG_DK, (2 * pair + 2) * HG_DK)
            lf2 = lf_ref[rows, cols2] * LOG2_E
            hi = lf2.astype(BF16)
            mid = (lf2 - hi.astype(F32)).astype(BF16)
            e_in, e_out, e_levels = exponentials(_dot(summat, jnp.concatenate([hi, mid], axis=0)))
            for lanes in (slice(0, HG_DK), slice(HG_DK, 2 * HG_DK)):
                e_heads.append((e_in[:, lanes], e_out[:, lanes], [e[:, lanes] for e in e_levels]))
        work = []
        for g in range(group):
            qs, ks, q_in, k_out = level_operands(rows, g, *e_heads[g])
            st = st_scr[g]
            work.append((level_scores(qs, ks), _dot_nt(q_in, st.astype(BF16)), k_out, st))
        for g in range(group):
            cols = slice(g * HG_DK, (g + 1) * HG_DK)
            scores, o_in, k_out, st = work[g]
            v = v_ref[rows, cols]
            att = mask_ref[0] * scores[0]
            for p in range(1, n_lev + 1):
                att = att + mask_ref[p] * scores[p]
            o = o_in + _dot(att.astype(BF16), v)
            e_last = e_heads[g][0][length - 1:length, :]
            st_scr[g] = st * e_last + _dot_tn(v, k_out)
            o_ref[rows, cols] = (_rms(o, gn) * g_ref[rows, cols]).astype(o_ref.dtype)
        return 0

    lax.fori_loop(0, n_chunks, chunk, 0, unroll=2 if n_chunks % 2 == 0 else 1)

    @pl.when(tb == pl.num_programs(2) - 1)
    def _():
        for g in range(group):
            sout_ref[g] = st_scr[g].T


def _hgrn(q, k, logf, v, gate, gn, state0, *, layer, row_block0, n_streams, stream_len, length, block_len, group):
    n_tb = stream_len // block_len
    n_chunks = block_len // length
    width = group * HG_DK
    summat, masks = _hgrn_tables(length)
    summat = jnp.asarray(np.concatenate([summat, summat], axis=1), BF16)
    masks = jnp.asarray(masks, F32)
    tok = lambda col0: pl.BlockSpec((block_len, width),
                                    lambda n, h, t: (row_block0 + n * n_tb + t, col0 // width + h))
    tokens = (q, k, logf, v, gate)
    st_spec = pl.BlockSpec((None, group, HG_DK, HG_DV), lambda n, h, t: (n, h, 0, 0))
    in_specs = [tok(col0) for _, col0 in tokens] + [pl.BlockSpec((1, HG_DV), lambda n, h, t: (0, 0)),
                pl.BlockSpec(summat.shape, lambda n, h, t: (0, 0)),
                pl.BlockSpec(masks.shape, lambda n, h, t: (0, 0, 0))]
    args = [arr for arr, _ in tokens] + [gn.reshape(1, HG_DV), summat, masks]
    if state0 is not None:
        in_specs.append(pl.BlockSpec((None, None, group, HG_DK, HG_DV), lambda n, h, t: (layer, n, h, 0, 0)))
        args.append(state0)
    return pl.pallas_call(
        functools.partial(_hgrn_body, length=length, n_chunks=n_chunks, group=group,
                          has_state=state0 is not None),
        grid=(n_streams, HG_HEADS // group, n_tb),
        in_specs=in_specs,
        out_specs=[pl.BlockSpec((block_len, width), lambda n, h, t: (n * n_tb + t, h)), st_spec],
        out_shape=[jax.ShapeDtypeStruct((n_streams * stream_len, HG_HEADS * HG_DV), BF16),
                   jax.ShapeDtypeStruct((n_streams, HG_HEADS, HG_DK, HG_DV), F32)],
        scratch_shapes=[pltpu.VMEM((group, HG_DV, HG_DK), F32)],
        compiler_params=_params("parallel", "parallel", "arbitrary"),
        name="hgrn_state" if state0 is not None else "hgrn_prompt",
    )(*args)


def _merge_body(*refs, tm, n_a):
    oa_parts, oh_parts = refs[:n_a], refs[n_a:-5]
    woa_ref, wob_ref, ga_ref, gb_ref, o_ref = refs[-5:]

    def emit(tiles):
        a = _dot(tiles[0], woa_ref[...])
        b = _dot(tiles[1], wob_ref[...])
        o_ref[...] = (ga_ref[...] * a + gb_ref[...] * b).astype(o_ref.dtype)

    _on_row_tiles([oa_parts, oh_parts], 1, tm, emit)


def _merge(oa_parts, oh_parts, woa, wob, ga, gb, layer, tm, tn):
    rows, k = _stacked_rows(oa_parts), oa_parts[0].shape[1]
    n = woa.shape[2]
    x_specs = lambda parts: _part_specs(parts, tm, lambda j, i: (i, 0), lambda j, i: (0, 0))
    w_spec = pl.BlockSpec((None, k, tn), lambda j, i: (layer, 0, j))
    t_spec = pl.BlockSpec((tm, tn), lambda j, i: (i, j))
    return pl.pallas_call(
        functools.partial(_merge_body, tm=tm, n_a=len(oa_parts)),
        grid=(n // tn, rows // tm),
        in_specs=x_specs(oa_parts) + x_specs(oh_parts) + [w_spec, w_spec]
        + [pl.BlockSpec((tm, tn), lambda j, i, c=col0 // tn: (i, c + j)) for _, col0 in (ga, gb)],
        out_specs=t_spec,
        out_shape=jax.ShapeDtypeStruct((rows, n), BF16),
        compiler_params=_params("parallel", "arbitrary"),
        name="gated_merge",
    )(*oa_parts, *oh_parts, woa, wob, ga[0], gb[0])


def _outproj_body(m_ref, w_ref, *refs, tm):
    x_parts, (g1_ref, g2_ref, x1_ref, h2_ref) = refs[:-4], refs[-4:]

    def emit(tiles):
        rc = tm // 2
        for c in range(2):
            rows = slice(c * rc, (c + 1) * rc)
            y = _dot(m_ref[rows, :], w_ref[...])
            x1 = tiles[0][rows, :] + _rms(y, g1_ref[...])
            x1_ref[rows, :] = x1
            h2_ref[rows, :] = _rms(x1, g2_ref[...]).astype(h2_ref.dtype)

    _on_row_tiles([x_parts], 0, tm, emit)


def _outproj(mix, w, x_parts, g_post, g_pre2, layer, tm):
    rows, d = mix.shape
    row = pl.BlockSpec((tm, d), lambda i: (i, 0))
    vec = pl.BlockSpec((1, d), lambda i: (0, 0))
    return pl.pallas_call(
        functools.partial(_outproj_body, tm=tm),
        grid=(rows // tm,),
        in_specs=[row, pl.BlockSpec((None, d, d), lambda i: (layer, 0, 0))]
        + _part_specs(x_parts, tm, lambda i: (i, 0), lambda i: (0, 0)) + [vec, vec],
        out_specs=[row, row],
        out_shape=[jax.ShapeDtypeStruct((rows, d), F32), jax.ShapeDtypeStruct((rows, d), BF16)],
        compiler_params=_params("arbitrary"),
        name="out_proj",
    )(mix, w, *x_parts, g_post.reshape(1, d), g_pre2.reshape(1, d))


def _mlp_body(h_ref, wu_ref, wd_ref, x_ref, g_ref, *rest, tail_start, norm_next):
    rest = list(rest)
    gn_ref = rest.pop(0) if norm_next else None
    o_ref = rest.pop(0)
    tail_ref = rest.pop(0) if tail_start is not None else None
    hn_ref = rest.pop(0) if norm_next else None
    acc_ref = rest.pop(0)
    f = pl.program_id(1)

    @pl.when(f == 0)
    def _():
        acc_ref[...] = jnp.zeros_like(acc_ref)

    u = jnp.maximum(_dot(h_ref[...], wu_ref[...]), 0.0)
    acc_ref[...] += _dot((u * u).astype(BF16), wd_ref[...])

    @pl.when(f == pl.num_programs(1) - 1)
    def _():
        y = x_ref[...] + _rms(acc_ref[...], g_ref[...])
        o_ref[...] = y
        if norm_next:
            hn_ref[...] = _rms(y, gn_ref[...]).astype(hn_ref.dtype)
        if tail_start is not None:
            @pl.when(pl.program_id(0) == pl.num_programs(0) - 1)
            def _():
                tail_ref[...] = y[tail_start:, :]


def _mlp(h2, w_up, w_down, x1, g, layer, tm, tf, split_rows=None, next_g=None):
    rows, d = x1.shape
    ff = w_up.shape[2]
    row = pl.BlockSpec((tm, d), lambda i, f: (i, 0))
    vec = pl.BlockSpec((1, d), lambda i, f: (0, 0))
    out_specs, out_shape, tail_start = [row], [jax.ShapeDtypeStruct((rows, d), F32)], None
    if split_rows is not None:
        tail = rows - split_rows
        tail_start = tm - tail
        assert 0 <= tail_start and tail % 8 == 0
        out_specs.append(pl.BlockSpec((tail, d), lambda i, f: (0, 0)))
        out_shape = [jax.ShapeDtypeStruct((split_rows, d), F32), jax.ShapeDtypeStruct((tail, d), F32)]
    in_specs = [row, pl.BlockSpec((None, d, tf), lambda i, f: (layer, 0, f)),
                pl.BlockSpec((None, tf, d), lambda i, f: (layer, f, 0)), row, vec]
    args = [h2, w_up, w_down, x1, g.reshape(1, d)]
    if next_g is not None:
        in_specs.append(vec)
        args.append(next_g.reshape(1, d))
        out_specs.append(row)
        out_shape.append(jax.ShapeDtypeStruct((rows, d), BF16))
    return pl.pallas_call(
        functools.partial(_mlp_body, tail_start=tail_start, norm_next=next_g is not None),
        grid=(rows // tm, ff // tf),
        in_specs=in_specs,
        out_specs=out_specs,
        out_shape=out_shape,
        scratch_shapes=[pltpu.VMEM((tm, d), F32)],
        compiler_params=_params("arbitrary", "arbitrary"),
        name="mlp",
    )(*args)


def _pad_lanes(w):
    return jnp.concatenate([w, jnp.zeros(w.shape[:-1] + (LANES - w.shape[-1],), w.dtype)], axis=-1)


def _swap_halves(w):
    half = w.shape[-1] // 2
    return jnp.concatenate([w[..., half:], w[..., :half]], axis=-1)


def _prep_weights(w_uq, w_uk, w_uv):
    depth = w_uq.shape[0]
    uq = w_uq.reshape(depth, Q_LORA, MLA_HEADS // 2, 2, QK_HEAD)
    nope = uq[..., :QK_NOPE]
    pe = _pad_lanes(uq[..., QK_NOPE:])
    pe_sw = _pad_lanes(_swap_halves(uq[..., QK_NOPE:]))
    w_q = jnp.concatenate([nope[:, :, :, 0], pe[:, :, :, 0], nope[:, :, :, 1], pe[:, :, :, 1],
                           pe_sw[:, :, :, 0], pe_sw[:, :, :, 1]], axis=-1)
    w_q = w_q.reshape(depth, Q_LORA, -1).astype(BF16)
    uk = w_uk.reshape(depth, KV_LORA, MLA_HEADS // 2, 2 * QK_NOPE)
    uv = w_uv.reshape(depth, KV_LORA, MLA_HEADS // 2, 2 * V_HEAD)
    w_kv = jnp.concatenate([uk, uv], axis=-1).reshape(depth, KV_LORA, -1).astype(BF16)
    wuk_t = jnp.transpose(w_uk, (0, 2, 3, 1)).astype(BF16)
    wuv_h = jnp.transpose(w_uv, (0, 2, 1, 3)).astype(BF16)
    return w_q, w_kv, wuk_t, wuv_h


def _rope_tables(positions):
    half = QK_ROPE // 2
    inv = ROPE_THETA ** (-jnp.arange(half, dtype=F32) / half)
    ang = positions.astype(F32)[:, None] * inv[None, :]
    cos, sin = jnp.cos(ang), jnp.sin(ang)
    zeros = jnp.zeros((positions.shape[0], LANES - QK_ROPE), F32)
    return (jnp.concatenate([cos, cos, zeros], axis=1), jnp.concatenate([-sin, sin, zeros], axis=1))


def kernel(x_prompt, x_sample, cache_ckv, cache_kpe, state_hgrn, pre_mix_g, w_in, q_norm_g, w_uq, kv_norm_g,
           w_uk, w_uv, w_oa, hg_lb, hg_norm_g, w_ob, w_out, post_mix_g, pre_mlp_g, w_up, w_down, post_mlp_g):
    batch, seq, d = x_prompt.shape
    dec_batch, dec_seq, _ = x_sample.shape
    depth = w_in.shape[0]
    past = cache_ckv.shape[2]
    n_p = batch * seq
    n_s = dec_batch * dec_seq
    rows = n_p + n_s
    tm = 768
    tm_wide = 1408
    tm_mlp = 704
    assert rows % tm == 0 and rows % tm_wide == 0 and rows % tm_mlp == 0 and n_p % 1024 == 0

    x_parts = (x_prompt.reshape(n_p, d), x_sample.reshape(n_s, d))
    pos = jnp.concatenate([jnp.tile(jnp.arange(seq, dtype=jnp.int32), batch),
                           jnp.tile(past + jnp.arange(dec_seq, dtype=jnp.int32), dec_batch)])
    cos_t, sin_t = _rope_tables(pos)
    cos_q, sin_q = cos_t * Q_SCALE, sin_t * Q_SCALE

    w_in_t = jnp.transpose(w_in, (0, 2, 1))
    cache_kpe_t = jnp.transpose(cache_kpe, (0, 1, 3, 2))
    wide0 = Q_LORA + KV_LORA + QK_ROPE
    w_q, w_kv, wuk_t, wuv_h = _prep_weights(w_uq, w_uk, w_uv)
    w_oa_b, w_ob_b, w_out_b = w_oa.astype(BF16), w_ob.astype(BF16), w_out.astype(BF16)
    w_up_b, w_down_b = w_up.astype(BF16), w_down.astype(BF16)

    ckv_out, kpe_out, st_p_out, st_s_out = [], [], [], []
    for l in range(depth):
        h = _rmsnorm(x_parts, pre_mix_g[l], tm)
        qn, ckv, ckv_b, krot, krot_b = _latent(h, w_in_t, q_norm_g[l], kv_norm_g[l], cos_t, sin_t, l, tm)
        hk, logf = _mm(h, w_in_t, [functools.partial(_epi_forget, layer=l)], [(1024, BF16), (1024, F32)],
                       layer=l, row0s=[wide0 + d], n=d, tm=tm_wide, tn=1024, row_chunks=4,
                       col_args=[(hg_lb, 1024)], name="proj_hf")
        hq, hv, hgate, ga, gb = [
            (*_mm(h, w_in_t, [epi], [(1024, BF16)], layer=l, row0s=[wide0 + g * d], n=d, tm=tm_wide, tn=1024,
                  row_chunks=4, name=name), 0)
            for g, epi, name in ((0, _epi_silu, "proj_hq"), (2, _epi_id, "proj_hi"), (3, _epi_silu, "proj_hg"),
                                 (4, _epi_sigmoid, "proj_ga"), (5, _epi_sigmoid, "proj_gb"))]
        hk, logf = [hk, 0], [logf, 0]
        (q,) = _mm_rows(qn, w_q, _epi_q, [(2 * QK_PAD, BF16)], layer=l, tm=tm, tn=6 * LANES,
                        tile_args=[(cos_q, LANES), (sin_q, LANES)], name="proj_q")
        keys, vals = _mm_rows(ckv_b, w_kv, _epi_kv, [(2 * QK_PAD, BF16), (2 * V_HEAD, BF16)], layer=l, tm=1024,
                              tn=4 * LANES, rows=n_p, tile_args=[(krot_b, LANES)], name="proj_kv")
        oa_p = _prompt_attention(q, keys, vals, batch, seq, 256)
        oa_s = _sample_attention(q, ckv_b, krot_b, cache_ckv, cache_kpe_t, wuk_t, wuv_h, l, n_p, dec_batch, dec_seq)
        oh_p, st_p = _hgrn(hq, hk, logf, hv, hgate, hg_norm_g[l], None, layer=l,
                           row_block0=0, n_streams=batch, stream_len=seq, length=2 * CHUNK,
                           block_len=1024, group=8)
        oh_s, st_s = _hgrn(hq, hk, logf, hv, hgate, hg_norm_g[l], state_hgrn, layer=l,
                           row_block0=n_p // dec_seq, n_streams=dec_batch, stream_len=dec_seq, length=dec_seq,
                           block_len=dec_seq, group=4)
        mix = _merge((oa_p, oa_s), (oh_p, oh_s), w_oa_b, w_ob_b, ga, gb, l, tm, 1024)
        x1, h2 = _outproj(mix, w_out_b, x_parts, post_mix_g[l], pre_mlp_g[l], l, 384)
        if l + 1 < depth:
            x_parts = tuple(_mlp(h2, w_up_b, w_down_b, x1, post_mlp_g[l], l, tm_mlp, 1024))
        else:
            y_p, y_s = _mlp(h2, w_up_b, w_down_b, x1, post_mlp_g[l], l, tm, 512, split_rows=n_p)

        ckv_out.append(ckv)
        kpe_out.append(krot)
        st_p_out.append(st_p)
        st_s_out.append(st_s)

    ckv_all = jnp.stack(ckv_out)
    kpe_all = jnp.stack(kpe_out)
    return (y_p.reshape(batch, seq, d),
            y_s.reshape(dec_batch, dec_seq, d),
            ckv_all[:, :n_p].reshape(depth, batch, seq, KV_LORA),
            kpe_all[:, :n_p].reshape(depth, batch, seq, QK_ROPE),
            jnp.stack(st_p_out),
            ckv_all[:, n_p:].reshape(depth, dec_batch, dec_seq, KV_LORA),
            kpe_all[:, n_p:].reshape(depth, dec_batch, dec_seq, QK_ROPE),
            jnp.stack(st_s_out))
```

```python
import functools

import jax
import jax.numpy as jnp
import numpy as np
from jax import lax
from jax.experimental import pallas as pl
from jax.experimental.pallas import tpu as pltpu

F32 = jnp.float32
BF16 = jnp.bfloat16

CHUNK = 64
MLA_HEADS = 16
QK_NOPE = 128
QK_ROPE = 64
QK_HEAD = QK_NOPE + QK_ROPE
V_HEAD = 128
Q_LORA = 512
KV_LORA = 512
ROPE_THETA = 10000.0
ATTN_SCALE = QK_HEAD ** -0.5
HG_HEADS = 16
HG_DK = 128
HG_DV = 128
EPS = 1e-6

LANES = 128
QK_PAD = 2 * LANES
NEG_BIG = -1e30
LOG2_E = 1.4426950408889634
Q_SCALE = ATTN_SCALE * LOG2_E
VMEM_LIMIT = 58 * 1024 * 1024


def _params(*sem):
    return pltpu.CompilerParams(dimension_semantics=sem, vmem_limit_bytes=VMEM_LIMIT)


def _rms(x, g):
    return x * lax.rsqrt(jnp.mean(x * x, axis=-1, keepdims=True) + EPS) * g


def _dot(a, b):
    return jnp.dot(a, b, preferred_element_type=F32)


def _dot_nt(a, b):
    return lax.dot_general(a, b, (((1,), (1,)), ((), ())), preferred_element_type=F32)


def _dot_tn(a, b):
    return lax.dot_general(a, b, (((0,), (0,)), ((), ())), preferred_element_type=F32)


def _part_specs(parts, tm, index_map, tail_index_map):
    specs = [pl.BlockSpec((tm, parts[0].shape[1]), index_map)]
    if len(parts) == 2:
        specs.append(pl.BlockSpec(parts[1].shape, tail_index_map))
    return specs


def _stacked_rows(parts):
    return sum(p.shape[0] for p in parts)


def _on_row_tiles(operands, row_axis, tm, fn):
    if all(len(o) == 1 for o in operands):
        fn([o[0][...] for o in operands])
        return
    is_last = pl.program_id(row_axis) == pl.num_programs(row_axis) - 1

    @pl.when(jnp.logical_not(is_last))
    def _():
        fn([o[0][...] for o in operands])

    @pl.when(is_last)
    def _():
        fn([o[0][...] if len(o) == 1 else
            jnp.concatenate([o[0][:tm - o[1].shape[0]], o[1][...]], axis=0) for o in operands])


def _rmsnorm_body(*refs, tm):
    x_parts, (g_ref, o_ref) = refs[:-2], refs[-2:]

    def emit(tiles):
        o_ref[...] = _rms(tiles[0], g_ref[...]).astype(o_ref.dtype)

    _on_row_tiles([x_parts], 0, tm, emit)


def _rmsnorm(x_parts, g, tm):
    rows, d = _stacked_rows(x_parts), x_parts[0].shape[1]
    return pl.pallas_call(
        functools.partial(_rmsnorm_body, tm=tm),
        grid=(rows // tm,),
        in_specs=_part_specs(x_parts, tm, lambda i: (i, 0), lambda i: (0, 0))
        + [pl.BlockSpec((1, d), lambda i: (0, 0))],
        out_specs=pl.BlockSpec((tm, d), lambda i: (i, 0)),
        out_shape=jax.ShapeDtypeStruct((rows, d), BF16),
        compiler_params=_params("arbitrary"),
        name="rmsnorm",
    )(*x_parts, g.reshape(1, d))


def _mm_body(x_ref, w_ref, *rest, epis, tiles_per_group, n_extra, row_chunks):
    extra = rest[:n_extra]
    outs = rest[n_extra:-1]
    w_scr = rest[-1]

    @pl.when(pl.program_id(1) == 0)
    def _():
        w_scr[...] = w_ref[0].astype(w_scr.dtype)

    tm = x_ref.shape[0]
    rc = tm // row_chunks

    def emit(epi):
        for c in range(row_chunks):
            rows = slice(c * rc, (c + 1) * rc)
            acc = _dot_nt(x_ref[rows, :], w_scr[...])
            res = epi(acc, *[e[rows, :] if e.shape[0] == tm else e[...] for e in extra])
            for o_ref, r in zip(outs, res):
                o_ref[rows, :] = r.astype(o_ref.dtype)

    if len(epis) == 1:
        emit(epis[0])
    else:
        group = pl.program_id(0) // tiles_per_group
        for g, epi in enumerate(epis):
            pl.when(group == g)(functools.partial(emit, epi))


def _mm(x, w_t, epis, outs, *, layer, row0s, n, tm, tn, row_chunks=1, col_args=(), tile_args=(), name):
    rows, k = x.shape
    assert all(r % 8 == 0 for r in row0s) and tn % 8 == 0 and len(epis) == len(row0s)
    assert tm % (16 * row_chunks) == 0
    tpg = n // tn
    n_tiles = tpg * len(row0s)

    def slab_row(j):
        row0 = row0s[0]
        for g in range(1, len(row0s)):
            row0 = jnp.where(j // tpg >= g, row0s[g], row0)
        return pl.multiple_of(row0 + (j % tpg) * tn, 8)

    grid = (n_tiles, rows // tm)
    in_specs = [pl.BlockSpec((tm, k), lambda j, i: (i, 0)),
                pl.BlockSpec((pl.Element(1), pl.Element(tn), pl.Element(k)),
                             lambda j, i: (layer, slab_row(j), 0))]
    args = [x, w_t]
    for arr, width in col_args:
        in_specs.append(pl.BlockSpec((arr.shape[0], width), lambda j, i: (0, j)))
        args.append(arr)
    for arr, width, follows_n in tile_args:
        if follows_n:
            in_specs.append(pl.BlockSpec((tm, width), lambda j, i: (i, j)))
        else:
            in_specs.append(pl.BlockSpec((tm, width), lambda j, i: (i, 0)))
        args.append(arr)
    out_specs = [pl.BlockSpec((tm, width), lambda j, i: (i, j)) for width, _ in outs]
    out_shape = [jax.ShapeDtypeStruct((rows, width * n_tiles), dt) for width, dt in outs]
    res = pl.pallas_call(
        functools.partial(_mm_body, epis=epis, tiles_per_group=tpg, n_extra=len(col_args) + len(tile_args),
                          row_chunks=row_chunks),
        grid=grid,
        in_specs=in_specs,
        out_specs=out_specs,
        out_shape=out_shape,
        scratch_shapes=[pltpu.VMEM((tn, k), BF16)],
        compiler_params=_params("parallel", "arbitrary"),
        name=name,
    )(*args)
    return res


def _mm_rows_body(x_ref, w_ref, *rest, epi, n_extra, tn, widths):
    extra = [e[...] for e in rest[:n_extra]]
    outs = rest[n_extra:]
    x = x_ref[...]
    for p in range(w_ref.shape[1] // tn):
        res = epi(_dot(x, w_ref[:, p * tn:(p + 1) * tn]), *extra)
        for o_ref, r, width in zip(outs, res, widths):
            o_ref[:, p * width:(p + 1) * width] = r.astype(o_ref.dtype)


def _mm_rows(x, w, epi, outs, *, layer, tm, tn, rows=None, tile_args=(), name):
    k = x.shape[1]
    rows = x.shape[0] if rows is None else rows
    n = w.shape[2]
    in_specs = [pl.BlockSpec((tm, k), lambda i: (i, 0)),
                pl.BlockSpec((None, k, n), lambda i: (layer, 0, 0))]
    in_specs += [pl.BlockSpec((tm, width), lambda i: (i, 0)) for _, width in tile_args]
    return pl.pallas_call(
        functools.partial(_mm_rows_body, epi=epi, n_extra=len(tile_args), tn=tn,
                          widths=[width for width, _ in outs]),
        grid=(rows // tm,),
        in_specs=in_specs,
        out_specs=[pl.BlockSpec((tm, width * (n // tn)), lambda i: (i, 0)) for width, _ in outs],
        out_shape=[jax.ShapeDtypeStruct((rows, width * (n // tn)), dt) for width, dt in outs],
        compiler_params=_params("parallel"),
        name=name,
    )(x, w, *[arr for arr, _ in tile_args])


def _silu(z):
    return z * jax.nn.sigmoid(z)


def _epi_silu(acc):
    return (_silu(acc),)


def _epi_sigmoid(acc):
    return (jax.nn.sigmoid(acc),)


def _epi_id(acc):
    return (acc,)


def _epi_relu2(acc):
    r = jnp.maximum(acc, 0.0)
    return (r * r,)


def _epi_forget(acc, lb_logits, *, layer):
    mx = jnp.max(lb_logits, axis=0, keepdims=True)
    e = jnp.exp(lb_logits - mx)
    sm = e / jnp.sum(e, axis=0, keepdims=True)
    lb = jnp.zeros_like(mx)
    for i in range(1, layer + 1):
        lb = lb + sm[i:i + 1]
    z = acc
    e = jnp.exp(-jnp.abs(z))
    r = 1.0 / (1.0 + e)
    sig_neg = jnp.where(z >= 0.0, e * r, r)
    log_sig = jnp.minimum(z, 0.0) - jnp.log(1.0 + e)
    a = jnp.log(lb)
    c = jnp.log1p(-lb) + log_sig
    hi = jnp.maximum(a, c)
    lo = jnp.minimum(a, c)
    logf = hi + jnp.log(1.0 + jnp.exp(lo - hi))
    kk = (1.0 - lb) * sig_neg
    return kk, logf


def _epi_q(acc, cos_t, sin_t):
    n0 = acc[:, 0 * LANES:1 * LANES] * Q_SCALE
    p0 = acc[:, 1 * LANES:2 * LANES] * cos_t + acc[:, 4 * LANES:5 * LANES] * sin_t
    n1 = acc[:, 2 * LANES:3 * LANES] * Q_SCALE
    p1 = acc[:, 3 * LANES:4 * LANES] * cos_t + acc[:, 5 * LANES:6 * LANES] * sin_t
    return (jnp.concatenate([n0, p0, n1, p1], axis=1),)


def _epi_kv(acc, krot):
    krot = krot.astype(F32)
    keys = jnp.concatenate([acc[:, 0:LANES], krot, acc[:, LANES:2 * LANES], krot], axis=1)
    return keys, acc[:, 2 * LANES:4 * LANES]


def _latent_body(h_ref, w_ref, gq_ref, gkv_ref, cos_ref, sin_ref,
                 qn_ref, ckv_ref, ckvb_ref, krot_ref, krotb_ref, w_scr):
    base = Q_LORA + KV_LORA
    half = QK_ROPE // 2

    @pl.when(pl.program_id(0) == 0)
    def _():
        w_scr[...] = jnp.zeros_like(w_scr)
        w_scr[0:base + QK_ROPE, :] = w_ref[...].astype(w_scr.dtype)
        w_scr[base + LANES:base + LANES + half, :] = w_ref[base + half:base + QK_ROPE, :].astype(w_scr.dtype)
        w_scr[base + LANES + half:base + LANES + QK_ROPE, :] = w_ref[base:base + half, :].astype(w_scr.dtype)

    acc = _dot_nt(h_ref[...], w_scr[...])
    qn_ref[...] = _rms(acc[:, :Q_LORA], gq_ref[...]).astype(qn_ref.dtype)
    ckv = _rms(acc[:, Q_LORA:base], gkv_ref[...])
    ckv_ref[...] = ckv
    ckvb_ref[...] = ckv.astype(ckvb_ref.dtype)
    kr = acc[:, base:base + LANES] * cos_ref[...] + acc[:, base + LANES:base + 2 * LANES] * sin_ref[...]
    krot_ref[...] = kr[:, :QK_ROPE]
    krotb_ref[...] = kr.astype(krotb_ref.dtype)


def _latent(h, w_t, gq, gkv, cos_t, sin_t, layer, tm):
    rows, d = h.shape
    n_in = Q_LORA + KV_LORA + QK_ROPE
    row = lambda width: pl.BlockSpec((tm, width), lambda i: (i, 0))
    const = lambda r, width: pl.BlockSpec((r, width), lambda i: (0, 0))
    return pl.pallas_call(
        _latent_body,
        grid=(rows // tm,),
        in_specs=[row(d), pl.BlockSpec((None, n_in, d), lambda i: (layer, 0, 0)),
                  const(1, Q_LORA), const(1, KV_LORA), row(LANES), row(LANES)],
        out_specs=[row(Q_LORA), row(KV_LORA), row(KV_LORA), row(QK_ROPE), row(LANES)],
        out_shape=[jax.ShapeDtypeStruct((rows, Q_LORA), BF16),
                   jax.ShapeDtypeStruct((rows, KV_LORA), F32),
                   jax.ShapeDtypeStruct((rows, KV_LORA), BF16),
                   jax.ShapeDtypeStruct((rows, QK_ROPE), F32),
                   jax.ShapeDtypeStruct((rows, LANES), BF16)],
        scratch_shapes=[pltpu.VMEM((Q_LORA + KV_LORA + 2 * LANES, d), BF16)],
        compiler_params=_params("arbitrary"),
        name="latent_proj",
    )(h, w_t, gq.reshape(1, -1), gkv.reshape(1, -1), cos_t, sin_t)


def _attn_body(q_ref, k_ref, v_ref, o_ref, *, tq, nq):
    k_chunk = lax.broadcasted_iota(jnp.int32, (tq, tq), 0) // CHUNK
    q_chunk = lax.broadcasted_iota(jnp.int32, (tq, tq), 1) // CHUNK
    visible = k_chunk <= q_chunk
    v_t = v_ref[...].T

    scores = []
    for n_past in range(nq):
        lo = n_past * tq
        q = q_ref[lo:lo + tq, :]
        s_d = jnp.where(visible, _dot_nt(k_ref[lo:lo + tq, :], q), NEG_BIG)
        s_p = _dot_nt(k_ref[0:lo, :], q) if n_past else None
        scores.append((s_d, s_p))
    probs = []
    for s_d, s_p in scores:
        m = jnp.max(s_d, axis=0, keepdims=True)
        if s_p is not None:
            m = jnp.maximum(m, jnp.max(s_p, axis=0, keepdims=True))
        p_d = jnp.exp2(s_d - m)
        l = jnp.sum(p_d, axis=0, keepdims=True)
        p_p = None
        if s_p is not None:
            p_p = jnp.exp2(s_p - m)
            l = l + jnp.sum(p_p, axis=0, keepdims=True)
            p_p = p_p.astype(BF16)
        probs.append((p_d.astype(BF16), p_p, l))
    for n_past, (p_d, p_p, l) in enumerate(probs):
        lo = n_past * tq
        acc = _dot(v_t[:, lo:lo + tq], p_d)
        if p_p is not None:
            acc = acc + _dot(v_t[:, 0:lo], p_p)
        o_ref[lo:lo + tq, :] = (acc / l).T.astype(o_ref.dtype)


def _prompt_attention(q, k, v, batch, seq, tq):
    return pl.pallas_call(
        functools.partial(_attn_body, tq=tq, nq=seq // tq),
        grid=(batch, MLA_HEADS),
        in_specs=[pl.BlockSpec((seq, QK_PAD), lambda b, h: (b, h)),
                  pl.BlockSpec((seq, QK_PAD), lambda b, h: (b, h)),
                  pl.BlockSpec((seq, V_HEAD), lambda b, h: (b, h))],
        out_specs=pl.BlockSpec((seq, V_HEAD), lambda b, h: (b, h)),
        out_shape=jax.ShapeDtypeStruct((batch * seq, MLA_HEADS * V_HEAD), BF16),
        compiler_params=_params("parallel", "parallel"),
        name="prompt_attention",
    )(q, k, v)


def _sattn_body(q_ref, cn_ref, kn_ref, cc_ref, ck_ref, wuk_ref, wuv_ref, o_ref, qa_scr, qp_scr,
                *, dec_seq, key_tile):
    for h in range(MLA_HEADS):
        qn = q_ref[:, h * QK_PAD:h * QK_PAD + QK_NOPE]
        qa_scr[h * dec_seq:(h + 1) * dec_seq, :] = _dot(qn, wuk_ref[h]).astype(BF16)
        qp_scr[h * dec_seq:(h + 1) * dec_seq, :] = q_ref[:, h * QK_PAD + QK_NOPE:(h + 1) * QK_PAD]
    qa = qa_scr[...]
    qp = qp_scr[...]
    rows = MLA_HEADS * dec_seq

    def update(carry, s, c):
        m, l, acc = carry
        m_new = jnp.maximum(m, jnp.max(s, axis=-1, keepdims=True))
        alpha = jnp.exp2(m - m_new)
        p = jnp.exp2(s - m_new)
        l = alpha * l + jnp.sum(p, axis=-1, keepdims=True)
        acc = alpha * acc + _dot(p.astype(BF16), c)
        return m_new, l, acc

    carry = (jnp.full((rows, 1), NEG_BIG, F32), jnp.zeros((rows, 1), F32), jnp.zeros((rows, KV_LORA), F32))
    past = cc_ref.shape[0]
    for t in range(past // key_tile):
        c = cc_ref[t * key_tile:(t + 1) * key_tile, :].astype(BF16)
        kp_t = ck_ref[:, t * key_tile:(t + 1) * key_tile].astype(BF16)
        s = _dot_nt(qa, c) + _dot(qp[:, :QK_ROPE], kp_t)
        carry = update(carry, s, c)
    cn = cn_ref[...]
    s = _dot_nt(qa, cn) + _dot_nt(qp, kn_ref[...])
    _, l, acc = update(carry, s, cn)
    o_lat = (acc / l).astype(BF16)
    for h in range(MLA_HEADS):
        o_ref[:, h * V_HEAD:(h + 1) * V_HEAD] = _dot(
            o_lat[h * dec_seq:(h + 1) * dec_seq, :], wuv_ref[h]).astype(o_ref.dtype)


def _sample_attention(q, ckv_b, krot_b, cache_c, cache_k, wuk_t, wuv_h, layer, n_prompt_rows, dec_batch, dec_seq):
    rb = n_prompt_rows // dec_seq
    past = cache_c.shape[2]
    key_tile = min(past, 1024)
    per_layer = lambda s: pl.BlockSpec((None,) + s, lambda b: (layer, 0, 0, 0))
    return pl.pallas_call(
        functools.partial(_sattn_body, dec_seq=dec_seq, key_tile=key_tile),
        grid=(dec_batch,),
        in_specs=[pl.BlockSpec((dec_seq, MLA_HEADS * QK_PAD), lambda b: (rb + b, 0)),
                  pl.BlockSpec((dec_seq, KV_LORA), lambda b: (rb + b, 0)),
                  pl.BlockSpec((dec_seq, LANES), lambda b: (rb + b, 0)),
                  pl.BlockSpec((None, None, past, KV_LORA), lambda b: (layer, b, 0, 0)),
                  pl.BlockSpec((None, None, QK_ROPE, past), lambda b: (layer, b, 0, 0)),
                  per_layer((MLA_HEADS, QK_NOPE, KV_LORA)),
                  per_layer((MLA_HEADS, KV_LORA, V_HEAD))],
        out_specs=pl.BlockSpec((dec_seq, MLA_HEADS * V_HEAD), lambda b: (b, 0)),
        out_shape=jax.ShapeDtypeStruct((dec_batch * dec_seq, MLA_HEADS * V_HEAD), BF16),
        scratch_shapes=[pltpu.VMEM((MLA_HEADS * dec_seq, KV_LORA), BF16),
                        pltpu.VMEM((MLA_HEADS * dec_seq, LANES), BF16)],
        compiler_params=_params("parallel"),
        name="sample_attention",
    )(q, ckv_b, krot_b, cache_c, cache_k, wuk_t, wuv_h)


def _hgrn_tables(length):
    t = np.arange(length)[:, None]
    r = np.arange(length)[None, :]
    groups = [(r <= t)]
    masks = [(r == t)]
    m = length // 2
    while m >= 1:
        blk = t // (2 * m)
        start2 = blk * 2 * m + m
        second = (t % (2 * m)) >= m
        incl = second & (r >= start2) & (r <= t)
        excl = (~second) & (r > t) & (r < start2)
        if m % 8:
            groups.append(incl | excl)
        r_blk = r // (2 * m)
        r_first = (r % (2 * m)) < m
        masks.append(second & r_first & (r_blk == blk))
        m //= 2
    return (np.concatenate(groups, axis=0).astype(np.float32),
            np.stack(masks, axis=0).astype(np.float32))


def _hgrn_body(*refs, length, n_chunks, group, has_state):
    if has_state:
        q_ref, k_ref, lf_ref, v_ref, g_ref, gn_ref, sum_ref, mask_ref, s0_ref, o_ref, sout_ref, st_scr = refs
    else:
        q_ref, k_ref, lf_ref, v_ref, g_ref, gn_ref, sum_ref, mask_ref, o_ref, sout_ref, st_scr = refs
    tb = pl.program_id(2)

    @pl.when(tb == 0)
    def _():
        for g in range(group):
            st_scr[g] = s0_ref[g].T if has_state else jnp.zeros((HG_DV, HG_DK), F32)

    n_lev = mask_ref.shape[0] - 1
    summat = sum_ref[...]
    gn = gn_ref[...]

    pack_levels = length % LANES == 0 and (n_lev + 1) % 2 == 0
    zeros_k = jnp.zeros((length, HG_DK), BF16)

    def mixed_rows(q, k, half):
        parts = []
        for r in range(0, length, half):
            parts.append((q if (r // half) % 2 else k)[r:r + half])
        return jnp.concatenate(parts, axis=0)

    halves = [length >> (lev + 1) for lev in range(n_lev)]
    coarse = [half for half in halves if half % 8 == 0]

    def exponentials(sums):
        b = sums[0:length]
        levels = []
        for half in coarse:
            parts = []
            for r in range(0, length, half):
                ref = (r // (2 * half)) * 2 * half + half - 1
                ref_row = b[ref:ref + 1]
                parts.append(b[r:r + half] - ref_row if (r // half) % 2 else ref_row - b[r:r + half])
            levels.append(jnp.concatenate(parts, axis=0))
        for i in range(n_lev - len(coarse)):
            levels.append(sums[(1 + i) * length:(2 + i) * length])
        e_in = jnp.exp2(b)
        e_out = jnp.exp2(b[length - 1:length] - b)
        return e_in, e_out, [jnp.exp2(x) for x in levels]

    def level_operands(rows, g, e_in, e_out, e_levels):
        cols = slice(g * HG_DK, (g + 1) * HG_DK)
        qs = [q_ref[rows, cols]]
        ks = [k_ref[rows, cols]]
        q = qs[0].astype(F32)
        k = ks[0].astype(F32)
        for half, e in zip(halves, e_levels):
            if half % 8 == 0:
                u = (mixed_rows(q, k, half) * e).astype(BF16)
                qs.append(u)
                ks.append(u)
            else:
                qs.append((q * e).astype(BF16))
                ks.append((k * e).astype(BF16))
        return qs, ks, (q * e_in).astype(BF16), (k * e_out).astype(BF16)

    def level_scores(qs, ks):
        if pack_levels:
            out = []
            for p in range(0, n_lev + 1, 2):
                lhs = jnp.concatenate([qs[p], qs[p + 1]], axis=1)
                rhs = jnp.concatenate([jnp.concatenate([ks[p], zeros_k], axis=1),
                                       jnp.concatenate([zeros_k, ks[p + 1]], axis=1)], axis=0)
                a = _dot_nt(lhs, rhs)
                out += [a[:, :length], a[:, length:]]
            return out
        return [_dot_nt(qs[p], ks[p]) for p in range(n_lev + 1)]

    def chunk(c, _):
        rows = pl.ds(pl.multiple_of(c * length, length), length)
        e_heads = []
        for pair in range(group // 2):
            cols2 = slice(2 * pair * HG_DK, (2 * pair + 2) * HG_DK)
            lf2 = lf_ref[rows, cols2] * LOG2_E
            hi = lf2.astype(BF16)
            mid = (lf2 - hi.astype(F32)).astype(BF16)
            e_in, e_out, e_levels = exponentials(_dot(summat, jnp.concatenate([hi, mid], axis=0)))
            for lanes in (slice(0, HG_DK), slice(HG_DK, 2 * HG_DK)):
                e_heads.append((e_in[:, lanes], e_out[:, lanes], [e[:, lanes] for e in e_levels]))
        work = []
        for g in range(group):
            qs, ks, q_in, k_out = level_operands(rows, g, *e_heads[g])
            st = st_scr[g]
            work.append((level_scores(qs, ks), _dot_nt(q_in, st.astype(BF16)), k_out, st))
        for g in range(group):
            cols = slice(g * HG_DK, (g + 1) * HG_DK)
            scores, o_in, k_out, st = work[g]
            v = v_ref[rows, cols]
            att = mask_ref[0] * scores[0]
            for p in range(1, n_lev + 1):
                att = att + mask_ref[p] * scores[p]
            o = o_in + _dot(att.astype(BF16), v)
            e_last = e_heads[g][0][length - 1:length, :]
            st_scr[g] = st * e_last + _dot_tn(v, k_out)
            o_ref[rows, cols] = (_rms(o, gn) * g_ref[rows, cols]).astype(o_ref.dtype)
        return 0

    lax.fori_loop(0, n_chunks, chunk, 0, unroll=4 if n_chunks % 4 == 0 else 1)

    @pl.when(tb == pl.num_programs(2) - 1)
    def _():
        for g in range(group):
            sout_ref[g] = st_scr[g].T


def _hgrn(q, k, logf, v, gate, gn, state0, *, layer, row_block0, n_streams, stream_len, length, block_len, group):
    n_tb = stream_len // block_len
    n_chunks = block_len // length
    width = group * HG_DK
    summat, masks = _hgrn_tables(length)
    summat = jnp.asarray(np.concatenate([summat, summat], axis=1), BF16)
    masks = jnp.asarray(masks, F32)
    tok = lambda col0: pl.BlockSpec((block_len, width),
                                    lambda n, h, t: (row_block0 + n * n_tb + t, col0 // width + h))
    tokens = (q, k, logf, v, gate)
    st_spec = pl.BlockSpec((None, group, HG_DK, HG_DV), lambda n, h, t: (n, h, 0, 0))
    in_specs = [tok(col0) for _, col0 in tokens] + [pl.BlockSpec((1, HG_DV), lambda n, h, t: (0, 0)),
                pl.BlockSpec(summat.shape, lambda n, h, t: (0, 0)),
                pl.BlockSpec(masks.shape, lambda n, h, t: (0, 0, 0))]
    args = [arr for arr, _ in tokens] + [gn.reshape(1, HG_DV), summat, masks]
    if state0 is not None:
        in_specs.append(pl.BlockSpec((None, None, group, HG_DK, HG_DV), lambda n, h, t: (layer, n, h, 0, 0)))
        args.append(state0)
    return pl.pallas_call(
        functools.partial(_hgrn_body, length=length, n_chunks=n_chunks, group=group,
                          has_state=state0 is not None),
        grid=(n_streams, HG_HEADS // group, n_tb),
        in_specs=in_specs,
        out_specs=[pl.BlockSpec((block_len, width), lambda n, h, t: (n * n_tb + t, h)), st_spec],
        out_shape=[jax.ShapeDtypeStruct((n_streams * stream_len, HG_HEADS * HG_DV), BF16),
                   jax.ShapeDtypeStruct((n_streams, HG_HEADS, HG_DK, HG_DV), F32)],
        scratch_shapes=[pltpu.VMEM((group, HG_DV, HG_DK), F32)],
        compiler_params=_params("parallel", "parallel", "arbitrary"),
        name="hgrn_state" if state0 is not None else "hgrn_prompt",
    )(*args)


def _merge_body(*refs, tm, n_a):
    oa_parts, oh_parts = refs[:n_a], refs[n_a:-5]
    woa_ref, wob_ref, ga_ref, gb_ref, o_ref = refs[-5:]

    def emit(tiles):
        a = _dot(tiles[0], woa_ref[...])
        b = _dot(tiles[1], wob_ref[...])
        o_ref[...] = (ga_ref[...] * a + gb_ref[...] * b).astype(o_ref.dtype)

    _on_row_tiles([oa_parts, oh_parts], 1, tm, emit)


def _merge(oa_parts, oh_parts, woa, wob, ga, gb, layer, tm, tn):
    rows, k = _stacked_rows(oa_parts), oa_parts[0].shape[1]
    n = woa.shape[2]
    x_specs = lambda parts: _part_specs(parts, tm, lambda j, i: (i, 0), lambda j, i: (0, 0))
    w_spec = pl.BlockSpec((None, k, tn), lambda j, i: (layer, 0, j))
    t_spec = pl.BlockSpec((tm, tn), lambda j, i: (i, j))
    return pl.pallas_call(
        functools.partial(_merge_body, tm=tm, n_a=len(oa_parts)),
        grid=(n // tn, rows // tm),
        in_specs=x_specs(oa_parts) + x_specs(oh_parts) + [w_spec, w_spec]
        + [pl.BlockSpec((tm, tn), lambda j, i, c=col0 // tn: (i, c + j)) for _, col0 in (ga, gb)],
        out_specs=t_spec,
        out_shape=jax.ShapeDtypeStruct((rows, n), BF16),
        compiler_params=_params("parallel", "arbitrary"),
        name="gated_merge",
    )(*oa_parts, *oh_parts, woa, wob, ga[0], gb[0])


def _outproj_body(m_ref, w_ref, *refs, tm):
    x_parts, (g1_ref, g2_ref, x1_ref, h2_ref) = refs[:-4], refs[-4:]

    def emit(tiles):
        rc = tm // 2
        for c in range(2):
            rows = slice(c * rc, (c + 1) * rc)
            y = _dot(m_ref[rows, :], w_ref[...])
            x1 = tiles[0][rows, :] + _rms(y, g1_ref[...])
            x1_ref[rows, :] = x1
            h2_ref[rows, :] = _rms(x1, g2_ref[...]).astype(h2_ref.dtype)

    _on_row_tiles([x_parts], 0, tm, emit)


def _outproj(mix, w, x_parts, g_post, g_pre2, layer, tm):
    rows, d = mix.shape
    row = pl.BlockSpec((tm, d), lambda i: (i, 0))
    vec = pl.BlockSpec((1, d), lambda i: (0, 0))
    return pl.pallas_call(
        functools.partial(_outproj_body, tm=tm),
        grid=(rows // tm,),
        in_specs=[row, pl.BlockSpec((None, d, d), lambda i: (layer, 0, 0))]
        + _part_specs(x_parts, tm, lambda i: (i, 0), lambda i: (0, 0)) + [vec, vec],
        out_specs=[row, row],
        out_shape=[jax.ShapeDtypeStruct((rows, d), F32), jax.ShapeDtypeStruct((rows, d), BF16)],
        compiler_params=_params("arbitrary"),
        name="out_proj",
    )(mix, w, *x_parts, g_post.reshape(1, d), g_pre2.reshape(1, d))


def _mlp_body(h_ref, wu_ref, wd_ref, x_ref, g_ref, *rest, tail_start, norm_next):
    rest = list(rest)
    gn_ref = rest.pop(0) if norm_next else None
    o_ref = rest.pop(0)
    tail_ref = rest.pop(0) if tail_start is not None else None
    hn_ref = rest.pop(0) if norm_next else None
    acc_ref = rest.pop(0)
    f = pl.program_id(1)

    @pl.when(f == 0)
    def _():
        acc_ref[...] = jnp.zeros_like(acc_ref)

    u = jnp.maximum(_dot(h_ref[...], wu_ref[...]), 0.0)
    acc_ref[...] += _dot((u * u).astype(BF16), wd_ref[...])

    @pl.when(f == pl.num_programs(1) - 1)
    def _():
        y = x_ref[...] + _rms(acc_ref[...], g_ref[...])
        o_ref[...] = y
        if norm_next:
            hn_ref[...] = _rms(y, gn_ref[...]).astype(hn_ref.dtype)
        if tail_start is not None:
            @pl.when(pl.program_id(0) == pl.num_programs(0) - 1)
            def _():
                tail_ref[...] = y[tail_start:, :]


def _mlp(h2, w_up, w_down, x1, g, layer, tm, tf, split_rows=None, next_g=None):
    rows, d = x1.shape
    ff = w_up.shape[2]
    row = pl.BlockSpec((tm, d), lambda i, f: (i, 0))
    vec = pl.BlockSpec((1, d), lambda i, f: (0, 0))
    out_specs, out_shape, tail_start = [row], [jax.ShapeDtypeStruct((rows, d), F32)], None
    if split_rows is not None:
        tail = rows - split_rows
        tail_start = tm - tail
        assert 0 <= tail_start and tail % 8 == 0
        out_specs.append(pl.BlockSpec((tail, d), lambda i, f: (0, 0)))
        out_shape = [jax.ShapeDtypeStruct((split_rows, d), F32), jax.ShapeDtypeStruct((tail, d), F32)]
    in_specs = [row, pl.BlockSpec((None, d, tf), lambda i, f: (layer, 0, f)),
                pl.BlockSpec((None, tf, d), lambda i, f: (layer, f, 0)), row, vec]
    args = [h2, w_up, w_down, x1, g.reshape(1, d)]
    if next_g is not None:
        in_specs.append(vec)
        args.append(next_g.reshape(1, d))
        out_specs.append(row)
        out_shape.append(jax.ShapeDtypeStruct((rows, d), BF16))
    return pl.pallas_call(
        functools.partial(_mlp_body, tail_start=tail_start, norm_next=next_g is not None),
        grid=(rows // tm, ff // tf),
        in_specs=in_specs,
        out_specs=out_specs,
        out_shape=out_shape,
        scratch_shapes=[pltpu.VMEM((tm, d), F32)],
        compiler_params=_params("arbitrary", "arbitrary"),
        name="mlp",
    )(*args)


def _pad_lanes(w):
    return jnp.concatenate([w, jnp.zeros(w.shape[:-1] + (LANES - w.shape[-1],), w.dtype)], axis=-1)


def _swap_halves(w):
    half = w.shape[-1] // 2
    return jnp.concatenate([w[..., half:], w[..., :half]], axis=-1)


def _prep_weights(w_uq, w_uk, w_uv):
    depth = w_uq.shape[0]
    uq = w_uq.reshape(depth, Q_LORA, MLA_HEADS // 2, 2, QK_HEAD)
    nope = uq[..., :QK_NOPE]
    pe = _pad_lanes(uq[..., QK_NOPE:])
    pe_sw = _pad_lanes(_swap_halves(uq[..., QK_NOPE:]))
    w_q = jnp.concatenate([nope[:, :, :, 0], pe[:, :, :, 0], nope[:, :, :, 1], pe[:, :, :, 1],
                           pe_sw[:, :, :, 0], pe_sw[:, :, :, 1]], axis=-1)
    w_q = w_q.reshape(depth, Q_LORA, -1).astype(BF16)
    uk = w_uk.reshape(depth, KV_LORA, MLA_HEADS // 2, 2 * QK_NOPE)
    uv = w_uv.reshape(depth, KV_LORA, MLA_HEADS // 2, 2 * V_HEAD)
    w_kv = jnp.concatenate([uk, uv], axis=-1).reshape(depth, KV_LORA, -1).astype(BF16)
    wuk_t = jnp.transpose(w_uk, (0, 2, 3, 1)).astype(BF16)
    wuv_h = jnp.transpose(w_uv, (0, 2, 1, 3)).astype(BF16)
    return w_q, w_kv, wuk_t, wuv_h


def _rope_tables(positions):
    half = QK_ROPE // 2
    inv = ROPE_THETA ** (-jnp.arange(half, dtype=F32) / half)
    ang = positions.astype(F32)[:, None] * inv[None, :]
    cos, sin = jnp.cos(ang), jnp.sin(ang)
    zeros = jnp.zeros((positions.shape[0], LANES - QK_ROPE), F32)
    return (jnp.concatenate([cos, cos, zeros], axis=1), jnp.concatenate([-sin, sin, zeros], axis=1))


def kernel(x_prompt, x_sample, cache_ckv, cache_kpe, state_hgrn, pre_mix_g, w_in, q_norm_g, w_uq, kv_norm_g,
           w_uk, w_uv, w_oa, hg_lb, hg_norm_g, w_ob, w_out, post_mix_g, pre_mlp_g, w_up, w_down, post_mlp_g):
    batch, seq, d = x_prompt.shape
    dec_batch, dec_seq, _ = x_sample.shape
    depth = w_in.shape[0]
    past = cache_ckv.shape[2]
    n_p = batch * seq
    n_s = dec_batch * dec_seq
    rows = n_p + n_s
    tm = 768
    tm_wide = 1408
    tm_mlp = 704
    assert rows % tm == 0 and rows % tm_wide == 0 and rows % tm_mlp == 0 and n_p % 1024 == 0

    x_parts = (x_prompt.reshape(n_p, d), x_sample.reshape(n_s, d))
    pos = jnp.concatenate([jnp.tile(jnp.arange(seq, dtype=jnp.int32), batch),
                           jnp.tile(past + jnp.arange(dec_seq, dtype=jnp.int32), dec_batch)])
    cos_t, sin_t = _rope_tables(pos)
    cos_q, sin_q = cos_t * Q_SCALE, sin_t * Q_SCALE

    w_in_t = jnp.transpose(w_in, (0, 2, 1))
    cache_kpe_t = jnp.transpose(cache_kpe, (0, 1, 3, 2))
    wide0 = Q_LORA + KV_LORA + QK_ROPE
    w_q, w_kv, wuk_t, wuv_h = _prep_weights(w_uq, w_uk, w_uv)
    w_oa_b, w_ob_b, w_out_b = w_oa.astype(BF16), w_ob.astype(BF16), w_out.astype(BF16)
    w_up_b, w_down_b = w_up.astype(BF16), w_down.astype(BF16)

    ckv_out, kpe_out, st_p_out, st_s_out = [], [], [], []
    for l in range(depth):
        h = _rmsnorm(x_parts, pre_mix_g[l], tm)
        qn, ckv, ckv_b, krot, krot_b = _latent(h, w_in_t, q_norm_g[l], kv_norm_g[l], cos_t, sin_t, l, tm)
        hk, logf = _mm(h, w_in_t, [functools.partial(_epi_forget, layer=l)], [(1024, BF16), (1024, F32)],
                       layer=l, row0s=[wide0 + d], n=d, tm=tm, tn=1024, col_args=[(hg_lb, 1024)], name="proj_hf")
        hq, hv, hgate, ga, gb = [
            (*_mm(h, w_in_t, [epi], [(1024, BF16)], layer=l, row0s=[wide0 + g * d], n=d, tm=tm_wide, tn=1024,
                  row_chunks=4, name=name), 0)
            for g, epi, name in ((0, _epi_silu, "proj_hq"), (2, _epi_id, "proj_hi"), (3, _epi_silu, "proj_hg"),
                                 (4, _epi_sigmoid, "proj_ga"), (5, _epi_sigmoid, "proj_gb"))]
        hk, logf = [hk, 0], [logf, 0]
        (q,) = _mm_rows(qn, w_q, _epi_q, [(2 * QK_PAD, BF16)], layer=l, tm=tm, tn=6 * LANES,
                        tile_args=[(cos_q, LANES), (sin_q, LANES)], name="proj_q")
        keys, vals = _mm_rows(ckv_b, w_kv, _epi_kv, [(2 * QK_PAD, BF16), (2 * V_HEAD, BF16)], layer=l, tm=1024,
                              tn=4 * LANES, rows=n_p, tile_args=[(krot_b, LANES)], name="proj_kv")
        oa_p = _prompt_attention(q, keys, vals, batch, seq, 256)
        oa_s = _sample_attention(q, ckv_b, krot_b, cache_ckv, cache_kpe_t, wuk_t, wuv_h, l, n_p, dec_batch, dec_seq)
        oh_p, st_p = _hgrn(hq, hk, logf, hv, hgate, hg_norm_g[l], None, layer=l,
                           row_block0=0, n_streams=batch, stream_len=seq, length=2 * CHUNK,
                           block_len=1024, group=8)
        oh_s, st_s = _hgrn(hq, hk, logf, hv, hgate, hg_norm_g[l], state_hgrn, layer=l,
                           row_block0=n_p // dec_seq, n_streams=dec_batch, stream_len=dec_seq, length=dec_seq,
                           block_len=dec_seq, group=4)
        mix = _merge((oa_p, oa_s), (oh_p, oh_s), w_oa_b, w_ob_b, ga, gb, l, tm, 1024)
        x1, h2 = _outproj(mix, w_out_b, x_parts, post_mix_g[l], pre_mlp_g[l], l, 384)
        if l + 1 < depth:
            x_parts = tuple(_mlp(h2, w_up_b, w_down_b, x1, post_mlp_g[l], l, tm_mlp, 1024))
        else:
            y_p, y_s = _mlp(h2, w_up_b, w_down_b, x1, post_mlp_g[l], l, tm, 512, split_rows=n_p)

        ckv_out.append(ckv)
        kpe_out.append(krot)
        st_p_out.append(st_p)
        st_s_out.append(st_s)

    ckv_all = jnp.stack(ckv_out)
    kpe_all = jnp.stack(kpe_out)
    return (y_p.reshape(batch, seq, d),
            y_s.reshape(dec_batch, dec_seq, d),
            ckv_all[:, :n_p].reshape(depth, batch, seq, KV_LORA),
            kpe_all[:, :n_p].reshape(depth, batch, seq, QK_ROPE),
            jnp.stack(st_p_out),
            ckv_all[:, n_p:].reshape(depth, dec_batch, dec_seq, KV_LORA),
            kpe_all[:, n_p:].reshape(depth, dec_batch, dec_seq, QK_ROPE),
            jnp.stack(st_s_out))
```
